```python
import math
import jax, jax.numpy as jnp
from jax import lax
import numpy as np

D_MODEL = 1024
BATCH = 8
SEQ = 2048
DEPTH = 1
DEC_BATCH = 128
DEC_SEQ = 1
PAST_LEN = 16384
PAGE_SIZE = 128

RWKV_WIDTH = D_MODEL // 2
RWKV_HEAD = 64
RWKV_HEADS = RWKV_WIDTH // RWKV_HEAD
DECAY_LORA = 64
AAA_LORA = 64
GATE_LORA = 128
LNX_EPS = 64e-5
S5_WIDTH = D_MODEL // 2
S5_GROUP = 16
S5_GROUPS = S5_WIDTH // S5_GROUP
S5_STATE = 64
FFN_DIM = 2816
NORM_EPS = 1e-6
N_BRANCH = 2
RWKV_COLS = 3 * RWKV_WIDTH + DECAY_LORA + AAA_LORA + GATE_LORA
IN_COLS = RWKV_COLS + S5_WIDTH + N_BRANCH * D_MODEL

kernel_name = 'rwkv7_s5_gated_macaron_step'


def rmsnorm(x, g):
    xf = x.astype(jnp.float32)
    y = xf * lax.rsqrt(jnp.mean(xf * xf, axis=-1, keepdims=True) + NORM_EPS)
    return (y * g.astype(jnp.float32)).astype(x.dtype)


def swiglu(x, w_gate, w_up, w_down):
    return (jax.nn.silu(x @ w_gate) * (x @ w_up)) @ w_down


def token_shift(f, prev):
    return jnp.concatenate([prev[:, None, :], f[:, :-1, :]], axis=1)


def rwkv7_recurrence(r, w, k, v, kk, b, S0):
    def step(S, inp):
        r_t, w_t, k_t, v_t, kk_t, b_t = inp
        sa = jnp.einsum('bhvk,bhk->bhv', S, -kk_t)
        S = (S * w_t[:, :, None, :] + sa[..., None] * b_t[:, :, None, :]
             + v_t[..., None] * k_t[:, :, None, :])
        o = jnp.einsum('bhvk,bhk->bhv', S, r_t)
        return S, o
    xs = tuple(jnp.moveaxis(t, 1, 0) for t in (r, w, k, v, kk, b))
    S, o = lax.scan(step, S0, xs)
    return jnp.moveaxis(o, 0, 1), S


def rwkv7_branch(f, S0, p):
    B, L, _ = f.shape
    H, N, W = RWKV_HEADS, RWKV_HEAD, RWKV_WIDTH
    f = f.astype(jnp.float32)
    r = f[..., 0:W]
    k = f[..., W:2 * W]
    v = f[..., 2 * W:3 * W]
    o0 = 3 * W
    wd = f[..., o0:o0 + DECAY_LORA]
    ad = f[..., o0 + DECAY_LORA:o0 + DECAY_LORA + AAA_LORA]
    gd = f[..., o0 + DECAY_LORA + AAA_LORA:]
    w_log = -jax.nn.softplus(-(p['w0'] + jnp.tanh(wd) @ p['w_up'])) - 0.5
    decay = jnp.exp(-jnp.exp(w_log))
    a = jax.nn.sigmoid(p['a0'] + ad @ p['a_up'])
    g = jax.nn.sigmoid(gd) @ p['g_up']
    heads = lambda t: t.reshape(B, L, H, N)
    kk = heads(k * p['k_k'])
    kk = kk / jnp.maximum(jnp.sqrt(jnp.sum(kk * kk, axis=-1, keepdims=True)), 1e-12)
    k = k * (1.0 + (a - 1.0) * p['k_a'])
    r_h, k_h, v_h, a_h = heads(r), heads(k), heads(v), heads(a)
    o, S = rwkv7_recurrence(r_h, heads(decay), k_h, v_h, kk, kk * a_h, S0.astype(jnp.float32))
    mu = jnp.mean(o, axis=-1, keepdims=True)
    var = jnp.mean(jnp.square(o - mu), axis=-1, keepdims=True)
    o = ((o - mu) * lax.rsqrt(var + LNX_EPS)).reshape(B, L, W) * p['lnx_w'] + p['lnx_b']
    bonus = jnp.sum(r_h * k_h * p['r_k'], axis=-1, keepdims=True) * v_h
    return (o + bonus.reshape(B, L, W)) * g, S


def s5_branch(u, xr0, xi0, p):
    B, L, _ = u.shape
    f32 = jnp.float32
    uf = u.astype(f32)
    ug = uf.reshape(B, L, S5_GROUPS, S5_GROUP)
    lam_r = p['A_re'].astype(f32)
    lam_i = p['A_im'].astype(f32)
    dt = jnp.exp(p['log_dt'].astype(f32))[:, None]
    mag = jnp.exp(lam_r * dt)
    ab_r, ab_i = mag * jnp.cos(lam_i * dt), mag * jnp.sin(lam_i * dt)
    den = lam_r * lam_r + lam_i * lam_i
    q_r = ((ab_r - 1.0) * lam_r + ab_i * lam_i) / den
    q_i = (ab_i * lam_r - (ab_r - 1.0) * lam_i) / den
    B_re, B_im = p['B_re'].astype(f32), p['B_im'].astype(f32)
    bb_r = q_r[..., None] * B_re - q_i[..., None] * B_im
    bb_i = q_r[..., None] * B_im + q_i[..., None] * B_re
    bu_r = jnp.einsum('blgh,gph->blgp', ug, bb_r)
    bu_i = jnp.einsum('blgh,gph->blgp', ug, bb_i)
    xr0 = xr0.astype(f32)
    xi0 = xi0.astype(f32)
    bu_r = bu_r.at[:, 0].add(ab_r * xr0 - ab_i * xi0)
    bu_i = bu_i.at[:, 0].add(ab_r * xi0 + ab_i * xr0)
    a_r = jnp.broadcast_to(ab_r, (1, L, S5_GROUPS, S5_STATE))
    a_i = jnp.broadcast_to(ab_i, (1, L, S5_GROUPS, S5_STATE))

    def combine(e1, e2):
        a1r, a1i, b1r, b1i = e1
        a2r, a2i, b2r, b2i = e2
        return (a2r * a1r - a2i * a1i, a2r * a1i + a2i * a1r,
                a2r * b1r - a2i * b1i + b2r, a2r * b1i + a2i * b1r + b2i)

    _, _, xr, xi = lax.associative_scan(combine, (a_r, a_i, bu_r, bu_i), axis=1)
    C_re, C_im = p['C_re'].astype(f32), p['C_im'].astype(f32)
    y = jnp.einsum('blgp,ghp->blgh', xr, C_re) - jnp.einsum('blgp,ghp->blgh', xi, C_im)
    y = y.reshape(B, L, S5_WIDTH) + p['D_skip'] * uf
    z = jax.nn.gelu(y)
    return z * jax.nn.sigmoid(z @ p['w_glu']), xr[:, -1], xi[:, -1]


def hybrid_layer(x, shift0, wkv0, s5r0, s5i0, p):
    x = x + 0.5 * swiglu(rmsnorm(x, p['g_ffn1']), p['ffn1_gate'], p['ffn1_up'], p['ffn1_down'])
    h = rmsnorm(x, p['g_mix'])
    proj = h @ p['w_in']
    f_rwkv = proj[..., :RWKV_COLS]
    u = proj[..., RWKV_COLS:RWKV_COLS + S5_WIDTH]
    gates = jax.nn.sigmoid(proj[..., RWKV_COLS + S5_WIDTH:].astype(jnp.float32))
    shifted = token_shift(f_rwkv, shift0.astype(f_rwkv.dtype))
    f_mix = f_rwkv + p['mu_shift'] * (shifted - f_rwkv)
    y_a, wkv = rwkv7_branch(f_mix, wkv0, p)
    y_b, s5r, s5i = s5_branch(u, s5r0, s5i0, p)
    merged = gates[..., :D_MODEL] * (y_a @ p['w_a_up']) + gates[..., D_MODEL:] * (y_b @ p['w_b_up'])
    x = x + (merged @ p['w_out']).astype(x.dtype)
    x = x + 0.5 * swiglu(rmsnorm(x, p['g_ffn2']), p['ffn2_gate'], p['ffn2_up'], p['ffn2_down'])
    return x, f_rwkv[:, -1], wkv, s5r, s5i


def setup_inputs(seed: int = 0) -> dict:
    key = jax.random.key(seed)
    ks = iter(jax.random.split(key, 64))
    f32 = jnp.float32

    def nrm(shape, scale):
        return scale * jax.random.normal(next(ks), shape, f32)

    def uni(shape, lo, hi):
        return jax.random.uniform(next(ks), shape, f32, lo, hi)

    W = RWKV_WIDTH
    n_idx = jnp.arange(S5_STATE, dtype=f32)
    return {
        'x_prompt': nrm((BATCH, SEQ, D_MODEL), 1.0),
        'x_sample': nrm((DEC_BATCH, DEC_SEQ, D_MODEL), 1.0),
        'state_shift': nrm((DEPTH, DEC_BATCH, RWKV_COLS), 1.0),
        'state_wkv': nrm((DEPTH, DEC_BATCH, RWKV_HEADS, RWKV_HEAD, RWKV_HEAD), 0.5),
        'state_s5_re': nrm((DEPTH, DEC_BATCH, S5_GROUPS, S5_STATE), 0.5),
        'state_s5_im': nrm((DEPTH, DEC_BATCH, S5_GROUPS, S5_STATE), 0.5),
        'g_ffn1': 1.0 + nrm((DEPTH, D_MODEL), 0.02),
        'ffn1_gate': nrm((DEPTH, D_MODEL, FFN_DIM), D_MODEL ** -0.5),
        'ffn1_up': nrm((DEPTH, D_MODEL, FFN_DIM), D_MODEL ** -0.5),
        'ffn1_down': nrm((DEPTH, FFN_DIM, D_MODEL), FFN_DIM ** -0.5),
        'g_mix': 1.0 + nrm((DEPTH, D_MODEL), 0.02),
        'w_in': nrm((DEPTH, D_MODEL, IN_COLS), D_MODEL ** -0.5),
        'mu_shift': uni((DEPTH, RWKV_COLS), 0.0, 1.0),
        'w0': uni((DEPTH, W), -6.0, -1.0),
        'w_up': nrm((DEPTH, DECAY_LORA, W), 0.1),
        'a0': nrm((DEPTH, W), 0.5),
        'a_up': nrm((DEPTH, AAA_LORA, W), AAA_LORA ** -0.5),
        'g_up': nrm((DEPTH, GATE_LORA, W), GATE_LORA ** -0.5),
        'k_k': 0.85 + nrm((DEPTH, W), 0.05),
        'k_a': 1.0 + nrm((DEPTH, W), 0.05),
        'r_k': nrm((DEPTH, RWKV_HEADS, RWKV_HEAD), 0.1),
        'lnx_w': 1.0 + nrm((DEPTH, W), 0.02),
        'lnx_b': nrm((DEPTH, W), 0.01),
        'A_re': -0.5 * jnp.exp(nrm((DEPTH, S5_GROUPS, S5_STATE), 0.05)),
        'A_im': jnp.pi * n_idx + nrm((DEPTH, S5_GROUPS, S5_STATE), 0.02),
        'log_dt': uni((DEPTH, S5_GROUPS), math.log(0.001), math.log(0.1)),
        'B_re': nrm((DEPTH, S5_GROUPS, S5_STATE, S5_GROUP), (2 * S5_GROUP) ** -0.5),
        'B_im': nrm((DEPTH, S5_GROUPS, S5_STATE, S5_GROUP), (2 * S5_GROUP) ** -0.5),
        'C_re': nrm((DEPTH, S5_GROUPS, S5_GROUP, S5_STATE), (2 * S5_STATE) ** -0.5),
        'C_im': nrm((DEPTH, S5_GROUPS, S5_GROUP, S5_STATE), (2 * S5_STATE) ** -0.5),
        'D_skip': nrm((DEPTH, S5_WIDTH), 1.0),
        'w_glu': nrm((DEPTH, S5_WIDTH, S5_WIDTH), S5_WIDTH ** -0.5),
        'w_a_up': nrm((DEPTH, RWKV_WIDTH, D_MODEL), RWKV_WIDTH ** -0.5),
        'w_b_up': nrm((DEPTH, S5_WIDTH, D_MODEL), S5_WIDTH ** -0.5),
        'w_out': nrm((DEPTH, D_MODEL, D_MODEL), D_MODEL ** -0.5),
        'g_ffn2': 1.0 + nrm((DEPTH, D_MODEL), 0.02),
        'ffn2_gate': nrm((DEPTH, D_MODEL, FFN_DIM), D_MODEL ** -0.5),
        'ffn2_up': nrm((DEPTH, D_MODEL, FFN_DIM), D_MODEL ** -0.5),
        'ffn2_down': nrm((DEPTH, FFN_DIM, D_MODEL), FFN_DIM ** -0.5),
        'g_final': 1.0 + nrm((D_MODEL,), 0.02),
    }


def reference(x_prompt, x_sample, state_shift, state_wkv, state_s5_re, state_s5_im,
              g_ffn1, ffn1_gate, ffn1_up, ffn1_down, g_mix, w_in, mu_shift,
              w0, w_up, a0, a_up, g_up, k_k, k_a, r_k, lnx_w, lnx_b,
              A_re, A_im, log_dt, B_re, B_im, C_re, C_im, D_skip, w_glu,
              w_a_up, w_b_up, w_out, g_ffn2, ffn2_gate, ffn2_up, ffn2_down, g_final):
    p = {'g_ffn1': g_ffn1, 'ffn1_gate': ffn1_gate, 'ffn1_up': ffn1_up, 'ffn1_down': ffn1_down,
         'g_mix': g_mix, 'w_in': w_in, 'mu_shift': mu_shift,
         'w0': w0, 'w_up': w_up, 'a0': a0, 'a_up': a_up, 'g_up': g_up,
         'k_k': k_k, 'k_a': k_a, 'r_k': r_k, 'lnx_w': lnx_w, 'lnx_b': lnx_b,
         'A_re': A_re, 'A_im': A_im, 'log_dt': log_dt, 'B_re': B_re, 'B_im': B_im,
         'C_re': C_re, 'C_im': C_im, 'D_skip': D_skip, 'w_glu': w_glu,
         'w_a_up': w_a_up, 'w_b_up': w_b_up, 'w_out': w_out,
         'g_ffn2': g_ffn2, 'ffn2_gate': ffn2_gate, 'ffn2_up': ffn2_up, 'ffn2_down': ffn2_down}
    bp = x_prompt.shape[0]
    f32 = jnp.float32
    hp, hs = x_prompt, x_sample
    sh_p, wk_p, sr_p, si_p = [], [], [], []
    sh_s, wk_s, sr_s, si_s = [], [], [], []
    for layer in range(DEPTH):
        pl = {name: arr[layer] for name, arr in p.items()}
        hp, a1, a2, a3, a4 = hybrid_layer(
            hp,
            jnp.zeros((bp, RWKV_COLS), f32),
            jnp.zeros((bp, RWKV_HEADS, RWKV_HEAD, RWKV_HEAD), f32),
            jnp.zeros((bp, S5_GROUPS, S5_STATE), f32),
            jnp.zeros((bp, S5_GROUPS, S5_STATE), f32), pl)
        sh_p.append(a1); wk_p.append(a2); sr_p.append(a3); si_p.append(a4)
        hs, b1, b2, b3, b4 = hybrid_layer(
            hs, state_shift[layer], state_wkv[layer], state_s5_re[layer], state_s5_im[layer], pl)
        sh_s.append(b1); wk_s.append(b2); sr_s.append(b3); si_s.append(b4)
    y_prompt = rmsnorm(hp, g_final)
    y_sample = rmsnorm(hs, g_final)
    new_shift_prompt = jnp.stack(sh_p, 0).astype(state_shift.dtype)
    new_wkv_prompt = jnp.stack(wk_p, 0).astype(state_wkv.dtype)
    new_s5_re_prompt = jnp.stack(sr_p, 0).astype(state_s5_re.dtype)
    new_s5_im_prompt = jnp.stack(si_p, 0).astype(state_s5_im.dtype)
    new_shift_sample = jnp.stack(sh_s, 0).astype(state_shift.dtype)
    new_wkv_sample = jnp.stack(wk_s, 0).astype(state_wkv.dtype)
    new_s5_re_sample = jnp.stack(sr_s, 0).astype(state_s5_re.dtype)
    new_s5_im_sample = jnp.stack(si_s, 0).astype(state_s5_im.dtype)
    return (y_prompt, y_sample, new_shift_prompt, new_wkv_prompt, new_s5_re_prompt, new_s5_im_prompt,
            new_shift_sample, new_wkv_sample, new_s5_re_sample, new_s5_im_sample)
```

```python
import functools

import jax
import jax.numpy as jnp
from jax import lax
from jax.experimental import pallas as pl
from jax.experimental.pallas import tpu as pltpu

F32 = jnp.float32
BF16 = jnp.bfloat16

NORM_EPS = 1e-6
LNX_EPS = 64e-5
HEAD = 64
HEADS = 8
RW = HEAD * HEADS
S5_BLOCKS = 4
CHUNK = 64
VMEM_LIMIT = 56 * 1024 * 1024


def _dot(a, b):
    return jnp.dot(a, b, preferred_element_type=F32)


def _dot_nt(a, b):
    return lax.dot_general(a, b, (((1,), (1,)), ((), ())), preferred_element_type=F32)


def _dot_tn(a, b, precision=None):
    return lax.dot_general(a, b, (((0,), (0,)), ((), ())), preferred_element_type=F32,
                           precision=precision)


def _rms(x, g):
    ms = jnp.mean(x * x, axis=-1, keepdims=True)
    return x * lax.rsqrt(ms + NORM_EPS) * g


def _const_spec(shape):
    nd = len(shape)
    return pl.BlockSpec(shape, lambda *_: (0,) * nd, pipeline_mode=pl.Buffered(1))


def _cparams(ngrid):
    return pltpu.CompilerParams(dimension_semantics=("arbitrary",) * ngrid,
                                vmem_limit_bytes=VMEM_LIMIT)


def _ffn_body(x_ref, g_ref, wg_ref, wu_ref, wd_ref, gn_ref, *rest, final):
    if final:
        y_ref, hn_scr, acc_scr = rest
    else:
        x1_ref, hn_ref, hn_scr, acc_scr = rest
    j = pl.program_id(1)

    @pl.when(j == 0)
    def _():
        hn_scr[...] = _rms(x_ref[...], g_ref[...]).astype(BF16)
        acc_scr[...] = jnp.zeros_like(acc_scr)

    hn = hn_scr[...]
    gate = _dot(hn, wg_ref[...])
    up = _dot(hn, wu_ref[...])
    act = (jax.nn.silu(gate) * up).astype(BF16)
    acc_scr[...] += _dot(act, wd_ref[...])

    @pl.when(j == pl.num_programs(1) - 1)
    def _():
        x1 = x_ref[...] + 0.5 * acc_scr[...]
        if final:
            y_ref[...] = _rms(x1, gn_ref[...])
        else:
            x1_ref[...] = x1
            hn_ref[...] = _rms(x1, gn_ref[...]).astype(BF16)


def _ffn(x, g, wg, wu, wd, gn, *, final, tm, fc):
    m, d = x.shape
    f = wg.shape[1]
    assert m % tm == 0 and f % fc == 0
    if final:
        out_shape = jax.ShapeDtypeStruct((m, d), F32)
        out_specs = pl.BlockSpec((tm, d), lambda i, j: (i, 0))
    else:
        out_shape = [jax.ShapeDtypeStruct((m, d), F32), jax.ShapeDtypeStruct((m, d), BF16)]
        out_specs = [pl.BlockSpec((tm, d), lambda i, j: (i, 0)),
                     pl.BlockSpec((tm, d), lambda i, j: (i, 0))]
    return pl.pallas_call(
        functools.partial(_ffn_body, final=final),
        grid=(m // tm, f // fc),
        in_specs=[
            pl.BlockSpec((tm, d), lambda i, j: (i, 0)),
            pl.BlockSpec((1, d), lambda i, j: (0, 0)),
            pl.BlockSpec((d, fc), lambda i, j: (0, j)),
            pl.BlockSpec((d, fc), lambda i, j: (0, j)),
            pl.BlockSpec((fc, d), lambda i, j: (j, 0)),
            pl.BlockSpec((1, d), lambda i, j: (0, 0)),
        ],
        out_specs=out_specs,
        out_shape=out_shape,
        scratch_shapes=[pltpu.VMEM((tm, d), BF16), pltpu.VMEM((tm, d), F32)],
        compiler_params=_cparams(2),
        name="ffn_final" if final else "ffn_in",
    )(x, g, wg, wu, wd, gn)


def _s5_param_body(are_ref, aim_ref, ldt_ref, bre_ref, bim_ref, abr_ref, abi_ref, bbr_ref, bbi_ref):
    lam_r = are_ref[...]
    lam_i = aim_ref[...]
    dt = jnp.exp(ldt_ref[...])
    mag = jnp.exp(lam_r * dt)
    ab_r = mag * jnp.cos(lam_i * dt)
    ab_i = mag * jnp.sin(lam_i * dt)
    den = lam_r * lam_r + lam_i * lam_i
    q_r = ((ab_r - 1.0) * lam_r + ab_i * lam_i) / den
    q_i = (ab_i * lam_r - (ab_r - 1.0) * lam_i) / den
    abr_ref[...] = ab_r
    abi_ref[...] = ab_i
    bre = bre_ref[...]
    bim = bim_ref[...]
    bbr_ref[...] = q_r[:, None, :] * bre - q_i[:, None, :] * bim
    bbi_ref[...] = q_r[:, None, :] * bim + q_i[:, None, :] * bre


def _s5_params(a_re, a_im, log_dt, b_re_t, b_im_t):
    g, p = a_re.shape
    hg = b_re_t.shape[1]
    return pl.pallas_call(
        _s5_param_body,
        out_shape=[jax.ShapeDtypeStruct((g, p), F32), jax.ShapeDtypeStruct((g, p), F32),
                   jax.ShapeDtypeStruct((g, hg, p), F32), jax.ShapeDtypeStruct((g, hg, p), F32)],
        name="s5_params",
    )(a_re, a_im, log_dt.reshape(g, 1), b_re_t, b_im_t)


def _block_diag(w, nblk):
    g, a, b = w.shape
    gb = g // nblk
    w4 = w.reshape(nblk, gb, a, b)
    eye = jnp.eye(gb, dtype=w.dtype)
    return jnp.einsum('jgab,gk->jgakb', w4, eye).reshape(nblk, gb * a, gb * b)


def _s5_body(h_ref, wu_ref, bb_ref, cm_ref, abr_ref, abi_ref, dsk_ref, wglu_ref, x0r_ref, x0i_ref,
             yb_ref, xr_out, xi_out, bu_scr, y_scr, xr_scr, xi_scr, *, nb, steps):
    c = pl.program_id(0)

    @pl.when(c == 0)
    def _():
        xr_scr[...] = x0r_ref[...]
        xi_scr[...] = x0i_ref[...]

    u = _dot(h_ref[...], wu_ref[...])
    ub = u.astype(BF16)
    nin = ub.shape[1] // S5_BLOCKS
    ns = abr_ref.shape[1] // S5_BLOCKS
    for j in range(S5_BLOCKS):
        bu_scr[...] = _dot(ub[:, j * nin:(j + 1) * nin], bb_ref[j])
        ar = jnp.broadcast_to(abr_ref[:, j * ns:(j + 1) * ns], (nb, ns))
        ai = jnp.broadcast_to(abi_ref[:, j * ns:(j + 1) * ns], (nb, ns))

        def step(t, carry):
            xr, xi = carry
            r0 = pl.multiple_of(t * nb, nb)
            bur = bu_scr[pl.ds(r0, nb), 0:ns]
            bui = bu_scr[pl.ds(r0, nb), ns:2 * ns]
            nxr = ar * xr - ai * xi + bur
            nxi = ar * xi + ai * xr + bui
            bu_scr[pl.ds(r0, nb), 0:ns] = nxr
            bu_scr[pl.ds(r0, nb), ns:2 * ns] = nxi
            return nxr, nxi

        xr, xi = lax.fori_loop(0, steps, step,
                               (xr_scr[:, j * ns:(j + 1) * ns], xi_scr[:, j * ns:(j + 1) * ns]),
                               unroll=min(steps, 8))
        xr_scr[:, j * ns:(j + 1) * ns] = xr
        xi_scr[:, j * ns:(j + 1) * ns] = xi
        y_scr[:, j * nin:(j + 1) * nin] = _dot(bu_scr[...].astype(BF16), cm_ref[j])
    y = y_scr[...] + dsk_ref[...] * u
    z = jax.nn.gelu(y)
    yb_ref[...] = z * jax.nn.sigmoid(_dot(z.astype(BF16), wglu_ref[...]))
    xr_out[...] = xr_scr[...]
    xi_out[...] = xi_scr[...]


def _s5(h_tm, w_u, bb, cm, abr, abi, dsk, wglu, x0r, x0i, *, nb, steps):
    rows_total, d = h_tm.shape
    rows = nb * steps
    assert rows_total % rows == 0
    su = w_u.shape[1]
    nst = abr.shape[1]
    return pl.pallas_call(
        functools.partial(_s5_body, nb=nb, steps=steps),
        grid=(rows_total // rows,),
        in_specs=[
            pl.BlockSpec((rows, d), lambda c: (c, 0)),
            _const_spec(w_u.shape), _const_spec(bb.shape), _const_spec(cm.shape),
            _const_spec(abr.shape), _const_spec(abi.shape), _const_spec(dsk.shape),
            _const_spec(wglu.shape), _const_spec(x0r.shape), _const_spec(x0i.shape),
        ],
        out_specs=[pl.BlockSpec((rows, su), lambda c: (c, 0)),
                   pl.BlockSpec((nb, nst), lambda c: (0, 0)),
                   pl.BlockSpec((nb, nst), lambda c: (0, 0))],
        out_shape=[jax.ShapeDtypeStruct((rows_total, su), F32),
                   jax.ShapeDtypeStruct((nb, nst), F32),
                   jax.ShapeDtypeStruct((nb, nst), F32)],
        scratch_shapes=[pltpu.VMEM((rows, 2 * nst // S5_BLOCKS), F32),
                        pltpu.VMEM((rows, su), F32),
                        pltpu.VMEM((nb, nst), F32), pltpu.VMEM((nb, nst), F32)],
        compiler_params=_cparams(1),
        name="s5_branch",
    )(h_tm, w_u, bb, cm, abr, abi, dsk, wglu, x0r, x0i)


def _headsum(x, e):
    hi = x.astype(BF16)
    lo = (x - hi.astype(F32)).astype(BF16)
    return _dot(hi, e) + _dot(lo, e)


def _split3(x):
    hi = x.astype(BF16)
    r1 = x - hi.astype(F32)
    mid = r1.astype(BF16)
    lo = (r1 - mid.astype(F32)).astype(BF16)
    return hi, mid, lo


def _rwkv_pre(fm, w):
    r = fm[:, 0:RW]
    k = fm[:, RW:2 * RW]
    v = fm[:, 2 * RW:3 * RW]
    o0 = 3 * RW
    nw = w['w_up'].shape[0]
    na = w['a_up'].shape[0]
    wd = fm[:, o0:o0 + nw]
    ad = fm[:, o0 + nw:o0 + nw + na]
    gd = fm[:, o0 + nw + na:]
    w_log = -jax.nn.softplus(-(w['w0'] + _dot(jnp.tanh(wd).astype(BF16), w['w_up']))) - 0.5
    logw = -jnp.exp(w_log)
    a = jax.nn.sigmoid(w['a0'] + _dot(ad.astype(BF16), w['a_up']))
    g = _dot(jax.nn.sigmoid(gd).astype(BF16), w['g_up'])
    kk = k * w['k_k']
    kk = kk / jnp.maximum(jnp.sqrt(_headsum(kk * kk, w['e'])), 1e-12)
    k2 = k * (1.0 + (a - 1.0) * w['k_a'])
    return r, logw, k2, v, kk, kk * a, g


def _rwkv_post(o, r, k2, v, g, w):
    inv_n = 1.0 / HEAD
    mu = _headsum(o, w['e']) * inv_n
    d = o - mu
    var = _headsum(d * d, w['e']) * inv_n
    on = d * lax.rsqrt(var + LNX_EPS) * w['lnx_w'] + w['lnx_b']
    bonus = _headsum(r * k2 * w['r_k'], w['e']) * v
    return (on + bonus) * g


def _merge(h, x1, y_a, y_b, w):
    gates = jax.nn.sigmoid(_dot(h, w['w_g']))
    dm = x1.shape[1]
    merged = (gates[:, :dm] * _dot(y_a.astype(BF16), w['w_a_up'])
              + gates[:, dm:] * _dot(y_b.astype(BF16), w['w_b_up']))
    return x1 + _dot(merged.astype(BF16), w['w_out'])


_W_NAMES = ('w_f', 'w_g', 'mu', 'w0', 'w_up', 'a0', 'a_up', 'g_up', 'k_k', 'k_a', 'r_k',
            'lnx_w', 'lnx_b', 'e', 'w_a_up', 'w_b_up', 'w_out')


def _mixer_prompt_body(*refs, tc):
    h_ref, x1_ref, yb_ref = refs[0:3]
    nwt = len(_W_NAMES)
    wrefs = dict(zip(_W_NAMES, refs[3:3 + nwt]))
    tri_ref, mask_ref = refs[3 + nwt:5 + nwt]
    x2_ref, shift_ref, wkv_ref = refs[5 + nwt:8 + nwt]
    (carry_scr, st_scr, rt_scr, kkt_scr, bt_scr, kt_scr, btg_scr, ktg_scr, v_scr, gt_scr,
     o_scr) = refs[8 + nwt:]
    c = pl.program_id(1)
    w = {n: wrefs[n][...] for n in _W_NAMES if n not in ('w_f', 'w_g', 'w_a_up', 'w_b_up', 'w_out')}
    for n in ('w_g', 'w_a_up', 'w_b_up', 'w_out'):
        w[n] = wrefs[n][...]

    @pl.when(c == 0)
    def _():
        carry_scr[...] = jnp.zeros_like(carry_scr)
        st_scr[...] = jnp.zeros_like(st_scr)

    h = h_ref[...]
    f = _dot(h, wrefs['w_f'][...])
    rows = lax.broadcasted_iota(jnp.int32, f.shape, 0)
    shifted = jnp.where(rows == 0, carry_scr[...], pltpu.roll(f, 1, 0))
    carry_scr[...] = f[tc - 1:tc, :]
    fm = f + w['mu'] * (shifted - f)
    r, logw, k2, v, kk, b, g = _rwkv_pre(fm, w)

    ns = tc // CHUNK
    tri = tri_ref[...]
    for s in range(ns):
        sl = slice(s * CHUNK, (s + 1) * CHUNK)
        lw = logw[sl]
        hi, mid, lo = _split3(lw)
        cs = _dot(tri, hi) + _dot(tri, mid) + _dot(tri, lo)
        lpre = cs[:CHUNK]
        lsuf = cs[CHUNK:]
        e_pre = jnp.exp(lpre)
        e_neg = jnp.exp(-lpre)
        e_suf = jnp.exp(lsuf)
        rt_scr[sl, :] = (r[sl] * e_pre).astype(BF16)
        kkt_scr[sl, :] = (kk[sl] * jnp.exp(lpre - lw)).astype(BF16)
        bt_scr[sl, :] = (b[sl] * e_neg).astype(BF16)
        kt_scr[sl, :] = (k2[sl] * e_neg).astype(BF16)
        btg_scr[sl, :] = (b[sl] * e_suf).astype(BF16)
        ktg_scr[sl, :] = (k2[sl] * e_suf).astype(BF16)
        gt_scr[s] = jnp.broadcast_to(e_pre[CHUNK - 1:CHUNK, :], (8, RW))
    v_scr[...] = v.astype(BF16)

    mask = mask_ref[...] > 0.5
    eye = (lax.broadcasted_iota(jnp.int32, (CHUNK, CHUNK), 0)
           == lax.broadcasted_iota(jnp.int32, (CHUNK, CHUNK), 1)).astype(F32)
    n_double = CHUNK.bit_length() - 2

    def chunk_step(s, carry):
        r0 = pl.multiple_of(s * CHUNK, CHUNK)
        rs = pl.ds(r0, CHUNK)
        for hd in range(HEADS):
            ls = slice(hd * HEAD, (hd + 1) * HEAD)
            rt = rt_scr[rs, ls]
            kkt = kkt_scr[rs, ls]
            bt = bt_scr[rs, ls]
            kt = kt_scr[rs, ls]
            btg = btg_scr[rs, ls]
            ktg = ktg_scr[rs, ls]
            vv = v_scr[rs, ls]
            gt = gt_scr[s][0:1, ls]
            a_mat = _dot_nt(jnp.concatenate([kkt, rt], axis=0), jnp.concatenate([bt, kt], axis=0))
            a_mat = jnp.where(mask, a_mat, 0.0)
            m_b = a_mat[:CHUNK, :CHUNK]
            m_k = a_mat[:CHUNK, CHUNK:].astype(BF16)
            n_b = a_mat[CHUNK:, :CHUNK].astype(BF16)
            n_k = a_mat[CHUNK:, CHUNK:].astype(BF16)
            p = -m_b
            winv = eye + p
            for _ in range(n_double):
                pb = p.astype(BF16)
                p = _dot(pb, pb)
                winv = winv + _dot(winv.astype(BF16), p.astype(BF16))
            wb = winv.astype(BF16)
            gm = _dot(wb, kkt)
            u0 = -_dot(wb, _dot(m_k, vv).astype(BF16))
            gb = gm.astype(BF16)
            u0b = u0.astype(BF16)
            q = rt.astype(F32) - _dot(n_b, gb)
            o0 = _dot(n_b, u0b) + _dot(n_k, vv)
            phi = eye * gt - _dot_tn(btg, gb)
            psi = _dot_tn(btg, u0b) + _dot_tn(ktg, vv)
            st = st_scr[hd]
            stb = st.astype(BF16)
            o_scr[rs, ls] = _dot(q.astype(BF16), stb) + o0
            st_scr[hd] = _dot(phi.astype(BF16), stb) + psi
        return carry

    lax.fori_loop(0, ns, chunk_step, 0)

    y_a = _rwkv_post(o_scr[...], r, k2, v, g, w)
    x2_ref[...] = _merge(h, x1_ref[...], y_a, yb_ref[...], w)

    @pl.when(c == pl.num_programs(1) - 1)
    def _():
        shift_ref[...] = f[tc - 1:tc, :]
        for hd in range(HEADS):
            wkv_ref[hd] = st_scr[hd].T


def _mixer_prompt(h, x1, yb, wts, tri, mask, *, tc):
    bsz, seq, d = h.shape
    assert seq % tc == 0 and tc % CHUNK == 0
    nf = wts['w_f'].shape[1]
    wlist = [wts[n] for n in _W_NAMES]
    tok = lambda width: pl.BlockSpec((None, tc, width), lambda b, c: (b, c, 0))
    return pl.pallas_call(
        functools.partial(_mixer_prompt_body, tc=tc),
        grid=(bsz, seq // tc),
        in_specs=[tok(d), tok(d), tok(RW)] + [_const_spec(a.shape) for a in wlist]
                 + [_const_spec(tri.shape), _const_spec(mask.shape)],
        out_specs=[tok(d),
                   pl.BlockSpec((None, 1, nf), lambda b, c: (b, 0, 0)),
                   pl.BlockSpec((None, HEADS, HEAD, HEAD), lambda b, c: (b, 0, 0, 0))],
        out_shape=[jax.ShapeDtypeStruct((bsz, seq, d), F32),
                   jax.ShapeDtypeStruct((bsz, 1, nf), F32),
                   jax.ShapeDtypeStruct((bsz, HEADS, HEAD, HEAD), F32)],
        scratch_shapes=[pltpu.VMEM((1, nf), F32),
                        pltpu.VMEM((HEADS, HEAD, HEAD), F32)]
                       + [pltpu.VMEM((tc, RW), BF16)] * 7
                       + [pltpu.VMEM((tc // CHUNK, 8, RW), F32),
                          pltpu.VMEM((tc, RW), F32)],
        compiler_params=_cparams(2),
        name="mixer_prompt",
    )(h, x1, yb, *wlist, tri, mask)


def _mixer_sample_body(*refs, sb):
    h_ref, x1_ref, yb_ref, shift_in_ref, wkv_in_ref = refs[0:5]
    nwt = len(_W_NAMES)
    wrefs = dict(zip(_W_NAMES, refs[5:5 + nwt]))
    x2_ref, shift_ref, wkv_ref = refs[5 + nwt:8 + nwt]
    (o_scr,) = refs[8 + nwt:]
    w = {n: wrefs[n][...] for n in _W_NAMES if n != 'w_f'}

    h = h_ref[...]
    f = _dot(h, wrefs['w_f'][...])
    fm = f + w['mu'] * (shift_in_ref[...] - f)
    shift_ref[...] = f
    r, logw, k2, v, kk, b, g = _rwkv_pre(fm, w)
    dec = jnp.exp(logw)
    row8 = lax.broadcasted_iota(jnp.int32, (8, HEAD), 0)

    for i in range(sb):
        for hd in range(HEADS):
            ls = slice(hd * HEAD, (hd + 1) * HEAD)
            rows = slice(i, i + 1)
            st = wkv_in_ref[i, hd]
            kk8 = jnp.broadcast_to(kk[rows, ls], (8, HEAD))
            sa8 = -_dot_nt(kk8.astype(BF16), st.astype(BF16))
            v8 = jnp.broadcast_to(v[rows, ls], (8, HEAD))
            k8 = jnp.broadcast_to(k2[rows, ls], (8, HEAD))
            b8 = jnp.broadcast_to(b[rows, ls], (8, HEAD))
            xm = jnp.where(row8 == 0, v8, jnp.where(row8 == 1, sa8, 0.0))
            ym = jnp.where(row8 == 0, k8, jnp.where(row8 == 1, b8, 0.0))
            st_new = st * dec[rows, ls] + _dot_tn(xm, ym, precision=lax.Precision.HIGHEST)
            wkv_ref[i, hd] = st_new
            r8 = jnp.broadcast_to(r[rows, ls], (8, HEAD))
            o8 = _dot_nt(r8.astype(BF16), st_new.astype(BF16))
            o_scr[rows, ls] = o8[0:1, :]

    y_a = _rwkv_post(o_scr[...], r, k2, v, g, w)
    x2_ref[...] = _merge(h, x1_ref[...], y_a, yb_ref[...], w)


def _mixer_sample(h, x1, yb, shift_in, wkv_in, wts, *, sb):
    n, d = h.shape
    assert n % sb == 0
    nf = wts['w_f'].shape[1]
    wlist = [wts[nm] for nm in _W_NAMES]
    row = lambda width: pl.BlockSpec((sb, width), lambda i: (i, 0))
    st_spec = pl.BlockSpec((sb, HEADS, HEAD, HEAD), lambda i: (i, 0, 0, 0))
    return pl.pallas_call(
        functools.partial(_mixer_sample_body, sb=sb),
        grid=(n // sb,),
        in_specs=[row(d), row(d), row(RW), row(nf), st_spec] + [_const_spec(a.shape) for a in wlist],
        out_specs=[row(d), row(nf), st_spec],
        out_shape=[jax.ShapeDtypeStruct((n, d), F32),
                   jax.ShapeDtypeStruct((n, nf), F32),
                   jax.ShapeDtypeStruct(wkv_in.shape, F32)],
        scratch_shapes=[pltpu.VMEM((sb, RW), F32)],
        compiler_params=_cparams(1),
        name="mixer_sample",
    )(h, x1, yb, shift_in, wkv_in, *wlist)


def _chunk_constants():
    t = jnp.arange(CHUNK)
    incl = (t[:, None] >= t[None, :])
    strict = (t[:, None] > t[None, :])
    tri = jnp.concatenate([incl, strict.T], axis=0).astype(BF16)
    top = jnp.concatenate([strict, strict], axis=1)
    bot = jnp.concatenate([incl, incl], axis=1)
    mask = jnp.concatenate([top, bot], axis=0).astype(F32)
    return tri, mask


def kernel(x_prompt, x_sample, state_shift, state_wkv, state_s5_re, state_s5_im, g_ffn1, ffn1_gate, ffn1_up, ffn1_down, g_mix, w_in, mu_shift, w0, w_up, a0, a_up, g_up, k_k, k_a, r_k, lnx_w, lnx_b, A_re, A_im, log_dt, B_re, B_im, C_re, C_im, D_skip, w_glu, w_a_up, w_b_up, w_out, g_ffn2, ffn2_gate, ffn2_up, ffn2_down, g_final):
    depth = g_ffn1.shape[0]
    assert depth == 1
    bp, seq, d = x_prompt.shape
    bs = x_sample.shape[0]
    assert x_sample.shape[1] == 1
    ncols_f = mu_shift.shape[1]
    su = D_skip.shape[1]
    groups, pstate = A_re.shape[1:]
    nstate = groups * pstate
    bf = lambda a: a.astype(BF16)
    row = lambda a: a.reshape(1, -1).astype(F32)

    w_in0 = w_in[0]
    eye_h = jnp.eye(HEADS, dtype=F32)
    wts = {
        'w_f': bf(w_in0[:, :ncols_f]),
        'w_g': bf(w_in0[:, ncols_f + su:]),
        'mu': row(mu_shift[0]), 'w0': row(w0[0]), 'w_up': bf(w_up[0]), 'a0': row(a0[0]),
        'a_up': bf(a_up[0]), 'g_up': bf(g_up[0]), 'k_k': row(k_k[0]), 'k_a': row(k_a[0]),
        'r_k': row(r_k[0]), 'lnx_w': row(lnx_w[0]), 'lnx_b': row(lnx_b[0]),
        'e': bf(jnp.kron(eye_h, jnp.ones((HEAD, HEAD), F32))),
        'w_a_up': bf(w_a_up[0]), 'w_b_up': bf(w_b_up[0]), 'w_out': bf(w_out[0]),
    }
    w_u = bf(w_in0[:, ncols_f:ncols_f + su])
    abr, abi, bbr, bbi = _s5_params(A_re[0], A_im[0], log_dt[0],
                                    jnp.swapaxes(B_re[0], 1, 2), jnp.swapaxes(B_im[0], 1, 2))
    bb = bf(jnp.concatenate([_block_diag(bbr, S5_BLOCKS), _block_diag(bbi, S5_BLOCKS)], axis=2))
    cm = bf(jnp.concatenate([_block_diag(jnp.swapaxes(C_re[0], 1, 2), S5_BLOCKS),
                             _block_diag(-jnp.swapaxes(C_im[0], 1, 2), S5_BLOCKS)], axis=1))
    abr = abr.reshape(1, nstate)
    abi = abi.reshape(1, nstate)
    dsk = row(D_skip[0])
    wglu = bf(w_glu[0])
    ffn1 = (row(g_ffn1[0]), bf(ffn1_gate[0]), bf(ffn1_up[0]), bf(ffn1_down[0]), row(g_mix[0]))
    ffn2 = (row(g_ffn2[0]), bf(ffn2_gate[0]), bf(ffn2_up[0]), bf(ffn2_down[0]), row(g_final))
    fc = ffn1_gate.shape[2] // 2
    tri, mask = _chunk_constants()

    xp = x_prompt.reshape(bp * seq, d)
    x1p, hnp = _ffn(xp, *ffn1, final=False, tm=512, fc=fc)
    steps = 64
    h_tm = jnp.swapaxes(hnp.reshape(bp, seq, d), 0, 1).reshape(seq * bp, d)
    zeros_state = jnp.zeros((bp, nstate), F32)
    yb_tm, s5r_p, s5i_p = _s5(h_tm, w_u, bb, cm, abr, abi, dsk, wglu, zeros_state, zeros_state,
                              nb=bp, steps=steps)
    yb_p = jnp.swapaxes(yb_tm.reshape(seq, bp, su), 0, 1)
    x2p, shift_p, wkv_p = _mixer_prompt(hnp.reshape(bp, seq, d), x1p.reshape(bp, seq, d), yb_p,
                                        wts, tri, mask, tc=512)
    y_prompt = _ffn(x2p.reshape(bp * seq, d), *ffn2, final=True, tm=512, fc=fc).reshape(bp, seq, d)

    xs = x_sample.reshape(bs, d)
    x1s, hns = _ffn(xs, *ffn1, final=False, tm=bs, fc=fc)
    yb_s, s5r_s, s5i_s = _s5(hns, w_u, bb, cm, abr, abi, dsk, wglu,
                             state_s5_re[0].reshape(bs, nstate), state_s5_im[0].reshape(bs, nstate),
                             nb=bs, steps=1)
    x2s, shift_s, wkv_s = _mixer_sample(hns, x1s, yb_s, state_shift[0], state_wkv[0], wts, sb=16)
    y_sample = _ffn(x2s, *ffn2, final=True, tm=bs, fc=fc).reshape(bs, 1, d)

    st5 = lambda a, n: a.reshape(1, n, groups, pstate)
    return (y_prompt, y_sample,
            shift_p.reshape(1, bp, ncols_f), wkv_p[None], st5(s5r_p, bp), st5(s5i_p, bp),
            shift_s[None], wkv_s[None], st5(s5r_s, bs), st5(s5i_s, bs))
```

```python
import functools

import jax
import jax.numpy as jnp
from jax import lax
from jax.experimental import pallas as pl
from jax.experimental.pallas import tpu as pltpu

F32 = jnp.float32
BF16 = jnp.bfloat16

NORM_EPS = 1e-6
LNX_EPS = 64e-5
HEAD = 64
HEADS = 8
RW = HEAD * HEADS
S5_BLOCKS = 4
CHUNK = 64
VMEM_LIMIT = 56 * 1024 * 1024


def _dot(a, b):
    return jnp.dot(a, b, preferred_element_type=F32)


def _dot_nt(a, b):
    return lax.dot_general(a, b, (((1,), (1,)), ((), ())), preferred_element_type=F32)


def _dot_tn(a, b, precision=None):
    return lax.dot_general(a, b, (((0,), (0,)), ((), ())), preferred_element_type=F32,
                           precision=precision)


def _rms(x, g):
    ms = jnp.mean(x * x, axis=-1, keepdims=True)
    return x * lax.rsqrt(ms + NORM_EPS) * g


def _const_spec(shape):
    nd = len(shape)
    return pl.BlockSpec(shape, lambda *_: (0,) * nd, pipeline_mode=pl.Buffered(1))


def _cparams(ngrid):
    return pltpu.CompilerParams(dimension_semantics=("arbitrary",) * ngrid,
                                vmem_limit_bytes=VMEM_LIMIT)


def _ffn_body(x_ref, g_ref, wg_ref, wu_ref, wd_ref, gn_ref, *rest, final):
    if final:
        y_ref, hn_scr, acc_scr = rest
    else:
        x1_ref, hn_ref, hn_scr, acc_scr = rest
    j = pl.program_id(1)

    @pl.when(j == 0)
    def _():
        hn_scr[...] = _rms(x_ref[...], g_ref[...]).astype(BF16)
        acc_scr[...] = jnp.zeros_like(acc_scr)

    hn = hn_scr[...]
    gate = _dot(hn, wg_ref[...])
    up = _dot(hn, wu_ref[...])
    act = (jax.nn.silu(gate) * up).astype(BF16)
    acc_scr[...] += _dot(act, wd_ref[...])

    @pl.when(j == pl.num_programs(1) - 1)
    def _():
        x1 = x_ref[...] + 0.5 * acc_scr[...]
        if final:
            y_ref[...] = _rms(x1, gn_ref[...])
        else:
            x1_ref[...] = x1
            hn_ref[...] = _rms(x1, gn_ref[...]).astype(BF16)


def _ffn(x, g, wg, wu, wd, gn, *, final, tm, fc):
    m, d = x.shape
    f = wg.shape[1]
    assert m % tm == 0 and f % fc == 0
    if final:
        out_shape = jax.ShapeDtypeStruct((m, d), F32)
        out_specs = pl.BlockSpec((tm, d), lambda i, j: (i, 0))
    else:
        out_shape = [jax.ShapeDtypeStruct((m, d), F32), jax.ShapeDtypeStruct((m, d), BF16)]
        out_specs = [pl.BlockSpec((tm, d), lambda i, j: (i, 0)),
                     pl.BlockSpec((tm, d), lambda i, j: (i, 0))]
    return pl.pallas_call(
        functools.partial(_ffn_body, final=final),
        grid=(m // tm, f // fc),
        in_specs=[
            pl.BlockSpec((tm, d), lambda i, j: (i, 0)),
            pl.BlockSpec((1, d), lambda i, j: (0, 0)),
            pl.BlockSpec((d, fc), lambda i, j: (0, j)),
            pl.BlockSpec((d, fc), lambda i, j: (0, j)),
            pl.BlockSpec((fc, d), lambda i, j: (j, 0)),
            pl.BlockSpec((1, d), lambda i, j: (0, 0)),
        ],
        out_specs=out_specs,
        out_shape=out_shape,
        scratch_shapes=[pltpu.VMEM((tm, d), BF16), pltpu.VMEM((tm, d), F32)],
        compiler_params=_cparams(2),
        name="ffn_final" if final else "ffn_in",
    )(x, g, wg, wu, wd, gn)


def _s5_param_body(are_ref, aim_ref, ldt_ref, bre_ref, bim_ref, abr_ref, abi_ref, bbr_ref, bbi_ref):
    lam_r = are_ref[...]
    lam_i = aim_ref[...]
    dt = jnp.exp(ldt_ref[...])
    mag = jnp.exp(lam_r * dt)
    ab_r = mag * jnp.cos(lam_i * dt)
    ab_i = mag * jnp.sin(lam_i * dt)
    den = lam_r * lam_r + lam_i * lam_i
    q_r = ((ab_r - 1.0) * lam_r + ab_i * lam_i) / den
    q_i = (ab_i * lam_r - (ab_r - 1.0) * lam_i) / den
    abr_ref[...] = ab_r
    abi_ref[...] = ab_i
    bre = bre_ref[...]
    bim = bim_ref[...]
    bbr_ref[...] = q_r[:, None, :] * bre - q_i[:, None, :] * bim
    bbi_ref[...] = q_r[:, None, :] * bim + q_i[:, None, :] * bre


def _s5_params(a_re, a_im, log_dt, b_re_t, b_im_t):
    g, p = a_re.shape
    hg = b_re_t.shape[1]
    return pl.pallas_call(
        _s5_param_body,
        out_shape=[jax.ShapeDtypeStruct((g, p), F32), jax.ShapeDtypeStruct((g, p), F32),
                   jax.ShapeDtypeStruct((g, hg, p), F32), jax.ShapeDtypeStruct((g, hg, p), F32)],
        name="s5_params",
    )(a_re, a_im, log_dt.reshape(g, 1), b_re_t, b_im_t)


def _block_diag(w, nblk):
    g, a, b = w.shape
    gb = g // nblk
    w4 = w.reshape(nblk, gb, a, b)
    eye = jnp.eye(gb, dtype=w.dtype)
    return jnp.einsum('jgab,gk->jgakb', w4, eye).reshape(nblk, gb * a, gb * b)


def _s5_body(h_ref, wu_ref, bb_ref, cm_ref, abr_ref, abi_ref, dsk_ref, wglu_ref, x0r_ref, x0i_ref,
             yb_ref, xr_out, xi_out, bu_scr, y_scr, xr_scr, xi_scr, *, nb, steps):
    c = pl.program_id(0)

    @pl.when(c == 0)
    def _():
        xr_scr[...] = x0r_ref[...]
        xi_scr[...] = x0i_ref[...]

    u = _dot(h_ref[...], wu_ref[...])
    ub = u.astype(BF16)
    nin = ub.shape[1] // S5_BLOCKS
    ns = abr_ref.shape[1] // S5_BLOCKS
    for j in range(S5_BLOCKS):
        bu_scr[...] = _dot(ub[:, j * nin:(j + 1) * nin], bb_ref[j])
        ar = jnp.broadcast_to(abr_ref[:, j * ns:(j + 1) * ns], (nb, ns))
        ai = jnp.broadcast_to(abi_ref[:, j * ns:(j + 1) * ns], (nb, ns))

        def step(t, carry):
            xr, xi = carry
            r0 = pl.multiple_of(t * nb, nb)
            bur = bu_scr[pl.ds(r0, nb), 0:ns]
            bui = bu_scr[pl.ds(r0, nb), ns:2 * ns]
            nxr = ar * xr - ai * xi + bur
            nxi = ar * xi + ai * xr + bui
            bu_scr[pl.ds(r0, nb), 0:ns] = nxr
            bu_scr[pl.ds(r0, nb), ns:2 * ns] = nxi
            return nxr, nxi

        xr, xi = lax.fori_loop(0, steps, step,
                               (xr_scr[:, j * ns:(j + 1) * ns], xi_scr[:, j * ns:(j + 1) * ns]),
                               unroll=min(steps, 8))
        xr_scr[:, j * ns:(j + 1) * ns] = xr
        xi_scr[:, j * ns:(j + 1) * ns] = xi
        y_scr[:, j * nin:(j + 1) * nin] = _dot(bu_scr[...].astype(BF16), cm_ref[j])
    y = y_scr[...] + dsk_ref[...] * u
    z = jax.nn.gelu(y)
    yb_ref[...] = z * jax.nn.sigmoid(_dot(z.astype(BF16), wglu_ref[...]))
    xr_out[...] = xr_scr[...]
    xi_out[...] = xi_scr[...]


def _s5(h_tm, w_u, bb, cm, abr, abi, dsk, wglu, x0r, x0i, *, nb, steps):
    rows_total, d = h_tm.shape
    rows = nb * steps
    assert rows_total % rows == 0
    su = w_u.shape[1]
    nst = abr.shape[1]
    return pl.pallas_call(
        functools.partial(_s5_body, nb=nb, steps=steps),
        grid=(rows_total // rows,),
        in_specs=[
            pl.BlockSpec((rows, d), lambda c: (c, 0)),
            _const_spec(w_u.shape), _const_spec(bb.shape), _const_spec(cm.shape),
            _const_spec(abr.shape), _const_spec(abi.shape), _const_spec(dsk.shape),
            _const_spec(wglu.shape), _const_spec(x0r.shape), _const_spec(x0i.shape),
        ],
        out_specs=[pl.BlockSpec((rows, su), lambda c: (c, 0)),
                   pl.BlockSpec((nb, nst), lambda c: (0, 0)),
                   pl.BlockSpec((nb, nst), lambda c: (0, 0))],
        out_shape=[jax.ShapeDtypeStruct((rows_total, su), F32),
                   jax.ShapeDtypeStruct((nb, nst), F32),
                   jax.ShapeDtypeStruct((nb, nst), F32)],
        scratch_shapes=[pltpu.VMEM((rows, 2 * nst // S5_BLOCKS), F32),
                        pltpu.VMEM((rows, su), F32),
                        pltpu.VMEM((nb, nst), F32), pltpu.VMEM((nb, nst), F32)],
        compiler_params=_cparams(1),
        name="s5_branch",
    )(h_tm, w_u, bb, cm, abr, abi, dsk, wglu, x0r, x0i)


def _headsum(x, e):
    hi = x.astype(BF16)
    lo = (x - hi.astype(F32)).astype(BF16)
    return _dot(hi, e) + _dot(lo, e)


def _split3(x):
    hi = x.astype(BF16)
    r1 = x - hi.astype(F32)
    mid = r1.astype(BF16)
    lo = (r1 - mid.astype(F32)).astype(BF16)
    return hi, mid, lo


def _rwkv_pre(fm, w):
    r = fm[:, 0:RW]
    k = fm[:, RW:2 * RW]
    v = fm[:, 2 * RW:3 * RW]
    o0 = 3 * RW
    nw = w['w_up'].shape[0]
    na = w['a_up'].shape[0]
    wd = fm[:, o0:o0 + nw]
    ad = fm[:, o0 + nw:o0 + nw + na]
    gd = fm[:, o0 + nw + na:]
    w_log = -jax.nn.softplus(-(w['w0'] + _dot(jnp.tanh(wd).astype(BF16), w['w_up']))) - 0.5
    logw = -jnp.exp(w_log)
    a = jax.nn.sigmoid(w['a0'] + _dot(ad.astype(BF16), w['a_up']))
    g = _dot(jax.nn.sigmoid(gd).astype(BF16), w['g_up'])
    kk = k * w['k_k']
    kk = kk / jnp.maximum(jnp.sqrt(_headsum(kk * kk, w['e'])), 1e-12)
    k2 = k * (1.0 + (a - 1.0) * w['k_a'])
    return r, logw, k2, v, kk, kk * a, g


def _rwkv_post(o, r, k2, v, g, w):
    inv_n = 1.0 / HEAD
    mu = _headsum(o, w['e']) * inv_n
    d = o - mu
    var = _headsum(d * d, w['e']) * inv_n
    on = d * lax.rsqrt(var + LNX_EPS) * w['lnx_w'] + w['lnx_b']
    bonus = _headsum(r * k2 * w['r_k'], w['e']) * v
    return (on + bonus) * g


def _merge(h, x1, y_a, y_b, w):
    gates = jax.nn.sigmoid(_dot(h, w['w_g']))
    dm = x1.shape[1]
    merged = (gates[:, :dm] * _dot(y_a.astype(BF16), w['w_a_up'])
              + gates[:, dm:] * _dot(y_b.astype(BF16), w['w_b_up']))
    return x1 + _dot(merged.astype(BF16), w['w_out'])


_W_NAMES = ('w_f', 'w_g', 'mu', 'w0', 'w_up', 'a0', 'a_up', 'g_up', 'k_k', 'k_a', 'r_k',
            'lnx_w', 'lnx_b', 'e', 'w_a_up', 'w_b_up', 'w_out')


def _mixer_prompt_body(*refs, tc):
    h_ref, x1_ref, yb_ref = refs[0:3]
    nwt = len(_W_NAMES)
    wrefs = dict(zip(_W_NAMES, refs[3:3 + nwt]))
    tri_ref, mask_ref = refs[3 + nwt:5 + nwt]
    x2_ref, shift_ref, wkv_ref = refs[5 + nwt:8 + nwt]
    carry_scr, st_scr, lhs_scr, rhs_scr, sfx_scr, v_scr, gt_scr, o_scr = refs[8 + nwt:]
    c = pl.program_id(1)
    w = {n: wrefs[n][...] for n in _W_NAMES if n not in ('w_f', 'w_g', 'w_a_up', 'w_b_up', 'w_out')}
    for n in ('w_g', 'w_a_up', 'w_b_up', 'w_out'):
        w[n] = wrefs[n][...]

    @pl.when(c == 0)
    def _():
        carry_scr[...] = jnp.zeros_like(carry_scr)
        st_scr[...] = jnp.zeros_like(st_scr)

    h = h_ref[...]
    f = _dot(h, wrefs['w_f'][...])
    rows = lax.broadcasted_iota(jnp.int32, f.shape, 0)
    shifted = jnp.where(rows == 0, carry_scr[...], pltpu.roll(f, 1, 0))
    carry_scr[...] = f[tc - 1:tc, :]
    fm = f + w['mu'] * (shifted - f)
    r, logw, k2, v, kk, b, g = _rwkv_pre(fm, w)

    ns = tc // CHUNK
    tri = tri_ref[...]
    for s in range(ns):
        sl = slice(s * CHUNK, (s + 1) * CHUNK)
        lw = logw[sl]
        hi, mid, lo = _split3(lw)
        cs = _dot(tri, hi) + _dot(tri, mid) + _dot(tri, lo)
        lpre = cs[:CHUNK]
        lsuf = cs[CHUNK:]
        e_pre = jnp.exp(lpre)
        e_neg = jnp.exp(-lpre)
        e_suf = jnp.exp(lsuf)
        lhs = jnp.concatenate([kk[sl] * jnp.exp(lpre - lw), r[sl] * e_pre], axis=0).astype(BF16)
        rhs = jnp.concatenate([b[sl] * e_neg, k2[sl] * e_neg], axis=0).astype(BF16)
        sfx = jnp.concatenate([b[sl] * e_suf, k2[sl] * e_suf], axis=0).astype(BF16)
        vb = v[sl].astype(BF16)
        for hd in range(HEADS):
            ls = slice(hd * HEAD, (hd + 1) * HEAD)
            lhs_scr[hd, s] = lhs[:, ls]
            rhs_scr[hd, s] = rhs[:, ls]
            sfx_scr[hd, s] = sfx[:, ls]
            v_scr[hd, s] = vb[:, ls]
        gt_scr[s] = jnp.broadcast_to(e_pre[CHUNK - 1:CHUNK, :], (8, RW))

    mask = mask_ref[...] > 0.5
    eye = (lax.broadcasted_iota(jnp.int32, (CHUNK, CHUNK), 0)
           == lax.broadcasted_iota(jnp.int32, (CHUNK, CHUNK), 1)).astype(F32)
    n_double = CHUNK.bit_length() - 2
    heads = range(HEADS)

    def chunk_step(s, carry):
        rs = pl.ds(pl.multiple_of(s * CHUNK, CHUNK), CHUNK)
        gt8 = gt_scr[s]
        lhs = [lhs_scr[hd, s] for hd in heads]
        rhs = [rhs_scr[hd, s] for hd in heads]
        a_mat = [jnp.where(mask, _dot_nt(lhs[hd], rhs[hd]), 0.0) for hd in heads]
        p = [-a_mat[hd][:CHUNK, :CHUNK] for hd in heads]
        winv = [eye + p[hd] for hd in heads]
        for _ in range(n_double):
            p = [_dot(p[hd].astype(BF16), p[hd].astype(BF16)) for hd in heads]
            winv = [winv[hd] + _dot(winv[hd].astype(BF16), p[hd].astype(BF16)) for hd in heads]
        vv = [v_scr[hd, s] for hd in heads]
        mkv = [_dot(a_mat[hd][:CHUNK, CHUNK:].astype(BF16), vv[hd]).astype(BF16) for hd in heads]
        wb = [winv[hd].astype(BF16) for hd in heads]
        gb = [_dot(wb[hd], lhs[hd][:CHUNK]).astype(BF16) for hd in heads]
        u0b = [(-_dot(wb[hd], mkv[hd])).astype(BF16) for hd in heads]
        n_b = [a_mat[hd][CHUNK:, :CHUNK].astype(BF16) for hd in heads]
        q = [(lhs[hd][CHUNK:].astype(F32) - _dot(n_b[hd], gb[hd])).astype(BF16) for hd in heads]
        o0 = [_dot(n_b[hd], u0b[hd]) + _dot(a_mat[hd][CHUNK:, CHUNK:].astype(BF16), vv[hd])
              for hd in heads]
        sfx = [sfx_scr[hd, s] for hd in heads]
        phi = [(eye * gt8[0:1, hd * HEAD:(hd + 1) * HEAD] - _dot_tn(sfx[hd][:CHUNK], gb[hd])).astype(BF16)
               for hd in heads]
        psi = [_dot_tn(sfx[hd][:CHUNK], u0b[hd]) + _dot_tn(sfx[hd][CHUNK:], vv[hd])
               for hd in heads]
        stb = [st_scr[hd].astype(BF16) for hd in heads]
        for hd in heads:
            o_scr[rs, hd * HEAD:(hd + 1) * HEAD] = _dot(q[hd], stb[hd]) + o0[hd]
            st_scr[hd] = _dot(phi[hd], stb[hd]) + psi[hd]
        return carry

    lax.fori_loop(0, ns, chunk_step, 0)

    y_a = _rwkv_post(o_scr[...], r, k2, v, g, w)
    x2_ref[...] = _merge(h, x1_ref[...], y_a, yb_ref[...], w)

    @pl.when(c == pl.num_programs(1) - 1)
    def _():
        shift_ref[...] = f[tc - 1:tc, :]
        for hd in range(HEADS):
            wkv_ref[hd] = st_scr[hd].T


def _mixer_prompt(h, x1, yb, wts, tri, mask, *, tc):
    bsz, seq, d = h.shape
    assert seq % tc == 0 and tc % CHUNK == 0
    nf = wts['w_f'].shape[1]
    wlist = [wts[n] for n in _W_NAMES]
    tok = lambda width: pl.BlockSpec((None, tc, width), lambda b, c: (b, c, 0))
    return pl.pallas_call(
        functools.partial(_mixer_prompt_body, tc=tc),
        grid=(bsz, seq // tc),
        in_specs=[tok(d), tok(d), tok(RW)] + [_const_spec(a.shape) for a in wlist]
                 + [_const_spec(tri.shape), _const_spec(mask.shape)],
        out_specs=[tok(d),
                   pl.BlockSpec((None, 1, nf), lambda b, c: (b, 0, 0)),
                   pl.BlockSpec((None, HEADS, HEAD, HEAD), lambda b, c: (b, 0, 0, 0))],
        out_shape=[jax.ShapeDtypeStruct((bsz, seq, d), F32),
                   jax.ShapeDtypeStruct((bsz, 1, nf), F32),
                   jax.ShapeDtypeStruct((bsz, HEADS, HEAD, HEAD), F32)],
        scratch_shapes=[pltpu.VMEM((1, nf), F32),
                        pltpu.VMEM((HEADS, HEAD, HEAD), F32)]
                       + [pltpu.VMEM((HEADS, tc // CHUNK, 2 * CHUNK, HEAD), BF16)] * 3
                       + [pltpu.VMEM((HEADS, tc // CHUNK, CHUNK, HEAD), BF16),
                          pltpu.VMEM((tc // CHUNK, 8, RW), F32),
                          pltpu.VMEM((tc, RW), F32)],
        compiler_params=_cparams(2),
        name="mixer_prompt",
    )(h, x1, yb, *wlist, tri, mask)


def _mixer_sample_body(*refs, sb):
    h_ref, x1_ref, yb_ref, shift_in_ref, wkv_in_ref = refs[0:5]
    nwt = len(_W_NAMES)
    wrefs = dict(zip(_W_NAMES, refs[5:5 + nwt]))
    x2_ref, shift_ref, wkv_ref = refs[5 + nwt:8 + nwt]
    (o_scr,) = refs[8 + nwt:]
    w = {n: wrefs[n][...] for n in _W_NAMES if n != 'w_f'}

    h = h_ref[...]
    f = _dot(h, wrefs['w_f'][...])
    fm = f + w['mu'] * (shift_in_ref[...] - f)
    shift_ref[...] = f
    r, logw, k2, v, kk, b, g = _rwkv_pre(fm, w)
    dec = jnp.exp(logw)
    row8 = lax.broadcasted_iota(jnp.int32, (8, HEAD), 0)

    idx = [(i, hd) for i in range(sb) for hd in range(HEADS)]
    cut = lambda a, i, hd: a[i:i + 1, hd * HEAD:(hd + 1) * HEAD]
    bc8 = lambda a, i, hd: jnp.broadcast_to(cut(a, i, hd), (8, HEAD))
    sa8 = {}
    for i, hd in idx:
        st = wkv_in_ref[i, hd]
        sa8[i, hd] = -_dot_nt(bc8(kk, i, hd).astype(BF16), st.astype(BF16))
    for i, hd in idx:
        xm = jnp.where(row8 == 0, bc8(v, i, hd), jnp.where(row8 == 1, sa8[i, hd], 0.0))
        ym = jnp.where(row8 == 0, bc8(k2, i, hd), jnp.where(row8 == 1, bc8(b, i, hd), 0.0))
        wkv_ref[i, hd] = (wkv_in_ref[i, hd] * cut(dec, i, hd)
                          + _dot_tn(xm, ym, precision=lax.Precision.HIGHEST))
    for i, hd in idx:
        o8 = _dot_nt(bc8(r, i, hd).astype(BF16), wkv_ref[i, hd].astype(BF16))
        o_scr[i:i + 1, hd * HEAD:(hd + 1) * HEAD] = o8[0:1, :]

    y_a = _rwkv_post(o_scr[...], r, k2, v, g, w)
    x2_ref[...] = _merge(h, x1_ref[...], y_a, yb_ref[...], w)


def _mixer_sample(h, x1, yb, shift_in, wkv_in, wts, *, sb):
    n, d = h.shape
    assert n % sb == 0
    nf = wts['w_f'].shape[1]
    wlist = [wts[nm] for nm in _W_NAMES]
    row = lambda width: pl.BlockSpec((sb, width), lambda i: (i, 0))
    st_spec = pl.BlockSpec((sb, HEADS, HEAD, HEAD), lambda i: (i, 0, 0, 0))
    return pl.pallas_call(
        functools.partial(_mixer_sample_body, sb=sb),
        grid=(n // sb,),
        in_specs=[row(d), row(d), row(RW), row(nf), st_spec] + [_const_spec(a.shape) for a in wlist],
        out_specs=[row(d), row(nf), st_spec],
        out_shape=[jax.ShapeDtypeStruct((n, d), F32),
                   jax.ShapeDtypeStruct((n, nf), F32),
                   jax.ShapeDtypeStruct(wkv_in.shape, F32)],
        scratch_shapes=[pltpu.VMEM((sb, RW), F32)],
        compiler_params=_cparams(1),
        name="mixer_sample",
    )(h, x1, yb, shift_in, wkv_in, *wlist)


def _chunk_constants():
    t = jnp.arange(CHUNK)
    incl = (t[:, None] >= t[None, :])
    strict = (t[:, None] > t[None, :])
    tri = jnp.concatenate([incl, strict.T], axis=0).astype(BF16)
    top = jnp.concatenate([strict, strict], axis=1)
    bot = jnp.concatenate([incl, incl], axis=1)
    mask = jnp.concatenate([top, bot], axis=0).astype(F32)
    return tri, mask


def kernel(x_prompt, x_sample, state_shift, state_wkv, state_s5_re, state_s5_im, g_ffn1, ffn1_gate, ffn1_up, ffn1_down, g_mix, w_in, mu_shift, w0, w_up, a0, a_up, g_up, k_k, k_a, r_k, lnx_w, lnx_b, A_re, A_im, log_dt, B_re, B_im, C_re, C_im, D_skip, w_glu, w_a_up, w_b_up, w_out, g_ffn2, ffn2_gate, ffn2_up, ffn2_down, g_final):
    depth = g_ffn1.shape[0]
    assert depth == 1
    bp, seq, d = x_prompt.shape
    bs = x_sample.shape[0]
    assert x_sample.shape[1] == 1
    ncols_f = mu_shift.shape[1]
    su = D_skip.shape[1]
    groups, pstate = A_re.shape[1:]
    nstate = groups * pstate
    bf = lambda a: a.astype(BF16)
    row = lambda a: a.reshape(1, -1).astype(F32)

    w_in0 = w_in[0]
    eye_h = jnp.eye(HEADS, dtype=F32)
    wts = {
        'w_f': bf(w_in0[:, :ncols_f]),
        'w_g': bf(w_in0[:, ncols_f + su:]),
        'mu': row(mu_shift[0]), 'w0': row(w0[0]), 'w_up': bf(w_up[0]), 'a0': row(a0[0]),
        'a_up': bf(a_up[0]), 'g_up': bf(g_up[0]), 'k_k': row(k_k[0]), 'k_a': row(k_a[0]),
        'r_k': row(r_k[0]), 'lnx_w': row(lnx_w[0]), 'lnx_b': row(lnx_b[0]),
        'e': bf(jnp.kron(eye_h, jnp.ones((HEAD, HEAD), F32))),
        'w_a_up': bf(w_a_up[0]), 'w_b_up': bf(w_b_up[0]), 'w_out': bf(w_out[0]),
    }
    w_u = bf(w_in0[:, ncols_f:ncols_f + su])
    abr, abi, bbr, bbi = _s5_params(A_re[0], A_im[0], log_dt[0],
                                    jnp.swapaxes(B_re[0], 1, 2), jnp.swapaxes(B_im[0], 1, 2))
    bb = bf(jnp.concatenate([_block_diag(bbr, S5_BLOCKS), _block_diag(bbi, S5_BLOCKS)], axis=2))
    cm = bf(jnp.concatenate([_block_diag(jnp.swapaxes(C_re[0], 1, 2), S5_BLOCKS),
                             _block_diag(-jnp.swapaxes(C_im[0], 1, 2), S5_BLOCKS)], axis=1))
    abr = abr.reshape(1, nstate)
    abi = abi.reshape(1, nstate)
    dsk = row(D_skip[0])
    wglu = bf(w_glu[0])
    ffn1 = (row(g_ffn1[0]), bf(ffn1_gate[0]), bf(ffn1_up[0]), bf(ffn1_down[0]), row(g_mix[0]))
    ffn2 = (row(g_ffn2[0]), bf(ffn2_gate[0]), bf(ffn2_up[0]), bf(ffn2_down[0]), row(g_final))
    fc = ffn1_gate.shape[2] // 2
    tri, mask = _chunk_constants()

    xp = x_prompt.reshape(bp * seq, d)
    x1p, hnp = _ffn(xp, *ffn1, final=False, tm=512, fc=fc)
    steps = 64
    h_tm = jnp.swapaxes(hnp.reshape(bp, seq, d), 0, 1).reshape(seq * bp, d)
    zeros_state = jnp.zeros((bp, nstate), F32)
    yb_tm, s5r_p, s5i_p = _s5(h_tm, w_u, bb, cm, abr, abi, dsk, wglu, zeros_state, zeros_state,
                              nb=bp, steps=steps)
    yb_p = jnp.swapaxes(yb_tm.reshape(seq, bp, su), 0, 1)
    x2p, shift_p, wkv_p = _mixer_prompt(hnp.reshape(bp, seq, d), x1p.reshape(bp, seq, d), yb_p,
                                        wts, tri, mask, tc=512)
    y_prompt = _ffn(x2p.reshape(bp * seq, d), *ffn2, final=True, tm=512, fc=fc).reshape(bp, seq, d)

    xs = x_sample.reshape(bs, d)
    x1s, hns = _ffn(xs, *ffn1, final=False, tm=bs, fc=fc)
    yb_s, s5r_s, s5i_s = _s5(hns, w_u, bb, cm, abr, abi, dsk, wglu,
                             state_s5_re[0].reshape(bs, nstate), state_s5_im[0].reshape(bs, nstate),
                             nb=bs, steps=1)
    x2s, shift_s, wkv_s = _mixer_sample(hns, x1s, yb_s, state_shift[0], state_wkv[0], wts, sb=16)
    y_sample = _ffn(x2s, *ffn2, final=True, tm=bs, fc=fc).reshape(bs, 1, d)

    st5 = lambda a, n: a.reshape(1, n, groups, pstate)
    return (y_prompt, y_sample,
            shift_p.reshape(1, bp, ncols_f), wkv_p[None], st5(s5r_p, bp), st5(s5i_p, bp),
            shift_s[None], wkv_s[None], st5(s5r_s, bs), st5(s5i_s, bs))
```

```python
import functools

import jax
import jax.numpy as jnp
from jax import lax
from jax.experimental import pallas as pl
from jax.experimental.pallas import tpu as pltpu

F32 = jnp.float32
BF16 = jnp.bfloat16

NORM_EPS = 1e-6
LNX_EPS = 64e-5
HEAD = 64
HEADS = 8
RW = HEAD * HEADS
S5_BLOCKS = 4
CHUNK = 64
VMEM_LIMIT = 56 * 1024 * 1024
MXU_TILE_V7X = 256


def _dot(a, b):
    return jnp.dot(a, b, preferred_element_type=F32)


def _dot_nt(a, b):
    return lax.dot_general(a, b, (((1,), (1,)), ((), ())), preferred_element_type=F32)


def _dot_tn(a, b, precision=None):
    return lax.dot_general(a, b, (((0,), (0,)), ((), ())), preferred_element_type=F32,
                           precision=precision)


def _rms(x, g):
    ms = jnp.mean(x * x, axis=-1, keepdims=True)
    return x * lax.rsqrt(ms + NORM_EPS) * g


def _const_spec(shape):
    nd = len(shape)
    return pl.BlockSpec(shape, lambda *_: (0,) * nd, pipeline_mode=pl.Buffered(1))


def _cparams(ngrid):
    return pltpu.CompilerParams(dimension_semantics=("arbitrary",) * ngrid,
                                vmem_limit_bytes=VMEM_LIMIT)


def _ffn_body(x_ref, g_ref, wg_ref, wu_ref, wd_ref, gn_ref, *outs, final, fc):
    x = x_ref[...]
    hn = _rms(x, g_ref[...]).astype(BF16)
    acc = None
    f = wg_ref.shape[1]
    for c0 in range(0, f, fc):
        c1 = min(c0 + fc, f)
        gate = _dot(hn, wg_ref[:, c0:c1])
        up = _dot(hn, wu_ref[:, c0:c1])
        act = (jax.nn.silu(gate) * up).astype(BF16)
        part = _dot(act, wd_ref[c0:c1, :])
        acc = part if acc is None else acc + part
    x1 = x + 0.5 * acc
    if final:
        outs[0][...] = _rms(x1, gn_ref[...])
    else:
        outs[0][...] = x1
        outs[1][...] = _rms(x1, gn_ref[...]).astype(BF16)


def _ffn(x, g, wg, wu, wd, gn, *, final, tm, fc):
    m, d = x.shape
    f = wg.shape[1]
    assert m % tm == 0 and fc % MXU_TILE_V7X == 0
    tile = pl.BlockSpec((tm, d), lambda i: (i, 0))
    if final:
        out_shape = jax.ShapeDtypeStruct((m, d), F32)
        out_specs = tile
    else:
        out_shape = [jax.ShapeDtypeStruct((m, d), F32), jax.ShapeDtypeStruct((m, d), BF16)]
        out_specs = [tile, tile]
    return pl.pallas_call(
        functools.partial(_ffn_body, final=final, fc=fc),
        grid=(m // tm,),
        in_specs=[tile, _const_spec(g.shape), _const_spec(wg.shape), _const_spec(wu.shape),
                  _const_spec(wd.shape), _const_spec(gn.shape)],
        out_specs=out_specs,
        out_shape=out_shape,
        compiler_params=_cparams(1),
        name="ffn_final" if final else "ffn_in",
    )(x, g, wg, wu, wd, gn)


def _s5_param_body(are_ref, aim_ref, ldt_ref, bre_ref, bim_ref, abr_ref, abi_ref, bbr_ref, bbi_ref):
    lam_r = are_ref[...]
    lam_i = aim_ref[...]
    dt = jnp.exp(ldt_ref[...])
    mag = jnp.exp(lam_r * dt)
    ab_r = mag * jnp.cos(lam_i * dt)
    ab_i = mag * jnp.sin(lam_i * dt)
    den = lam_r * lam_r + lam_i * lam_i
    q_r = ((ab_r - 1.0) * lam_r + ab_i * lam_i) / den
    q_i = (ab_i * lam_r - (ab_r - 1.0) * lam_i) / den
    abr_ref[...] = ab_r
    abi_ref[...] = ab_i
    bre = bre_ref[...]
    bim = bim_ref[...]
    bbr_ref[...] = q_r[:, None, :] * bre - q_i[:, None, :] * bim
    bbi_ref[...] = q_r[:, None, :] * bim + q_i[:, None, :] * bre


def _s5_params(a_re, a_im, log_dt, b_re_t, b_im_t):
    g, p = a_re.shape
    hg = b_re_t.shape[1]
    return pl.pallas_call(
        _s5_param_body,
        out_shape=[jax.ShapeDtypeStruct((g, p), F32), jax.ShapeDtypeStruct((g, p), F32),
                   jax.ShapeDtypeStruct((g, hg, p), F32), jax.ShapeDtypeStruct((g, hg, p), F32)],
        name="s5_params",
    )(a_re, a_im, log_dt.reshape(g, 1), b_re_t, b_im_t)


def _block_diag(w, nblk):
    g, a, b = w.shape
    gb = g // nblk
    w4 = w.reshape(nblk, gb, a, b)
    eye = jnp.eye(gb, dtype=w.dtype)
    return jnp.einsum('jgab,gk->jgakb', w4, eye).reshape(nblk, gb * a, gb * b)


def _s5_body(h_ref, perm_ref, wu_ref, bb_ref, cm_ref, abr_ref, abi_ref, dsk_ref, wglu_ref, x0r_ref,
             x0i_ref, yb_ref, xr_out, xi_out, bu_scr, y_scr, xr_scr, xi_scr, *, nb, steps):
    c = pl.program_id(0)
    rows = nb * steps

    @pl.when(c == 0)
    def _():
        xr_scr[...] = x0r_ref[...]
        xi_scr[...] = x0i_ref[...]

    h = h_ref[...].reshape(rows, h_ref.shape[-1])
    if steps > 1:
        h = _dot(perm_ref[0], h).astype(BF16)
    u = _dot(h, wu_ref[...])
    ub = u.astype(BF16)
    nin = ub.shape[1] // S5_BLOCKS
    ns = abr_ref.shape[1] // S5_BLOCKS
    for j in range(S5_BLOCKS):
        bu_scr[...] = _dot(ub[:, j * nin:(j + 1) * nin], bb_ref[j])
        ar = jnp.broadcast_to(abr_ref[:, j * ns:(j + 1) * ns], (nb, ns))
        ai = jnp.broadcast_to(abi_ref[:, j * ns:(j + 1) * ns], (nb, ns))

        def step(t, carry):
            xr, xi = carry
            r0 = pl.multiple_of(t * nb, nb)
            bur = bu_scr[pl.ds(r0, nb), 0:ns]
            bui = bu_scr[pl.ds(r0, nb), ns:2 * ns]
            nxr = ar * xr - ai * xi + bur
            nxi = ar * xi + ai * xr + bui
            bu_scr[pl.ds(r0, nb), 0:ns] = nxr
            bu_scr[pl.ds(r0, nb), ns:2 * ns] = nxi
            return nxr, nxi

        xr, xi = lax.fori_loop(0, steps, step,
                               (xr_scr[:, j * ns:(j + 1) * ns], xi_scr[:, j * ns:(j + 1) * ns]),
                               unroll=min(steps, 8))
        xr_scr[:, j * ns:(j + 1) * ns] = xr
        xi_scr[:, j * ns:(j + 1) * ns] = xi
        y_scr[:, j * nin:(j + 1) * nin] = _dot(bu_scr[...].astype(BF16), cm_ref[j])
    y = y_scr[...] + dsk_ref[...] * u
    z = jax.nn.gelu(y)
    yb = (z * jax.nn.sigmoid(_dot(z.astype(BF16), wglu_ref[...]))).astype(BF16)
    if steps > 1:
        yb = _dot(perm_ref[1], yb).astype(BF16)
    yb_ref[...] = yb.reshape(yb_ref.shape)
    xr_out[...] = xr_scr[...]
    xi_out[...] = xi_scr[...]


def _s5(h, w_u, bb, cm, abr, abi, dsk, wglu, x0r, x0i, *, nb, steps):
    g, seq, d = h.shape
    rows = nb * steps
    assert seq % (rows // g) == 0 and (g == nb or steps == 1)
    blk = rows // g
    su = w_u.shape[1]
    nst = abr.shape[1]
    r = jnp.arange(rows)
    to_tm = (r[:, None] % nb) * steps + r[:, None] // nb == r[None, :]
    perm = jnp.stack([to_tm, to_tm.T]).astype(BF16)
    return pl.pallas_call(
        functools.partial(_s5_body, nb=nb, steps=steps),
        grid=(seq // blk,),
        in_specs=[
            pl.BlockSpec((g, blk, d), lambda c: (0, c, 0)),
            _const_spec(perm.shape),
            _const_spec(w_u.shape), _const_spec(bb.shape), _const_spec(cm.shape),
            _const_spec(abr.shape), _const_spec(abi.shape), _const_spec(dsk.shape),
            _const_spec(wglu.shape), _const_spec(x0r.shape), _const_spec(x0i.shape),
        ],
        out_specs=[pl.BlockSpec((g, blk, su), lambda c: (0, c, 0)),
                   pl.BlockSpec((nb, nst), lambda c: (0, 0)),
                   pl.BlockSpec((nb, nst), lambda c: (0, 0))],
        out_shape=[jax.ShapeDtypeStruct((g, seq, su), BF16),
                   jax.ShapeDtypeStruct((nb, nst), F32),
                   jax.ShapeDtypeStruct((nb, nst), F32)],
        scratch_shapes=[pltpu.VMEM((rows, 2 * nst // S5_BLOCKS), F32),
                        pltpu.VMEM((rows, su), F32),
                        pltpu.VMEM((nb, nst), F32), pltpu.VMEM((nb, nst), F32)],
        compiler_params=_cparams(1),
        name="s5_branch",
    )(h, perm, w_u, bb, cm, abr, abi, dsk, wglu, x0r, x0i)


def _headsum(x, e):
    return _dot(x.astype(BF16), e)


def _split3(x):
    hi = x.astype(BF16)
    r1 = x - hi.astype(F32)
    mid = r1.astype(BF16)
    lo = (r1 - mid.astype(F32)).astype(BF16)
    return hi, mid, lo


def _rwkv_pre(fm, w):
    r = fm[:, 0:RW]
    k = fm[:, RW:2 * RW]
    v = fm[:, 2 * RW:3 * RW]
    o0 = 3 * RW
    nw = w['w_up'].shape[0]
    na = w['a_up'].shape[0]
    wd = fm[:, o0:o0 + nw]
    ad = fm[:, o0 + nw:o0 + nw + na]
    gd = fm[:, o0 + nw + na:]
    w_log = -jax.nn.softplus(-(w['w0'] + _dot(jnp.tanh(wd).astype(BF16), w['w_up']))) - 0.5
    logw = -jnp.exp(w_log)
    a = jax.nn.sigmoid(w['a0'] + _dot(ad.astype(BF16), w['a_up']))
    g = _dot(jax.nn.sigmoid(gd).astype(BF16), w['g_up'])
    kk = k * w['k_k']
    kk = kk / jnp.maximum(jnp.sqrt(_headsum(kk * kk, w['e'])), 1e-12)
    k2 = k * (1.0 + (a - 1.0) * w['k_a'])
    return r, logw, k2, v, kk, kk * a, g


def _rwkv_post(o, r, k2, v, g, w):
    inv_n = 1.0 / HEAD
    mu = _headsum(o, w['e']) * inv_n
    d = o - mu
    var = _headsum(d * d, w['e']) * inv_n
    on = d * lax.rsqrt(var + LNX_EPS) * w['lnx_w'] + w['lnx_b']
    bonus = _headsum(r * k2 * w['r_k'], w['e']) * v
    return (on + bonus) * g


def _merge(h, x1, y_a, y_b, w):
    gates = jax.nn.sigmoid(_dot(h, w['w_g']))
    dm = x1.shape[1]
    merged = (gates[:, :dm] * _dot(y_a.astype(BF16), w['w_a_up'])
              + gates[:, dm:] * _dot(y_b.astype(BF16), w['w_b_up']))
    return x1 + _dot(merged.astype(BF16), w['w_out'])


_W_NAMES = ('w_f', 'w_g', 'mu', 'w0', 'w_up', 'a0', 'a_up', 'g_up', 'k_k', 'k_a', 'r_k',
            'lnx_w', 'lnx_b', 'e', 'w_a_up', 'w_b_up', 'w_out')


def _mixer_prompt_body(*refs, tc):
    h_ref, x1_ref, yb_ref = refs[0:3]
    nwt = len(_W_NAMES)
    wrefs = dict(zip(_W_NAMES, refs[3:3 + nwt]))
    tri_ref, mask_ref = refs[3 + nwt:5 + nwt]
    x2_ref, shift_ref, wkv_ref = refs[5 + nwt:8 + nwt]
    carry_scr, st_scr, lhs_scr, rhs_scr, sfx_scr, v_scr, gt_scr, o_scr = refs[8 + nwt:]
    c = pl.program_id(1)
    w = {n: wrefs[n][...] for n in _W_NAMES if n not in ('w_f', 'w_g', 'w_a_up', 'w_b_up', 'w_out')}
    for n in ('w_g', 'w_a_up', 'w_b_up', 'w_out'):
        w[n] = wrefs[n][...]

    @pl.when(c == 0)
    def _():
        carry_scr[...] = jnp.zeros_like(carry_scr)
        st_scr[...] = jnp.zeros_like(st_scr)

    h = h_ref[...]
    f = _dot(h, wrefs['w_f'][...])
    rows = lax.broadcasted_iota(jnp.int32, f.shape, 0)
    shifted = jnp.where(rows == 0, carry_scr[...], pltpu.roll(f, 1, 0))
    carry_scr[...] = f[tc - 1:tc, :]
    fm = f + w['mu'] * (shifted - f)
    r, logw, k2, v, kk, b, g = _rwkv_pre(fm, w)

    ns = tc // CHUNK
    tri = tri_ref[...]
    for s in range(ns):
        sl = slice(s * CHUNK, (s + 1) * CHUNK)
        lw = logw[sl]
        hi, mid, lo = _split3(lw)
        cs = _dot(tri, hi) + _dot(tri, mid) + _dot(tri, lo)
        lpre = cs[:CHUNK]
        lsuf = cs[CHUNK:]
        e_pre = jnp.exp(lpre)
        e_neg = jnp.exp(-lpre)
        e_suf = jnp.exp(lsuf)
        lhs = jnp.concatenate([kk[sl] * jnp.exp(lpre - lw), r[sl] * e_pre], axis=0).astype(BF16)
        rhs = jnp.concatenate([b[sl] * e_neg, k2[sl] * e_neg], axis=0).astype(BF16)
        sfx = jnp.concatenate([b[sl] * e_suf, k2[sl] * e_suf], axis=0).astype(BF16)
        vb = v[sl].astype(BF16)
        for hd in range(HEADS):
            ls = slice(hd * HEAD, (hd + 1) * HEAD)
            lhs_scr[hd, s] = lhs[:, ls]
            rhs_scr[hd, s] = rhs[:, ls]
            sfx_scr[hd, s] = sfx[:, ls]
            v_scr[hd, s] = vb[:, ls]
        gt_scr[s] = jnp.broadcast_to(e_pre[CHUNK - 1:CHUNK, :], (8, RW))

    mask = mask_ref[...] > 0.5
    eye = (lax.broadcasted_iota(jnp.int32, (CHUNK, CHUNK), 0)
           == lax.broadcasted_iota(jnp.int32, (CHUNK, CHUNK), 1)).astype(F32)
    n_double = CHUNK.bit_length() - 2
    heads = range(HEADS)

    def chunk_step(s, carry):
        rs = pl.ds(pl.multiple_of(s * CHUNK, CHUNK), CHUNK)
        gt8 = gt_scr[s]
        lhs = [lhs_scr[hd, s] for hd in heads]
        rhs = [rhs_scr[hd, s] for hd in heads]
        a_mat = [jnp.where(mask, _dot_nt(lhs[hd], rhs[hd]), 0.0) for hd in heads]
        p = [-a_mat[hd][:CHUNK, :CHUNK] for hd in heads]
        winv = [eye + p[hd] for hd in heads]
        for _ in range(n_double):
            p = [_dot(p[hd].astype(BF16), p[hd].astype(BF16)) for hd in heads]
            winv = [winv[hd] + _dot(winv[hd].astype(BF16), p[hd].astype(BF16)) for hd in heads]
        vv = [v_scr[hd, s] for hd in heads]
        mkv = [_dot(a_mat[hd][:CHUNK, CHUNK:].astype(BF16), vv[hd]).astype(BF16) for hd in heads]
        wb = [winv[hd].astype(BF16) for hd in heads]
        gb = [_dot(wb[hd], lhs[hd][:CHUNK]).astype(BF16) for hd in heads]
        u0b = [(-_dot(wb[hd], mkv[hd])).astype(BF16) for hd in heads]
        n_b = [a_mat[hd][CHUNK:, :CHUNK].astype(BF16) for hd in heads]
        q = [(lhs[hd][CHUNK:].astype(F32) - _dot(n_b[hd], gb[hd])).astype(BF16) for hd in heads]
        o0 = [_dot(n_b[hd], u0b[hd]) + _dot(a_mat[hd][CHUNK:, CHUNK:].astype(BF16), vv[hd])
              for hd in heads]
        sfx = [sfx_scr[hd, s] for hd in heads]
        phi = [(eye * gt8[0:1, hd * HEAD:(hd + 1) * HEAD] - _dot_tn(sfx[hd][:CHUNK], gb[hd])).astype(BF16)
               for hd in heads]
        psi = [_dot_tn(sfx[hd][:CHUNK], u0b[hd]) + _dot_tn(sfx[hd][CHUNK:], vv[hd])
               for hd in heads]
        stb = [st_scr[hd].astype(BF16) for hd in heads]
        for hd in heads:
            o_scr[rs, hd * HEAD:(hd + 1) * HEAD] = _dot(q[hd], stb[hd]) + o0[hd]
            st_scr[hd] = _dot(phi[hd], stb[hd]) + psi[hd]
        return carry

    lax.fori_loop(0, ns, chunk_step, 0)

    y_a = _rwkv_post(o_scr[...], r, k2, v, g, w)
    x2_ref[...] = _merge(h, x1_ref[...], y_a, yb_ref[...], w)

    @pl.when(c == pl.num_programs(1) - 1)
    def _():
        shift_ref[...] = f[tc - 1:tc, :]
        for hd in range(HEADS):
            wkv_ref[hd] = st_scr[hd].T


def _mixer_prompt(h, x1, yb, wts, tri, mask, *, tc):
    bsz, seq, d = h.shape
    assert seq % tc == 0 and tc % CHUNK == 0
    nf = wts['w_f'].shape[1]
    wlist = [wts[n] for n in _W_NAMES]
    tok = lambda width: pl.BlockSpec((None, tc, width), lambda b, c: (b, c, 0))
    return pl.pallas_call(
        functools.partial(_mixer_prompt_body, tc=tc),
        grid=(bsz, seq // tc),
        in_specs=[tok(d), tok(d), tok(RW)] + [_const_spec(a.shape) for a in wlist]
                 + [_const_spec(tri.shape), _const_spec(mask.shape)],
        out_specs=[tok(d),
                   pl.BlockSpec((None, 1, nf), lambda b, c: (b, 0, 0)),
                   pl.BlockSpec((None, HEADS, HEAD, HEAD), lambda b, c: (b, 0, 0, 0))],
        out_shape=[jax.ShapeDtypeStruct((bsz, seq, d), F32),
                   jax.ShapeDtypeStruct((bsz, 1, nf), F32),
                   jax.ShapeDtypeStruct((bsz, HEADS, HEAD, HEAD), F32)],
        scratch_shapes=[pltpu.VMEM((1, nf), F32),
                        pltpu.VMEM((HEADS, HEAD, HEAD), F32)]
                       + [pltpu.VMEM((HEADS, tc // CHUNK, 2 * CHUNK, HEAD), BF16)] * 3
                       + [pltpu.VMEM((HEADS, tc // CHUNK, CHUNK, HEAD), BF16),
                          pltpu.VMEM((tc // CHUNK, 8, RW), F32),
                          pltpu.VMEM((tc, RW), F32)],
        compiler_params=_cparams(2),
        name="mixer_prompt",
    )(h, x1, yb, *wlist, tri, mask)


def _mixer_sample_body(*refs, sb):
    h_ref, x1_ref, yb_ref, shift_in_ref, wkv_in_ref = refs[0:5]
    nwt = len(_W_NAMES)
    wrefs = dict(zip(_W_NAMES, refs[5:5 + nwt]))
    x2_ref, shift_ref, wkv_ref = refs[5 + nwt:8 + nwt]
    (o_scr,) = refs[8 + nwt:]
    w = {n: wrefs[n][...] for n in _W_NAMES if n != 'w_f'}

    h = h_ref[...]
    f = _dot(h, wrefs['w_f'][...])
    fm = f + w['mu'] * (shift_in_ref[...] - f)
    shift_ref[...] = f
    r, logw, k2, v, kk, b, g = _rwkv_pre(fm, w)
    dec = jnp.exp(logw)
    row8 = lax.broadcasted_iota(jnp.int32, (8, HEAD), 0)

    idx = [(i, hd) for i in range(sb) for hd in range(HEADS)]
    cut = lambda a, i, hd: a[i:i + 1, hd * HEAD:(hd + 1) * HEAD]
    bc8 = lambda a, i, hd: jnp.broadcast_to(cut(a, i, hd), (8, HEAD))
    sa8 = {}
    for i, hd in idx:
        st = wkv_in_ref[i, hd]
        sa8[i, hd] = -_dot_nt(bc8(kk, i, hd).astype(BF16), st.astype(BF16))
    for i, hd in idx:
        xm = jnp.where(row8 == 0, bc8(v, i, hd), jnp.where(row8 == 1, sa8[i, hd], 0.0))
        ym = jnp.where(row8 == 0, bc8(k2, i, hd), jnp.where(row8 == 1, bc8(b, i, hd), 0.0))
        wkv_ref[i, hd] = (wkv_in_ref[i, hd] * cut(dec, i, hd)
                          + _dot_tn(xm, ym, precision=lax.Precision.HIGHEST))
    for i, hd in idx:
        o8 = _dot_nt(bc8(r, i, hd).astype(BF16), wkv_ref[i, hd].astype(BF16))
        o_scr[i:i + 1, hd * HEAD:(hd + 1) * HEAD] = o8[0:1, :]

    y_a = _rwkv_post(o_scr[...], r, k2, v, g, w)
    x2_ref[...] = _merge(h, x1_ref[...], y_a, yb_ref[...], w)


def _mixer_sample(h, x1, yb, shift_in, wkv_in, wts, *, sb):
    n, d = h.shape
    assert n % sb == 0
    nf = wts['w_f'].shape[1]
    wlist = [wts[nm] for nm in _W_NAMES]
    row = lambda width: pl.BlockSpec((sb, width), lambda i: (i, 0))
    st_spec = pl.BlockSpec((sb, HEADS, HEAD, HEAD), lambda i: (i, 0, 0, 0))
    return pl.pallas_call(
        functools.partial(_mixer_sample_body, sb=sb),
        grid=(n // sb,),
        in_specs=[row(d), row(d), row(RW), row(nf), st_spec] + [_const_spec(a.shape) for a in wlist],
        out_specs=[row(d), row(nf), st_spec],
        out_shape=[jax.ShapeDtypeStruct((n, d), F32),
                   jax.ShapeDtypeStruct((n, nf), F32),
                   jax.ShapeDtypeStruct(wkv_in.shape, F32)],
        scratch_shapes=[pltpu.VMEM((sb, RW), F32)],
        compiler_params=_cparams(1),
        name="mixer_sample",
    )(h, x1, yb, shift_in, wkv_in, *wlist)


def _chunk_constants():
    t = jnp.arange(CHUNK)
    incl = (t[:, None] >= t[None, :])
    strict = (t[:, None] > t[None, :])
    tri = jnp.concatenate([incl, strict.T], axis=0).astype(BF16)
    top = jnp.concatenate([strict, strict], axis=1)
    bot = jnp.concatenate([incl, incl], axis=1)
    mask = jnp.concatenate([top, bot], axis=0).astype(F32)
    return tri, mask


def kernel(x_prompt, x_sample, state_shift, state_wkv, state_s5_re, state_s5_im, g_ffn1, ffn1_gate, ffn1_up, ffn1_down, g_mix, w_in, mu_shift, w0, w_up, a0, a_up, g_up, k_k, k_a, r_k, lnx_w, lnx_b, A_re, A_im, log_dt, B_re, B_im, C_re, C_im, D_skip, w_glu, w_a_up, w_b_up, w_out, g_ffn2, ffn2_gate, ffn2_up, ffn2_down, g_final):
    depth = g_ffn1.shape[0]
    assert depth == 1
    bp, seq, d = x_prompt.shape
    bs = x_sample.shape[0]
    assert x_sample.shape[1] == 1
    ncols_f = mu_shift.shape[1]
    su = D_skip.shape[1]
    groups, pstate = A_re.shape[1:]
    nstate = groups * pstate
    bf = lambda a: a.astype(BF16)
    row = lambda a: a.reshape(1, -1).astype(F32)

    w_in0 = w_in[0]
    eye_h = jnp.eye(HEADS, dtype=F32)
    wts = {
        'w_f': bf(w_in0[:, :ncols_f]),
        'w_g': bf(w_in0[:, ncols_f + su:]),
        'mu': row(mu_shift[0]), 'w0': row(w0[0]), 'w_up': bf(w_up[0]), 'a0': row(a0[0]),
        'a_up': bf(a_up[0]), 'g_up': bf(g_up[0]), 'k_k': row(k_k[0]), 'k_a': row(k_a[0]),
        'r_k': row(r_k[0]), 'lnx_w': row(lnx_w[0]), 'lnx_b': row(lnx_b[0]),
        'e': bf(jnp.kron(eye_h, jnp.ones((HEAD, HEAD), F32))),
        'w_a_up': bf(w_a_up[0]), 'w_b_up': bf(w_b_up[0]), 'w_out': bf(w_out[0]),
    }
    w_u = bf(w_in0[:, ncols_f:ncols_f + su])
    abr, abi, bbr, bbi = _s5_params(A_re[0], A_im[0], log_dt[0],
                                    jnp.swapaxes(B_re[0], 1, 2), jnp.swapaxes(B_im[0], 1, 2))
    bb = bf(jnp.concatenate([_block_diag(bbr, S5_BLOCKS), _block_diag(bbi, S5_BLOCKS)], axis=2))
    cm = bf(jnp.concatenate([_block_diag(jnp.swapaxes(C_re[0], 1, 2), S5_BLOCKS),
                             _block_diag(-jnp.swapaxes(C_im[0], 1, 2), S5_BLOCKS)], axis=1))
    abr = abr.reshape(1, nstate)
    abi = abi.reshape(1, nstate)
    dsk = row(D_skip[0])
    wglu = bf(w_glu[0])
    ffn1 = (row(g_ffn1[0]), bf(ffn1_gate[0]), bf(ffn1_up[0]), bf(ffn1_down[0]), row(g_mix[0]))
    ffn2 = (row(g_ffn2[0]), bf(ffn2_gate[0]), bf(ffn2_up[0]), bf(ffn2_down[0]), row(g_final))
    fc = 4 * MXU_TILE_V7X
    tri, mask = _chunk_constants()

    xp = x_prompt.reshape(bp * seq, d)
    x1p, hnp = _ffn(xp, *ffn1, final=False, tm=1024, fc=fc)
    hnp = hnp.reshape(bp, seq, d)
    zeros_state = jnp.zeros((bp, nstate), F32)
    yb_p, s5r_p, s5i_p = _s5(hnp, w_u, bb, cm, abr, abi, dsk, wglu, zeros_state, zeros_state,
                             nb=bp, steps=64)
    x2p, shift_p, wkv_p = _mixer_prompt(hnp, x1p.reshape(bp, seq, d), yb_p, wts, tri, mask, tc=512)
    y_prompt = _ffn(x2p.reshape(bp * seq, d), *ffn2, final=True, tm=1024, fc=fc).reshape(bp, seq, d)

    xs = x_sample.reshape(bs, d)
    x1s, hns = _ffn(xs, *ffn1, final=False, tm=bs, fc=fc)
    yb_s, s5r_s, s5i_s = _s5(hns[None], w_u, bb, cm, abr, abi, dsk, wglu,
                             state_s5_re[0].reshape(bs, nstate), state_s5_im[0].reshape(bs, nstate),
                             nb=bs, steps=1)
    x2s, shift_s, wkv_s = _mixer_sample(hns, x1s, yb_s[0], state_shift[0], state_wkv[0], wts, sb=16)
    y_sample = _ffn(x2s, *ffn2, final=True, tm=bs, fc=fc).reshape(bs, 1, d)

    st5 = lambda a, n: a.reshape(1, n, groups, pstate)
    return (y_prompt, y_sample,
            shift_p.reshape(1, bp, ncols_f), wkv_p[None], st5(s5r_p, bp), st5(s5i_p, bp),
            shift_s[None], wkv_s[None], st5(s5r_s, bs), st5(s5i_s, bs))
```

```python
import functools

import jax
import jax.numpy as jnp
from jax import lax
from jax.experimental import pallas as pl
from jax.experimental.pallas import tpu as pltpu

F32 = jnp.float32
BF16 = jnp.bfloat16

NORM_EPS = 1e-6
LNX_EPS = 64e-5
HEAD = 64
HEADS = 8
RW = HEAD * HEADS
S5_BLOCKS = 4
CHUNK = 64
CHUNKS_PER_TRIP = 4
VMEM_LIMIT = 56 * 1024 * 1024
MXU_TILE_V7X = 256


def _dot(a, b):
    return jnp.dot(a, b, preferred_element_type=F32)


def _dot_nt(a, b):
    return lax.dot_general(a, b, (((1,), (1,)), ((), ())), preferred_element_type=F32)


def _dot_tn(a, b, precision=None):
    return lax.dot_general(a, b, (((0,), (0,)), ((), ())), preferred_element_type=F32,
                           precision=precision)


def _rms(x, g):
    ms = jnp.mean(x * x, axis=-1, keepdims=True)
    return x * lax.rsqrt(ms + NORM_EPS) * g


def _const_spec(shape):
    nd = len(shape)
    return pl.BlockSpec(shape, lambda *_: (0,) * nd, pipeline_mode=pl.Buffered(1))


def _cparams(ngrid):
    return pltpu.CompilerParams(dimension_semantics=("arbitrary",) * ngrid,
                                vmem_limit_bytes=VMEM_LIMIT)


def _ffn_body(x_ref, g_ref, wg_ref, wu_ref, wd_ref, gn_ref, *outs, final, fc):
    x = x_ref[...]
    hn = _rms(x, g_ref[...]).astype(BF16)
    acc = None
    f = wg_ref.shape[1]
    for c0 in range(0, f, fc):
        c1 = min(c0 + fc, f)
        gate = _dot(hn, wg_ref[:, c0:c1])
        up = _dot(hn, wu_ref[:, c0:c1])
        act = (jax.nn.silu(gate) * up).astype(BF16)
        part = _dot(act, wd_ref[c0:c1, :])
        acc = part if acc is None else acc + part
    x1 = x + 0.5 * acc
    if final:
        outs[0][...] = _rms(x1, gn_ref[...])
    else:
        outs[0][...] = x1
        outs[1][...] = _rms(x1, gn_ref[...]).astype(BF16)


def _ffn(x, g, wg, wu, wd, gn, *, final, tm, fc):
    m, d = x.shape
    f = wg.shape[1]
    assert m % tm == 0 and fc % MXU_TILE_V7X == 0
    tile = pl.BlockSpec((tm, d), lambda i: (i, 0))
    if final:
        out_shape = jax.ShapeDtypeStruct((m, d), F32)
        out_specs = tile
    else:
        out_shape = [jax.ShapeDtypeStruct((m, d), F32), jax.ShapeDtypeStruct((m, d), BF16)]
        out_specs = [tile, tile]
    return pl.pallas_call(
        functools.partial(_ffn_body, final=final, fc=fc),
        grid=(m // tm,),
        in_specs=[tile, _const_spec(g.shape), _const_spec(wg.shape), _const_spec(wu.shape),
                  _const_spec(wd.shape), _const_spec(gn.shape)],
        out_specs=out_specs,
        out_shape=out_shape,
        compiler_params=_cparams(1),
        name="ffn_final" if final else "ffn_in",
    )(x, g, wg, wu, wd, gn)


def _s5_param_body(are_ref, aim_ref, ldt_ref, bre_ref, bim_ref, abr_ref, abi_ref, bbr_ref, bbi_ref):
    lam_r = are_ref[...]
    lam_i = aim_ref[...]
    dt = jnp.exp(ldt_ref[...])
    mag = jnp.exp(lam_r * dt)
    ab_r = mag * jnp.cos(lam_i * dt)
    ab_i = mag * jnp.sin(lam_i * dt)
    den = lam_r * lam_r + lam_i * lam_i
    q_r = ((ab_r - 1.0) * lam_r + ab_i * lam_i) / den
    q_i = (ab_i * lam_r - (ab_r - 1.0) * lam_i) / den
    abr_ref[...] = ab_r
    abi_ref[...] = ab_i
    bre = bre_ref[...]
    bim = bim_ref[...]
    bbr_ref[...] = q_r[:, None, :] * bre - q_i[:, None, :] * bim
    bbi_ref[...] = q_r[:, None, :] * bim + q_i[:, None, :] * bre


def _s5_params(a_re, a_im, log_dt, b_re_t, b_im_t):
    g, p = a_re.shape
    hg = b_re_t.shape[1]
    return pl.pallas_call(
        _s5_param_body,
        out_shape=[jax.ShapeDtypeStruct((g, p), F32), jax.ShapeDtypeStruct((g, p), F32),
                   jax.ShapeDtypeStruct((g, hg, p), F32), jax.ShapeDtypeStruct((g, hg, p), F32)],
        name="s5_params",
    )(a_re, a_im, log_dt.reshape(g, 1), b_re_t, b_im_t)


def _block_diag(w, nblk):
    g, a, b = w.shape
    gb = g // nblk
    w4 = w.reshape(nblk, gb, a, b)
    eye = jnp.eye(gb, dtype=w.dtype)
    return jnp.einsum('jgab,gk->jgakb', w4, eye).reshape(nblk, gb * a, gb * b)


def _s5_body(h_ref, perm_ref, wu_ref, bb_ref, cm_ref, abr_ref, abi_ref, dsk_ref, wglu_ref, x0r_ref,
             x0i_ref, yb_ref, xr_out, xi_out, bu_scr, y_scr, xr_scr, xi_scr, *, nb, steps):
    c = pl.program_id(0)
    rows = nb * steps

    @pl.when(c == 0)
    def _():
        xr_scr[...] = x0r_ref[...]
        xi_scr[...] = x0i_ref[...]

    h = h_ref[...].reshape(rows, h_ref.shape[-1])
    if steps > 1:
        h = _dot(perm_ref[0], h).astype(BF16)
    u = _dot(h, wu_ref[...])
    ub = u.astype(BF16)
    nin = ub.shape[1] // S5_BLOCKS
    ns = abr_ref.shape[1] // S5_BLOCKS
    for j in range(S5_BLOCKS):
        bu_scr[...] = _dot(ub[:, j * nin:(j + 1) * nin], bb_ref[j])
        ar = jnp.broadcast_to(abr_ref[:, j * ns:(j + 1) * ns], (nb, ns))
        ai = jnp.broadcast_to(abi_ref[:, j * ns:(j + 1) * ns], (nb, ns))

        def step(t, carry):
            xr, xi = carry
            r0 = pl.multiple_of(t * nb, nb)
            bur = bu_scr[pl.ds(r0, nb), 0:ns]
            bui = bu_scr[pl.ds(r0, nb), ns:2 * ns]
            nxr = ar * xr - ai * xi + bur
            nxi = ar * xi + ai * xr + bui
            bu_scr[pl.ds(r0, nb), 0:ns] = nxr
            bu_scr[pl.ds(r0, nb), ns:2 * ns] = nxi
            return nxr, nxi

        xr, xi = lax.fori_loop(0, steps, step,
                               (xr_scr[:, j * ns:(j + 1) * ns], xi_scr[:, j * ns:(j + 1) * ns]),
                               unroll=min(steps, 8))
        xr_scr[:, j * ns:(j + 1) * ns] = xr
        xi_scr[:, j * ns:(j + 1) * ns] = xi
        y_scr[:, j * nin:(j + 1) * nin] = _dot(bu_scr[...].astype(BF16), cm_ref[j])
    y = y_scr[...] + dsk_ref[...] * u
    z = jax.nn.gelu(y)
    yb = (z * jax.nn.sigmoid(_dot(z.astype(BF16), wglu_ref[...]))).astype(BF16)
    if steps > 1:
        yb = _dot(perm_ref[1], yb).astype(BF16)
    yb_ref[...] = yb.reshape(yb_ref.shape)
    xr_out[...] = xr_scr[...]
    xi_out[...] = xi_scr[...]


def _s5(h, w_u, bb, cm, abr, abi, dsk, wglu, x0r, x0i, *, nb, steps):
    g, seq, d = h.shape
    rows = nb * steps
    assert seq % (rows // g) == 0 and (g == nb or steps == 1)
    blk = rows // g
    su = w_u.shape[1]
    nst = abr.shape[1]
    r = jnp.arange(rows)
    to_tm = (r[:, None] % nb) * steps + r[:, None] // nb == r[None, :]
    perm = jnp.stack([to_tm, to_tm.T]).astype(BF16)
    return pl.pallas_call(
        functools.partial(_s5_body, nb=nb, steps=steps),
        grid=(seq // blk,),
        in_specs=[
            pl.BlockSpec((g, blk, d), lambda c: (0, c, 0)),
            _const_spec(perm.shape),
            _const_spec(w_u.shape), _const_spec(bb.shape), _const_spec(cm.shape),
            _const_spec(abr.shape), _const_spec(abi.shape), _const_spec(dsk.shape),
            _const_spec(wglu.shape), _const_spec(x0r.shape), _const_spec(x0i.shape),
        ],
        out_specs=[pl.BlockSpec((g, blk, su), lambda c: (0, c, 0)),
                   pl.BlockSpec((nb, nst), lambda c: (0, 0)),
                   pl.BlockSpec((nb, nst), lambda c: (0, 0))],
        out_shape=[jax.ShapeDtypeStruct((g, seq, su), BF16),
                   jax.ShapeDtypeStruct((nb, nst), F32),
                   jax.ShapeDtypeStruct((nb, nst), F32)],
        scratch_shapes=[pltpu.VMEM((rows, 2 * nst // S5_BLOCKS), F32),
                        pltpu.VMEM((rows, su), F32),
                        pltpu.VMEM((nb, nst), F32), pltpu.VMEM((nb, nst), F32)],
        compiler_params=_cparams(1),
        name="s5_branch",
    )(h, perm, w_u, bb, cm, abr, abi, dsk, wglu, x0r, x0i)


def _headsum(x, e):
    xb = x.astype(BF16)
    t = MXU_TILE_V7X
    return jnp.concatenate([_dot(xb[:, c:c + t], e[c:c + t, c:c + t]) for c in range(0, RW, t)], axis=1)


def _split3(x):
    hi = x.astype(BF16)
    r1 = x - hi.astype(F32)
    mid = r1.astype(BF16)
    lo = (r1 - mid.astype(F32)).astype(BF16)
    return hi, mid, lo


def _rwkv_pre(fm, w):
    r = fm[:, 0:RW]
    k = fm[:, RW:2 * RW]
    v = fm[:, 2 * RW:3 * RW]
    o0 = 3 * RW
    nw = w['w_up'].shape[0]
    na = w['a_up'].shape[0]
    wd = fm[:, o0:o0 + nw]
    ad = fm[:, o0 + nw:o0 + nw + na]
    gd = fm[:, o0 + nw + na:]
    w_log = -jax.nn.softplus(-(w['w0'] + _dot(jnp.tanh(wd).astype(BF16), w['w_up']))) - 0.5
    logw = -jnp.exp(w_log)
    a = jax.nn.sigmoid(w['a0'] + _dot(ad.astype(BF16), w['a_up']))
    g = _dot(jax.nn.sigmoid(gd).astype(BF16), w['g_up'])
    kk = k * w['k_k']
    kk = kk / jnp.maximum(jnp.sqrt(_headsum(kk * kk, w['e'])), 1e-12)
    k2 = k * (1.0 + (a - 1.0) * w['k_a'])
    return r, logw, k2, v, kk, kk * a, g


def _rwkv_post(o, r, k2, v, g, w):
    inv_n = 1.0 / HEAD
    mu = _headsum(o, w['e']) * inv_n
    d = o - mu
    var = _headsum(d * d, w['e']) * inv_n
    on = d * lax.rsqrt(var + LNX_EPS) * w['lnx_w'] + w['lnx_b']
    bonus = _headsum(r * k2 * w['r_k'], w['e']) * v
    return (on + bonus) * g


def _sigmoid(x):
    return 0.5 * jnp.tanh(0.5 * x) + 0.5


def _merge_gates(h, y_b, w):
    gates = _sigmoid(_dot(h, w['w_g']))
    dm = gates.shape[1] // 2
    return gates[:, :dm], gates[:, dm:] * _dot(y_b.astype(BF16), w['w_b_up'])


def _merge_out(x1, y_a, gate_a, gated_b, w):
    merged = gate_a * _dot(y_a.astype(BF16), w['w_a_up']) + gated_b
    return x1 + _dot(merged.astype(BF16), w['w_out'])


_W_NAMES = ('w_f', 'w_g', 'mu', 'w0', 'w_up', 'a0', 'a_up', 'g_up', 'k_k', 'k_a', 'r_k',
            'lnx_w', 'lnx_b', 'e', 'w_a_up', 'w_b_up', 'w_out')


def _mixer_prompt_body(*refs, tc):
    h_ref, x1_ref, yb_ref = refs[0:3]
    nwt = len(_W_NAMES)
    wrefs = dict(zip(_W_NAMES, refs[3:3 + nwt]))
    tri_ref, mask_ref = refs[3 + nwt:5 + nwt]
    x2_ref, shift_ref, wkv_ref = refs[5 + nwt:8 + nwt]
    (carry_scr, st_scr, lhs_scr, rhs_scr, sfx_scr, v_scr, gt_scr, o_scr, ga_scr,
     gb_scr) = refs[8 + nwt:]
    c = pl.program_id(1)
    w = {n: wrefs[n][...] for n in _W_NAMES if n != 'w_f'}

    @pl.when(c == 0)
    def _():
        carry_scr[...] = jnp.zeros_like(carry_scr)
        st_scr[...] = jnp.zeros_like(st_scr)

    h = h_ref[...]
    ga_scr[...], gb_scr[...] = _merge_gates(h, yb_ref[...], w)
    f = _dot(h, wrefs['w_f'][...])
    rows = lax.broadcasted_iota(jnp.int32, f.shape, 0)
    shifted = jnp.where(rows == 0, carry_scr[...], pltpu.roll(f, 1, 0))
    carry_scr[...] = f[tc - 1:tc, :]
    fm = f + w['mu'] * (shifted - f)
    r, logw, k2, v, kk, b, g = _rwkv_pre(fm, w)

    ns = tc // CHUNK
    tri = tri_ref[...]
    for s in range(ns):
        sl = slice(s * CHUNK, (s + 1) * CHUNK)
        lw = logw[sl]
        hi, mid, lo = _split3(lw)
        lpre = _dot(tri, hi) + _dot(tri, mid) + _dot(tri, lo)
        lsuf = lpre[CHUNK - 1:CHUNK, :] - lpre
        e_pre = jnp.exp(lpre)
        e_neg = jnp.exp(-lpre)
        e_suf = jnp.exp(lsuf)
        lhs = jnp.concatenate([kk[sl] * jnp.exp(lpre - lw), r[sl] * e_pre], axis=0).astype(BF16)
        rhs = jnp.concatenate([b[sl] * e_neg, k2[sl] * e_neg], axis=0).astype(BF16)
        sfx = jnp.concatenate([b[sl] * e_suf, k2[sl] * e_suf], axis=0).astype(BF16)
        vb = v[sl].astype(BF16)
        for hd in range(HEADS):
            ls = slice(hd * HEAD, (hd + 1) * HEAD)
            lhs_scr[hd, s] = lhs[:, ls]
            rhs_scr[hd, s] = rhs[:, ls]
            sfx_scr[hd, s] = sfx[:, ls]
            v_scr[hd, s] = vb[:, ls]
        gt_scr[s] = jnp.broadcast_to(e_pre[CHUNK - 1:CHUNK, :], (8, RW))

    mask = mask_ref[...] > 0.5
    eye = (lax.broadcasted_iota(jnp.int32, (CHUNK, CHUNK), 0)
           == lax.broadcasted_iota(jnp.int32, (CHUNK, CHUNK), 1)).astype(F32)
    n_double = CHUNK.bit_length() - 2
    heads = range(HEADS)
    units = [(j, hd) for j in range(CHUNKS_PER_TRIP) for hd in heads]

    def chunk_step(i, carry):
        s0 = i * CHUNKS_PER_TRIP
        lhs = {u: lhs_scr[u[1], s0 + u[0]] for u in units}
        rhs = {u: rhs_scr[u[1], s0 + u[0]] for u in units}
        a_mat = {u: jnp.where(mask, _dot_nt(lhs[u], rhs[u]), 0.0) for u in units}
        p = {u: -a_mat[u][:CHUNK, :CHUNK] for u in units}
        winv = {u: eye + p[u] for u in units}
        for _ in range(n_double):
            p = {u: _dot(p[u].astype(BF16), p[u].astype(BF16)) for u in units}
            winv = {u: winv[u] + _dot(winv[u].astype(BF16), p[u].astype(BF16)) for u in units}
        vv = {u: v_scr[u[1], s0 + u[0]] for u in units}
        mkv = {u: _dot(a_mat[u][:CHUNK, CHUNK:].astype(BF16), vv[u]).astype(BF16) for u in units}
        wb = {u: winv[u].astype(BF16) for u in units}
        gb = {u: _dot(wb[u], lhs[u][:CHUNK]).astype(BF16) for u in units}
        u0b = {u: (-_dot(wb[u], mkv[u])).astype(BF16) for u in units}
        n_b = {u: a_mat[u][CHUNK:, :CHUNK].astype(BF16) for u in units}
        q = {u: (lhs[u][CHUNK:].astype(F32) - _dot(n_b[u], gb[u])).astype(BF16) for u in units}
        o0 = {u: _dot(n_b[u], u0b[u]) + _dot(a_mat[u][CHUNK:, CHUNK:].astype(BF16), vv[u])
              for u in units}
        sfx = {u: sfx_scr[u[1], s0 + u[0]] for u in units}
        gt8 = [gt_scr[s0 + j] for j in range(CHUNKS_PER_TRIP)]
        phi = {u: (eye * gt8[u[0]][0:1, u[1] * HEAD:(u[1] + 1) * HEAD]
                   - _dot_tn(sfx[u][:CHUNK], gb[u])).astype(BF16) for u in units}
        psi = {u: _dot_tn(sfx[u][:CHUNK], u0b[u]) + _dot_tn(sfx[u][CHUNK:], vv[u])
               for u in units}
        st = [st_scr[hd] for hd in heads]
        for j in range(CHUNKS_PER_TRIP):
            rs = pl.ds(pl.multiple_of((s0 + j) * CHUNK, CHUNK), CHUNK)
            stb = [st[hd].astype(BF16) for hd in heads]
            for hd in heads:
                o_scr[rs, hd * HEAD:(hd + 1) * HEAD] = _dot(q[j, hd], stb[hd]) + o0[j, hd]
            st = [_dot(phi[j, hd], stb[hd]) + psi[j, hd] for hd in heads]
        for hd in heads:
            st_scr[hd] = st[hd]
        return carry

    lax.fori_loop(0, ns // CHUNKS_PER_TRIP, chunk_step, 0)

    y_a = _rwkv_post(o_scr[...], r, k2, v, g, w)
    x2_ref[...] = _merge_out(x1_ref[...], y_a, ga_scr[...], gb_scr[...], w)

    @pl.when(c == pl.num_programs(1) - 1)
    def _():
        shift_ref[...] = f[tc - 1:tc, :]
        for hd in range(HEADS):
            wkv_ref[hd] = st_scr[hd].T


def _mixer_prompt(h, x1, yb, wts, tri, mask, *, tc):
    bsz, seq, d = h.shape
    assert seq % tc == 0 and tc % CHUNK == 0
    nf = wts['w_f'].shape[1]
    wlist = [wts[n] for n in _W_NAMES]
    tok = lambda width: pl.BlockSpec((None, tc, width), lambda b, c: (b, c, 0))
    return pl.pallas_call(
        functools.partial(_mixer_prompt_body, tc=tc),
        grid=(bsz, seq // tc),
        in_specs=[tok(d), tok(d), tok(RW)] + [_const_spec(a.shape) for a in wlist]
                 + [_const_spec(tri.shape), _const_spec(mask.shape)],
        out_specs=[tok(d),
                   pl.BlockSpec((None, 1, nf), lambda b, c: (b, 0, 0)),
                   pl.BlockSpec((None, HEADS, HEAD, HEAD), lambda b, c: (b, 0, 0, 0))],
        out_shape=[jax.ShapeDtypeStruct((bsz, seq, d), F32),
                   jax.ShapeDtypeStruct((bsz, 1, nf), F32),
                   jax.ShapeDtypeStruct((bsz, HEADS, HEAD, HEAD), F32)],
        scratch_shapes=[pltpu.VMEM((1, nf), F32),
                        pltpu.VMEM((HEADS, HEAD, HEAD), F32)]
                       + [pltpu.VMEM((HEADS, tc // CHUNK, 2 * CHUNK, HEAD), BF16)] * 3
                       + [pltpu.VMEM((HEADS, tc // CHUNK, CHUNK, HEAD), BF16),
                          pltpu.VMEM((tc // CHUNK, 8, RW), F32),
                          pltpu.VMEM((tc, RW), F32),
                          pltpu.VMEM((tc, d), F32), pltpu.VMEM((tc, d), F32)],
        compiler_params=_cparams(2),
        name="mixer_prompt",
    )(h, x1, yb, *wlist, tri, mask)


def _mixer_sample_body(*refs, sb):
    h_ref, x1_ref, yb_ref, shift_in_ref, wkv_in_ref = refs[0:5]
    nwt = len(_W_NAMES)
    wrefs = dict(zip(_W_NAMES, refs[5:5 + nwt]))
    x2_ref, shift_ref, wkv_ref = refs[5 + nwt:8 + nwt]
    (o_scr,) = refs[8 + nwt:]
    w = {n: wrefs[n][...] for n in _W_NAMES if n != 'w_f'}

    h = h_ref[...]
    f = _dot(h, wrefs['w_f'][...])
    fm = f + w['mu'] * (shift_in_ref[...] - f)
    shift_ref[...] = f
    r, logw, k2, v, kk, b, g = _rwkv_pre(fm, w)
    dec = jnp.exp(logw)
    row8 = lax.broadcasted_iota(jnp.int32, (8, HEAD), 0)

    idx = [(i, hd) for i in range(sb) for hd in range(HEADS)]
    cut = lambda a, i, hd: a[i:i + 1, hd * HEAD:(hd + 1) * HEAD]
    bc8 = lambda a, i, hd: jnp.broadcast_to(cut(a, i, hd), (8, HEAD))
    sa8 = {}
    for i, hd in idx:
        st = wkv_in_ref[i, hd]
        sa8[i, hd] = -_dot_nt(bc8(kk, i, hd).astype(BF16), st.astype(BF16))
    for i, hd in idx:
        xm = jnp.where(row8 == 0, bc8(v, i, hd), jnp.where(row8 == 1, sa8[i, hd], 0.0))
        ym = jnp.where(row8 == 0, bc8(k2, i, hd), jnp.where(row8 == 1, bc8(b, i, hd), 0.0))
        wkv_ref[i, hd] = (wkv_in_ref[i, hd] * cut(dec, i, hd)
                          + _dot_tn(xm, ym, precision=lax.Precision.HIGHEST))
    for i, hd in idx:
        o8 = _dot_nt(bc8(r, i, hd).astype(BF16), wkv_ref[i, hd].astype(BF16))
        o_scr[i:i + 1, hd * HEAD:(hd + 1) * HEAD] = o8[0:1, :]

    y_a = _rwkv_post(o_scr[...], r, k2, v, g, w)
    gate_a, gated_b = _merge_gates(h, yb_ref[...], w)
    x2_ref[...] = _merge_out(x1_ref[...], y_a, gate_a, gated_b, w)


def _mixer_sample(h, x1, yb, shift_in, wkv_in, wts, *, sb):
    n, d = h.shape
    assert n % sb == 0
    nf = wts['w_f'].shape[1]
    wlist = [wts[nm] for nm in _W_NAMES]
    row = lambda width: pl.BlockSpec((sb, width), lambda i: (i, 0))
    st_spec = pl.BlockSpec((sb, HEADS, HEAD, HEAD), lambda i: (i, 0, 0, 0))
    return pl.pallas_call(
        functools.partial(_mixer_sample_body, sb=sb),
        grid=(n // sb,),
        in_specs=[row(d), row(d), row(RW), row(nf), st_spec] + [_const_spec(a.shape) for a in wlist],
        out_specs=[row(d), row(nf), st_spec],
        out_shape=[jax.ShapeDtypeStruct((n, d), F32),
                   jax.ShapeDtypeStruct((n, nf), F32),
                   jax.ShapeDtypeStruct(wkv_in.shape, F32)],
        scratch_shapes=[pltpu.VMEM((sb, RW), F32)],
        compiler_params=_cparams(1),
        name="mixer_sample",
    )(h, x1, yb, shift_in, wkv_in, *wlist)


def _chunk_constants():
    t = jnp.arange(CHUNK)
    incl = (t[:, None] >= t[None, :])
    strict = (t[:, None] > t[None, :])
    tri = incl.astype(BF16)
    top = jnp.concatenate([strict, strict], axis=1)
    bot = jnp.concatenate([incl, incl], axis=1)
    mask = jnp.concatenate([top, bot], axis=0).astype(F32)
    return tri, mask


def kernel(x_prompt, x_sample, state_shift, state_wkv, state_s5_re, state_s5_im, g_ffn1, ffn1_gate, ffn1_up, ffn1_down, g_mix, w_in, mu_shift, w0, w_up, a0, a_up, g_up, k_k, k_a, r_k, lnx_w, lnx_b, A_re, A_im, log_dt, B_re, B_im, C_re, C_im, D_skip, w_glu, w_a_up, w_b_up, w_out, g_ffn2, ffn2_gate, ffn2_up, ffn2_down, g_final):
    depth = g_ffn1.shape[0]
    assert depth == 1
    bp, seq, d = x_prompt.shape
    bs = x_sample.shape[0]
    assert x_sample.shape[1] == 1
    ncols_f = mu_shift.shape[1]
    su = D_skip.shape[1]
    groups, pstate = A_re.shape[1:]
    nstate = groups * pstate
    bf = lambda a: a.astype(BF16)
    row = lambda a: a.reshape(1, -1).astype(F32)

    w_in0 = w_in[0]
    eye_h = jnp.eye(HEADS, dtype=F32)
    wts = {
        'w_f': bf(w_in0[:, :ncols_f]),
        'w_g': bf(w_in0[:, ncols_f + su:]),
        'mu': row(mu_shift[0]), 'w0': row(w0[0]), 'w_up': bf(w_up[0]), 'a0': row(a0[0]),
        'a_up': bf(a_up[0]), 'g_up': bf(g_up[0]), 'k_k': row(k_k[0]), 'k_a': row(k_a[0]),
        'r_k': row(r_k[0]), 'lnx_w': row(lnx_w[0]), 'lnx_b': row(lnx_b[0]),
        'e': bf(jnp.kron(eye_h, jnp.ones((HEAD, HEAD), F32))),
        'w_a_up': bf(w_a_up[0]), 'w_b_up': bf(w_b_up[0]), 'w_out': bf(w_out[0]),
    }
    w_u = bf(w_in0[:, ncols_f:ncols_f + su])
    abr, abi, bbr, bbi = _s5_params(A_re[0], A_im[0], log_dt[0],
                                    jnp.swapaxes(B_re[0], 1, 2), jnp.swapaxes(B_im[0], 1, 2))
    bb = bf(jnp.concatenate([_block_diag(bbr, S5_BLOCKS), _block_diag(bbi, S5_BLOCKS)], axis=2))
    cm = bf(jnp.concatenate([_block_diag(jnp.swapaxes(C_re[0], 1, 2), S5_BLOCKS),
                             _block_diag(-jnp.swapaxes(C_im[0], 1, 2), S5_BLOCKS)], axis=1))
    abr = abr.reshape(1, nstate)
    abi = abi.reshape(1, nstate)
    dsk = row(D_skip[0])
    wglu = bf(w_glu[0])
    ffn1 = (row(g_ffn1[0]), bf(ffn1_gate[0]), bf(ffn1_up[0]), bf(ffn1_down[0]), row(g_mix[0]))
    ffn2 = (row(g_ffn2[0]), bf(ffn2_gate[0]), bf(ffn2_up[0]), bf(ffn2_down[0]), row(g_final))
    fc = 4 * MXU_TILE_V7X
    tri, mask = _chunk_constants()

    xp = x_prompt.reshape(bp * seq, d)
    x1p, hnp = _ffn(xp, *ffn1, final=False, tm=1024, fc=fc)
    hnp = hnp.reshape(bp, seq, d)
    zeros_state = jnp.zeros((bp, nstate), F32)
    yb_p, s5r_p, s5i_p = _s5(hnp, w_u, bb, cm, abr, abi, dsk, wglu, zeros_state, zeros_state,
                             nb=bp, steps=64)
    x2p, shift_p, wkv_p = _mixer_prompt(hnp, x1p.reshape(bp, seq, d), yb_p, wts, tri, mask, tc=512)
    y_prompt = _ffn(x2p.reshape(bp * seq, d), *ffn2, final=True, tm=1024, fc=fc).reshape(bp, seq, d)

    xs = x_sample.reshape(bs, d)
    x1s, hns = _ffn(xs, *ffn1, final=False, tm=bs, fc=fc)
    yb_s, s5r_s, s5i_s = _s5(hns[None], w_u, bb, cm, abr, abi, dsk, wglu,
                             state_s5_re[0].reshape(bs, nstate), state_s5_im[0].reshape(bs, nstate),
                             nb=bs, steps=1)
    x2s, shift_s, wkv_s = _mixer_sample(hns, x1s, yb_s[0], state_shift[0], state_wkv[0], wts, sb=16)
    y_sample = _ffn(x2s, *ffn2, final=True, tm=bs, fc=fc).reshape(bs, 1, d)

    st5 = lambda a, n: a.reshape(1, n, groups, pstate)
    return (y_prompt, y_sample,
            shift_p.reshape(1, bp, ncols_f), wkv_p[None], st5(s5r_p, bp), st5(s5i_p, bp),
            shift_s[None], wkv_s[None], st5(s5r_s, bs), st5(s5i_s, bs))
```

```python
import functools
import math

import jax
import jax.numpy as jnp
from jax import lax
from jax.experimental import pallas as pl
from jax.experimental.pallas import tpu as pltpu

F32 = jnp.float32
BF16 = jnp.bfloat16

NORM_EPS = 1e-6
LNX_EPS = 64e-5
HEAD = 64
HEADS = 8
RW = HEAD * HEADS
S5_BLOCKS = 4
CHUNK = 64
CHUNKS_PER_TRIP = 4
VMEM_LIMIT = 56 * 1024 * 1024
MXU_TILE_V7X = 256
CAST_BLOCK_BYTES = 4 * 1024 * 1024


def _dot(a, b):
    return jnp.dot(a, b, preferred_element_type=F32)


def _dot_nt(a, b):
    return lax.dot_general(a, b, (((1,), (1,)), ((), ())), preferred_element_type=F32)


def _dot_tn(a, b, precision=None):
    return lax.dot_general(a, b, (((0,), (0,)), ((), ())), preferred_element_type=F32,
                           precision=precision)


def _rms(x, g):
    ms = jnp.mean(x * x, axis=-1, keepdims=True)
    return x * lax.rsqrt(ms + NORM_EPS) * g


def _const_spec(shape):
    nd = len(shape)
    return pl.BlockSpec(shape, lambda *_: (0,) * nd, pipeline_mode=pl.Buffered(1))


def _cparams(ngrid):
    return pltpu.CompilerParams(dimension_semantics=("arbitrary",) * ngrid,
                                vmem_limit_bytes=VMEM_LIMIT)


def _cast_body(x_ref, o_ref):
    o_ref[...] = x_ref[...].astype(o_ref.dtype)


def _to_bf16(x, col0=0, ncols=None):
    rows, width = x.shape
    ncols = width - col0 if ncols is None else ncols
    bw = math.gcd(col0, ncols) if col0 else ncols
    cap = max(8, CAST_BLOCK_BYTES // (4 * bw) // 8 * 8)
    tr = next(t for t in range(min(rows, cap), 7, -8) if rows % t == 0)
    assert bw % 128 == 0
    return pl.pallas_call(
        _cast_body,
        grid=(rows // tr, ncols // bw),
        in_specs=[pl.BlockSpec((tr, bw), lambda i, j: (i, j + col0 // bw))],
        out_specs=pl.BlockSpec((tr, bw), lambda i, j: (i, j)),
        out_shape=jax.ShapeDtypeStruct((rows, ncols), BF16),
        compiler_params=_cparams(2),
        name="to_bf16",
    )(x)


def _ffn_body(x_ref, g_ref, wg_ref, wu_ref, wd_ref, gn_ref, *outs, final, fc):
    x = x_ref[...]
    hn = _rms(x, g_ref[...]).astype(BF16)
    acc = None
    f = wg_ref.shape[1]
    for c0 in range(0, f, fc):
        c1 = min(c0 + fc, f)
        gate = _dot(hn, wg_ref[:, c0:c1])
        up = _dot(hn, wu_ref[:, c0:c1])
        act = (jax.nn.silu(gate) * up).astype(BF16)
        part = _dot(act, wd_ref[c0:c1, :])
        acc = part if acc is None else acc + part
    x1 = x + 0.5 * acc
    if final:
        outs[0][...] = _rms(x1, gn_ref[...])
    else:
        outs[0][...] = x1
        outs[1][...] = _rms(x1, gn_ref[...]).astype(BF16)


def _ffn(x, g, wg, wu, wd, gn, *, final, tm, fc):
    m, d = x.shape
    f = wg.shape[1]
    assert m % tm == 0 and fc % MXU_TILE_V7X == 0
    tile = pl.BlockSpec((tm, d), lambda i: (i, 0))
    if final:
        out_shape = jax.ShapeDtypeStruct((m, d), F32)
        out_specs = tile
    else:
        out_shape = [jax.ShapeDtypeStruct((m, d), F32), jax.ShapeDtypeStruct((m, d), BF16)]
        out_specs = [tile, tile]
    return pl.pallas_call(
        functools.partial(_ffn_body, final=final, fc=fc),
        grid=(m // tm,),
        in_specs=[tile, _const_spec(g.shape), _const_spec(wg.shape), _const_spec(wu.shape),
                  _const_spec(wd.shape), _const_spec(gn.shape)],
        out_specs=out_specs,
        out_shape=out_shape,
        compiler_params=_cparams(1),
        name="ffn_final" if final else "ffn_in",
    )(x, g, wg, wu, wd, gn)


def _s5_param_body(are_ref, aim_ref, ldt_ref, bre_ref, bim_ref, abr_ref, abi_ref, bbr_ref, bbi_ref):
    lam_r = are_ref[...]
    lam_i = aim_ref[...]
    dt = jnp.exp(ldt_ref[...])
    mag = jnp.exp(lam_r * dt)
    ab_r = mag * jnp.cos(lam_i * dt)
    ab_i = mag * jnp.sin(lam_i * dt)
    den = lam_r * lam_r + lam_i * lam_i
    q_r = ((ab_r - 1.0) * lam_r + ab_i * lam_i) / den
    q_i = (ab_i * lam_r - (ab_r - 1.0) * lam_i) / den
    abr_ref[...] = ab_r
    abi_ref[...] = ab_i
    bre = bre_ref[...]
    bim = bim_ref[...]
    bbr_ref[...] = q_r[:, None, :] * bre - q_i[:, None, :] * bim
    bbi_ref[...] = q_r[:, None, :] * bim + q_i[:, None, :] * bre


def _s5_params(a_re, a_im, log_dt, b_re_t, b_im_t):
    g, p = a_re.shape
    hg = b_re_t.shape[1]
    return pl.pallas_call(
        _s5_param_body,
        out_shape=[jax.ShapeDtypeStruct((g, p), F32), jax.ShapeDtypeStruct((g, p), F32),
                   jax.ShapeDtypeStruct((g, hg, p), F32), jax.ShapeDtypeStruct((g, hg, p), F32)],
        name="s5_params",
    )(a_re, a_im, log_dt.reshape(g, 1), b_re_t, b_im_t)


def _block_diag(w, nblk):
    g, a, b = w.shape
    gb = g // nblk
    w4 = w.reshape(nblk, gb, a, b)
    eye = jnp.eye(gb, dtype=w.dtype)
    return jnp.einsum('jgab,gk->jgakb', w4, eye).reshape(nblk, gb * a, gb * b)


def _s5_body(h_ref, perm_ref, wu_ref, bb_ref, cm_ref, abr_ref, abi_ref, dsk_ref, wglu_ref, x0r_ref,
             x0i_ref, yb_ref, xr_out, xi_out, bu_scr, y_scr, xr_scr, xi_scr, *, nb, steps):
    c = pl.program_id(0)
    rows = nb * steps

    @pl.when(c == 0)
    def _():
        xr_scr[...] = x0r_ref[...]
        xi_scr[...] = x0i_ref[...]

    h = h_ref[...].reshape(rows, h_ref.shape[-1])
    if steps > 1:
        h = _dot(perm_ref[0], h).astype(BF16)
    u = _dot(h, wu_ref[...])
    ub = u.astype(BF16)
    nin = ub.shape[1] // S5_BLOCKS
    ns = abr_ref.shape[1] // S5_BLOCKS
    for j in range(S5_BLOCKS):
        bu_scr[j] = _dot(ub[:, j * nin:(j + 1) * nin], bb_ref[j])
    for j in range(S5_BLOCKS):
        ar = jnp.broadcast_to(abr_ref[:, j * ns:(j + 1) * ns], (nb, ns))
        ai = jnp.broadcast_to(abi_ref[:, j * ns:(j + 1) * ns], (nb, ns))
        xr = xr_scr[:, j * ns:(j + 1) * ns]
        xi = xi_scr[:, j * ns:(j + 1) * ns]
        for t in range(steps):
            tr = slice(t * nb, (t + 1) * nb)
            xr, xi = (ar * xr - ai * xi + bu_scr[j, tr, 0:ns],
                      ar * xi + ai * xr + bu_scr[j, tr, ns:2 * ns])
            bu_scr[j, tr, 0:ns] = xr
            bu_scr[j, tr, ns:2 * ns] = xi
        xr_scr[:, j * ns:(j + 1) * ns] = xr
        xi_scr[:, j * ns:(j + 1) * ns] = xi
        y_scr[:, j * nin:(j + 1) * nin] = _dot(bu_scr[j].astype(BF16), cm_ref[j])
    y = y_scr[...] + dsk_ref[...] * u
    z = jax.nn.gelu(y)
    yb = (z * jax.nn.sigmoid(_dot(z.astype(BF16), wglu_ref[...]))).astype(BF16)
    if steps > 1:
        yb = _dot(perm_ref[1], yb).astype(BF16)
    yb_ref[...] = yb.reshape(yb_ref.shape)
    xr_out[...] = xr_scr[...]
    xi_out[...] = xi_scr[...]


def _s5(h, w_u, bb, cm, abr, abi, dsk, wglu, x0r, x0i, *, nb, steps):
    g, seq, d = h.shape
    rows = nb * steps
    assert seq % (rows // g) == 0 and (g == nb or steps == 1)
    blk = rows // g
    su = w_u.shape[1]
    nst = abr.shape[1]
    r = jnp.arange(rows)
    to_tm = (r[:, None] % nb) * steps + r[:, None] // nb == r[None, :]
    perm = jnp.stack([to_tm, to_tm.T]).astype(BF16)
    return pl.pallas_call(
        functools.partial(_s5_body, nb=nb, steps=steps),
        grid=(seq // blk,),
        in_specs=[
            pl.BlockSpec((g, blk, d), lambda c: (0, c, 0)),
            _const_spec(perm.shape),
            _const_spec(w_u.shape), _const_spec(bb.shape), _const_spec(cm.shape),
            _const_spec(abr.shape), _const_spec(abi.shape), _const_spec(dsk.shape),
            _const_spec(wglu.shape), _const_spec(x0r.shape), _const_spec(x0i.shape),
        ],
        out_specs=[pl.BlockSpec((g, blk, su), lambda c: (0, c, 0)),
                   pl.BlockSpec((nb, nst), lambda c: (0, 0)),
                   pl.BlockSpec((nb, nst), lambda c: (0, 0))],
        out_shape=[jax.ShapeDtypeStruct((g, seq, su), BF16),
                   jax.ShapeDtypeStruct((nb, nst), F32),
                   jax.ShapeDtypeStruct((nb, nst), F32)],
        scratch_shapes=[pltpu.VMEM((S5_BLOCKS, rows, 2 * nst // S5_BLOCKS), F32),
                        pltpu.VMEM((rows, su), F32),
                        pltpu.VMEM((nb, nst), F32), pltpu.VMEM((nb, nst), F32)],
        compiler_params=_cparams(1),
        name="s5_branch",
    )(h, perm, w_u, bb, cm, abr, abi, dsk, wglu, x0r, x0i)


def _headsum(x, e):
    xb = x.astype(BF16)
    t = MXU_TILE_V7X
    return jnp.concatenate([_dot(xb[:, c:c + t], e[c:c + t, c:c + t]) for c in range(0, RW, t)], axis=1)


def _split3(x):
    hi = x.astype(BF16)
    r1 = x - hi.astype(F32)
    mid = r1.astype(BF16)
    lo = (r1 - mid.astype(F32)).astype(BF16)
    return hi, mid, lo


def _rwkv_pre(fm, w):
    r = fm[:, 0:RW]
    k = fm[:, RW:2 * RW]
    v = fm[:, 2 * RW:3 * RW]
    o0 = 3 * RW
    nw = w['w_up'].shape[0]
    na = w['a_up'].shape[0]
    wd = fm[:, o0:o0 + nw]
    ad = fm[:, o0 + nw:o0 + nw + na]
    gd = fm[:, o0 + nw + na:]
    w_log = -jax.nn.softplus(-(w['w0'][...] + _dot(jnp.tanh(wd).astype(BF16), w['w_up'][...]))) - 0.5
    logw = -jnp.exp(w_log)
    a = jax.nn.sigmoid(w['a0'][...] + _dot(ad.astype(BF16), w['a_up'][...]))
    g = _dot(jax.nn.sigmoid(gd).astype(BF16), w['g_up'][...])
    kk = k * w['k_k'][...]
    kk = kk * lax.rsqrt(jnp.maximum(_headsum(kk * kk, w['e']), 1e-24))
    k2 = k * (1.0 + (a - 1.0) * w['k_a'][...])
    return r, logw, k2, v, kk, kk * a, g


def _rwkv_post(o, r, k2, v, g, w):
    inv_n = 1.0 / HEAD
    mu = _headsum(o, w['e']) * inv_n
    d = o - mu
    var = _headsum(d * d, w['e']) * inv_n
    on = d * lax.rsqrt(var + LNX_EPS) * w['lnx_w'][...] + w['lnx_b'][...]
    bonus = _headsum(r * k2 * w['r_k'][...], w['e']) * v
    return (on + bonus) * g


def _sigmoid(x):
    return 0.5 * jnp.tanh(0.5 * x) + 0.5


def _merge_gates(h, y_b, w):
    gates = _sigmoid(_dot(h, w['w_g'][...]))
    dm = gates.shape[1] // 2
    return gates[:, :dm], gates[:, dm:] * _dot(y_b.astype(BF16), w['w_b_up'][...])


def _merge_out(x1, y_a, gate_a, gated_b, w):
    merged = gate_a * _dot(y_a.astype(BF16), w['w_a_up'][...]) + gated_b
    return x1 + _dot(merged.astype(BF16), w['w_out'][...])


_W_NAMES = ('w_f', 'w_g', 'mu', 'w0', 'w_up', 'a0', 'a_up', 'g_up', 'k_k', 'k_a', 'r_k',
            'lnx_w', 'lnx_b', 'e', 'w_a_up', 'w_b_up', 'w_out')


def _mixer_prompt_body(*refs, tc):
    h_ref, x1_ref, yb_ref = refs[0:3]
    nwt = len(_W_NAMES)
    wrefs = dict(zip(_W_NAMES, refs[3:3 + nwt]))
    tri_ref, mask_ref = refs[3 + nwt:5 + nwt]
    x2_ref, shift_ref, wkv_ref = refs[5 + nwt:8 + nwt]
    (f_scr, st_scr, lhs_scr, rhs_scr, sfx_scr, v_scr, gt_scr, o_scr, ga_scr,
     gb_scr) = refs[8 + nwt:]
    c = pl.program_id(1)
    w = wrefs

    @pl.when(c == 0)
    def _():
        f_scr[0:8, :] = jnp.zeros((8, f_scr.shape[1]), F32)
        st_scr[...] = jnp.zeros_like(st_scr)

    @pl.when(c > 0)
    def _():
        f_scr[7:8, :] = f_scr[tc + 7:tc + 8, :]

    h = h_ref[...]
    f = _dot(h, wrefs['w_f'][...])
    f_scr[8:8 + tc, :] = f
    ga_scr[...], gb_scr[...] = _merge_gates(h, yb_ref[...], w)
    shifted = f_scr[7:7 + tc, :]
    fm = f + w['mu'][...] * (shifted - f)
    r, logw, k2, v, kk, b, g = _rwkv_pre(fm, w)

    ns = tc // CHUNK
    tri = tri_ref[...]
    for s in range(ns):
        sl = slice(s * CHUNK, (s + 1) * CHUNK)
        lw = logw[sl]
        hi, mid, lo = _split3(lw)
        lpre = _dot(tri, hi) + _dot(tri, mid) + _dot(tri, lo)
        lsuf = lpre[CHUNK - 1:CHUNK, :] - lpre
        e_pre = jnp.exp(lpre)
        e_neg = jnp.exp(-lpre)
        e_suf = jnp.exp(lsuf)
        lhs = jnp.concatenate([kk[sl] * jnp.exp(lpre - lw), r[sl] * e_pre], axis=0).astype(BF16)
        rhs = jnp.concatenate([b[sl] * e_neg, k2[sl] * e_neg], axis=0).astype(BF16)
        sfx_t = jnp.concatenate([b[sl] * e_suf, k2[sl] * e_suf], axis=0).T.astype(BF16)
        vb = v[sl].astype(BF16)
        for hd in range(HEADS):
            ls = slice(hd * HEAD, (hd + 1) * HEAD)
            lhs_scr[hd, s] = lhs[:, ls]
            rhs_scr[hd, s] = rhs[:, ls]
            sfx_scr[hd, s] = sfx_t[ls, :]
            v_scr[hd, s] = vb[:, ls]
        gt_scr[s] = jnp.broadcast_to(e_pre[CHUNK - 1:CHUNK, :], (8, RW))

    mask = mask_ref[...] > 0.5
    eye = (lax.broadcasted_iota(jnp.int32, (CHUNK, CHUNK), 0)
           == lax.broadcasted_iota(jnp.int32, (CHUNK, CHUNK), 1)).astype(F32)
    n_double = CHUNK.bit_length() - 2
    heads = range(HEADS)
    units = [(j, hd) for j in range(CHUNKS_PER_TRIP) for hd in heads]

    def chunk_step(i, carry):
        s0 = i * CHUNKS_PER_TRIP
        lhs = {u: lhs_scr[u[1], s0 + u[0]] for u in units}
        rhs = {u: rhs_scr[u[1], s0 + u[0]] for u in units}
        a_mat = {u: jnp.where(mask, _dot_nt(lhs[u], rhs[u]), 0.0) for u in units}
        stack = lambda top, bot: jnp.concatenate([top.astype(BF16), bot.astype(BF16)], axis=0)
        p = {u: -a_mat[u][:CHUNK, :CHUNK] for u in units}
        winv = {u: eye + p[u] for u in units}
        p = {u: _dot(p[u].astype(BF16), p[u].astype(BF16)) for u in units}
        for _ in range(n_double - 1):
            wp = {u: _dot(stack(winv[u], p[u]), p[u].astype(BF16)) for u in units}
            winv = {u: winv[u] + wp[u][:CHUNK] for u in units}
            p = {u: wp[u][CHUNK:] for u in units}
        wb = {u: (winv[u] + _dot(winv[u].astype(BF16), p[u].astype(BF16))).astype(BF16) for u in units}
        vv = {u: v_scr[u[1], s0 + u[0]] for u in units}
        mkv = {u: _dot(a_mat[u][:CHUNK, CHUNK:].astype(BF16), vv[u]).astype(BF16) for u in units}
        gb = {u: _dot(wb[u], lhs[u][:CHUNK]).astype(BF16) for u in units}
        uv = {u: jnp.concatenate([(-_dot(wb[u], mkv[u])).astype(BF16), vv[u]], axis=0)
              for u in units}
        sfx_t = {u: sfx_scr[u[1], s0 + u[0]] for u in units}
        n_bk = {u: a_mat[u][CHUNK:, :] for u in units}
        ng = {u: _dot(stack(n_bk[u][:, :CHUNK], sfx_t[u][:, :CHUNK]), gb[u]) for u in units}
        nuv = {u: _dot(stack(n_bk[u], sfx_t[u]), uv[u]) for u in units}
        gt8 = [gt_scr[s0 + j] for j in range(CHUNKS_PER_TRIP)]
        qphi = {u: stack(lhs[u][CHUNK:].astype(F32) - ng[u][:CHUNK],
                         eye * gt8[u[0]][0:1, u[1] * HEAD:(u[1] + 1) * HEAD] - ng[u][CHUNK:])
                for u in units}
        st = [st_scr[hd] for hd in heads]
        for j in range(CHUNKS_PER_TRIP):
            rs = pl.ds(pl.multiple_of((s0 + j) * CHUNK, CHUNK), CHUNK)
            os = [_dot(qphi[j, hd], st[hd].astype(BF16)) + nuv[j, hd] for hd in heads]
            for hd in heads:
                o_scr[rs, hd * HEAD:(hd + 1) * HEAD] = os[hd][:CHUNK]
            st = [os[hd][CHUNK:] for hd in heads]
        for hd in heads:
            st_scr[hd] = st[hd]
        return carry

    lax.fori_loop(0, ns // CHUNKS_PER_TRIP, chunk_step, 0)

    y_a = _rwkv_post(o_scr[...], r, k2, v, g, w)
    x2_ref[...] = _merge_out(x1_ref[...], y_a, ga_scr[...], gb_scr[...], w)

    @pl.when(c == pl.num_programs(1) - 1)
    def _():
        shift_ref[...] = f[tc - 1:tc, :]
        for hd in range(HEADS):
            wkv_ref[hd] = st_scr[hd].T


def _mixer_prompt(h, x1, yb, wts, tri, mask, *, tc):
    bsz, seq, d = h.shape
    assert seq % tc == 0 and tc % CHUNK == 0
    nf = wts['w_f'].shape[1]
    wlist = [wts[n] for n in _W_NAMES]
    tok = lambda width: pl.BlockSpec((None, tc, width), lambda b, c: (b, c, 0))
    return pl.pallas_call(
        functools.partial(_mixer_prompt_body, tc=tc),
        grid=(bsz, seq // tc),
        in_specs=[tok(d), tok(d), tok(RW)] + [_const_spec(a.shape) for a in wlist]
                 + [_const_spec(tri.shape), _const_spec(mask.shape)],
        out_specs=[tok(d),
                   pl.BlockSpec((None, 1, nf), lambda b, c: (b, 0, 0)),
                   pl.BlockSpec((None, HEADS, HEAD, HEAD), lambda b, c: (b, 0, 0, 0))],
        out_shape=[jax.ShapeDtypeStruct((bsz, seq, d), F32),
                   jax.ShapeDtypeStruct((bsz, 1, nf), F32),
                   jax.ShapeDtypeStruct((bsz, HEADS, HEAD, HEAD), F32)],
        scratch_shapes=[pltpu.VMEM((tc + 8, nf), F32),
                        pltpu.VMEM((HEADS, HEAD, HEAD), F32)]
                       + [pltpu.VMEM((HEADS, tc // CHUNK, 2 * CHUNK, HEAD), BF16)] * 2
                       + [pltpu.VMEM((HEADS, tc // CHUNK, HEAD, 2 * CHUNK), BF16),
                          pltpu.VMEM((HEADS, tc // CHUNK, CHUNK, HEAD), BF16),
                          pltpu.VMEM((tc // CHUNK, 8, RW), F32),
                          pltpu.VMEM((tc, RW), F32),
                          pltpu.VMEM((tc, d), F32), pltpu.VMEM((tc, d), F32)],
        compiler_params=_cparams(2),
        name="mixer_prompt",
    )(h, x1, yb, *wlist, tri, mask)


def _mixer_sample_body(*refs, sb):
    h_ref, x1_ref, yb_ref, shift_in_ref, wkv_in_ref = refs[0:5]
    nwt = len(_W_NAMES)
    wrefs = dict(zip(_W_NAMES, refs[5:5 + nwt]))
    x2_ref, shift_ref, wkv_ref = refs[5 + nwt:8 + nwt]
    (o_scr,) = refs[8 + nwt:]
    w = wrefs

    h = h_ref[...]
    f = _dot(h, wrefs['w_f'][...])
    fm = f + w['mu'][...] * (shift_in_ref[...] - f)
    shift_ref[...] = f
    r, logw, k2, v, kk, b, g = _rwkv_pre(fm, w)
    dec = jnp.exp(logw)
    row8 = lax.broadcasted_iota(jnp.int32, (8, HEAD), 0)

    idx = [(i, hd) for i in range(sb) for hd in range(HEADS)]
    cut = lambda a, i, hd: a[i:i + 1, hd * HEAD:(hd + 1) * HEAD]
    bc8 = lambda a, i, hd: jnp.broadcast_to(cut(a, i, hd), (8, HEAD))
    sa8 = {}
    for i, hd in idx:
        st = wkv_in_ref[i, hd]
        sa8[i, hd] = -_dot_nt(bc8(kk, i, hd).astype(BF16), st.astype(BF16))
    for i, hd in idx:
        xm = jnp.where(row8 == 0, bc8(v, i, hd), jnp.where(row8 == 1, sa8[i, hd], 0.0))
        ym = jnp.where(row8 == 0, bc8(k2, i, hd), jnp.where(row8 == 1, bc8(b, i, hd), 0.0))
        wkv_ref[i, hd] = (wkv_in_ref[i, hd] * cut(dec, i, hd)
                          + _dot_tn(xm, ym, precision=lax.Precision.HIGHEST))
    for i, hd in idx:
        o8 = _dot_nt(bc8(r, i, hd).astype(BF16), wkv_ref[i, hd].astype(BF16))
        o_scr[i:i + 1, hd * HEAD:(hd + 1) * HEAD] = o8[0:1, :]

    y_a = _rwkv_post(o_scr[...], r, k2, v, g, w)
    gate_a, gated_b = _merge_gates(h, yb_ref[...], w)
    x2_ref[...] = _merge_out(x1_ref[...], y_a, gate_a, gated_b, w)


def _mixer_sample(h, x1, yb, shift_in, wkv_in, wts, *, sb):
    n, d = h.shape
    assert n % sb == 0
    nf = wts['w_f'].shape[1]
    wlist = [wts[nm] for nm in _W_NAMES]
    row = lambda width: pl.BlockSpec((sb, width), lambda i: (i, 0))
    st_spec = pl.BlockSpec((sb, HEADS, HEAD, HEAD), lambda i: (i, 0, 0, 0))
    return pl.pallas_call(
        functools.partial(_mixer_sample_body, sb=sb),
        grid=(n // sb,),
        in_specs=[row(d), row(d), row(RW), row(nf), st_spec] + [_const_spec(a.shape) for a in wlist],
        out_specs=[row(d), row(nf), st_spec],
        out_shape=[jax.ShapeDtypeStruct((n, d), F32),
                   jax.ShapeDtypeStruct((n, nf), F32),
                   jax.ShapeDtypeStruct(wkv_in.shape, F32)],
        scratch_shapes=[pltpu.VMEM((sb, RW), F32)],
        compiler_params=_cparams(1),
        name="mixer_sample",
    )(h, x1, yb, shift_in, wkv_in, *wlist)


def _chunk_constants():
    t = jnp.arange(CHUNK)
    incl = (t[:, None] >= t[None, :])
    strict = (t[:, None] > t[None, :])
    tri = incl.astype(BF16)
    top = jnp.concatenate([strict, strict], axis=1)
    bot = jnp.concatenate([incl, incl], axis=1)
    mask = jnp.concatenate([top, bot], axis=0).astype(F32)
    return tri, mask


def kernel(x_prompt, x_sample, state_shift, state_wkv, state_s5_re, state_s5_im, g_ffn1, ffn1_gate, ffn1_up, ffn1_down, g_mix, w_in, mu_shift, w0, w_up, a0, a_up, g_up, k_k, k_a, r_k, lnx_w, lnx_b, A_re, A_im, log_dt, B_re, B_im, C_re, C_im, D_skip, w_glu, w_a_up, w_b_up, w_out, g_ffn2, ffn2_gate, ffn2_up, ffn2_down, g_final):
    depth = g_ffn1.shape[0]
    assert depth == 1
    bp, seq, d = x_prompt.shape
    bs = x_sample.shape[0]
    assert x_sample.shape[1] == 1
    ncols_f = mu_shift.shape[1]
    su = D_skip.shape[1]
    groups, pstate = A_re.shape[1:]
    nstate = groups * pstate
    bf = lambda a: a.astype(BF16)
    row = lambda a: a.reshape(1, -1).astype(F32)

    w_in0 = w_in[0]
    eye_h = jnp.eye(HEADS, dtype=F32)
    wts = {
        'w_f': _to_bf16(w_in0, 0, ncols_f),
        'w_g': _to_bf16(w_in0, ncols_f + su),
        'mu': row(mu_shift[0]), 'w0': row(w0[0]), 'w_up': bf(w_up[0]), 'a0': row(a0[0]),
        'a_up': bf(a_up[0]), 'g_up': bf(g_up[0]), 'k_k': row(k_k[0]), 'k_a': row(k_a[0]),
        'r_k': row(r_k[0]), 'lnx_w': row(lnx_w[0]), 'lnx_b': row(lnx_b[0]),
        'e': bf(jnp.kron(eye_h, jnp.ones((HEAD, HEAD), F32))),
        'w_a_up': bf(w_a_up[0]), 'w_b_up': bf(w_b_up[0]), 'w_out': _to_bf16(w_out[0]),
    }
    w_u = _to_bf16(w_in0, ncols_f, su)
    abr, abi, bbr, bbi = _s5_params(A_re[0], A_im[0], log_dt[0],
                                    jnp.swapaxes(B_re[0], 1, 2), jnp.swapaxes(B_im[0], 1, 2))
    bb = bf(jnp.concatenate([_block_diag(bbr, S5_BLOCKS), _block_diag(bbi, S5_BLOCKS)], axis=2))
    cm = bf(jnp.concatenate([_block_diag(jnp.swapaxes(C_re[0], 1, 2), S5_BLOCKS),
                             _block_diag(-jnp.swapaxes(C_im[0], 1, 2), S5_BLOCKS)], axis=1))
    abr = abr.reshape(1, nstate)
    abi = abi.reshape(1, nstate)
    dsk = row(D_skip[0])
    wglu = bf(w_glu[0])
    ffn1 = (row(g_ffn1[0]), _to_bf16(ffn1_gate[0]), _to_bf16(ffn1_up[0]), _to_bf16(ffn1_down[0]),
            row(g_mix[0]))
    ffn2 = (row(g_ffn2[0]), _to_bf16(ffn2_gate[0]), _to_bf16(ffn2_up[0]), _to_bf16(ffn2_down[0]),
            row(g_final))
    fc = 4 * MXU_TILE_V7X
    tri, mask = _chunk_constants()

    xp = x_prompt.reshape(bp * seq, d)
    x1p, hnp = _ffn(xp, *ffn1, final=False, tm=1024, fc=fc)
    hnp = hnp.reshape(bp, seq, d)
    zeros_state = jnp.zeros((bp, nstate), F32)
    yb_p, s5r_p, s5i_p = _s5(hnp, w_u, bb, cm, abr, abi, dsk, wglu, zeros_state, zeros_state,
                             nb=bp, steps=64)
    x2p, shift_p, wkv_p = _mixer_prompt(hnp, x1p.reshape(bp, seq, d), yb_p, wts, tri, mask, tc=512)
    y_prompt = _ffn(x2p.reshape(bp * seq, d), *ffn2, final=True, tm=1024, fc=fc).reshape(bp, seq, d)

    xs = x_sample.reshape(bs, d)
    x1s, hns = _ffn(xs, *ffn1, final=False, tm=bs, fc=fc)
    yb_s, s5r_s, s5i_s = _s5(hns[None], w_u, bb, cm, abr, abi, dsk, wglu,
                             state_s5_re[0].reshape(bs, nstate), state_s5_im[0].reshape(bs, nstate),
                             nb=bs, steps=1)
    x2s, shift_s, wkv_s = _mixer_sample(hns, x1s, yb_s[0], state_shift[0], state_wkv[0], wts, sb=16)
    y_sample = _ffn(x2s, *ffn2, final=True, tm=bs, fc=fc).reshape(bs, 1, d)

    st5 = lambda a, n: a.reshape(1, n, groups, pstate)
    return (y_prompt, y_sample,
            shift_p.reshape(1, bp, ncols_f), wkv_p[None], st5(s5r_p, bp), st5(s5i_p, bp),
            shift_s[None], wkv_s[None], st5(s5r_s, bs), st5(s5i_s, bs))
```

```python
import functools

import jax
import jax.numpy as jnp
from jax import lax
from jax.experimental import pallas as pl
from jax.experimental.pallas import tpu as pltpu

F32 = jnp.float32
BF16 = jnp.bfloat16

NORM_EPS = 1e-6
LNX_EPS = 64e-5
HEAD = 64
HEADS = 8
RW = HEAD * HEADS
S5_BLOCKS = 4
CHUNK = 64
CHUNKS_PER_TRIP = 4
VMEM_LIMIT = 56 * 1024 * 1024
MXU_TILE_V7X = 256


def _dot(a, b):
    return jnp.dot(a, b, preferred_element_type=F32)


def _dot_nt(a, b):
    return lax.dot_general(a, b, (((1,), (1,)), ((), ())), preferred_element_type=F32)


def _dot_tn(a, b, precision=None):
    return lax.dot_general(a, b, (((0,), (0,)), ((), ())), preferred_element_type=F32,
                           precision=precision)


def _rms(x, g):
    ms = jnp.mean(x * x, axis=-1, keepdims=True)
    return x * lax.rsqrt(ms + NORM_EPS) * g


def _const_spec(shape):
    nd = len(shape)
    return pl.BlockSpec(shape, lambda *_: (0,) * nd, pipeline_mode=pl.Buffered(1))


def _cparams(ngrid):
    return pltpu.CompilerParams(dimension_semantics=("arbitrary",) * ngrid,
                                vmem_limit_bytes=VMEM_LIMIT)


def _ffn_body(x_ref, g_ref, wg_ref, wu_ref, wd_ref, gn_ref, *outs, final, fc):
    x = x_ref[...]
    hn = _rms(x, g_ref[...]).astype(BF16)
    acc = None
    f = wg_ref.shape[1]
    for c0 in range(0, f, fc):
        c1 = min(c0 + fc, f)
        gate = _dot(hn, wg_ref[:, c0:c1])
        up = _dot(hn, wu_ref[:, c0:c1])
        act = (jax.nn.silu(gate) * up).astype(BF16)
        part = _dot(act, wd_ref[c0:c1, :])
        acc = part if acc is None else acc + part
    x1 = x + 0.5 * acc
    if final:
        outs[0][...] = _rms(x1, gn_ref[...])
    else:
        outs[0][...] = x1
        outs[1][...] = _rms(x1, gn_ref[...]).astype(BF16)


def _ffn(x, g, wg, wu, wd, gn, *, final, tm, fc):
    m, d = x.shape
    f = wg.shape[1]
    assert m % tm == 0 and fc % MXU_TILE_V7X == 0
    tile = pl.BlockSpec((tm, d), lambda i: (i, 0))
    if final:
        out_shape = jax.ShapeDtypeStruct((m, d), F32)
        out_specs = tile
    else:
        out_shape = [jax.ShapeDtypeStruct((m, d), F32), jax.ShapeDtypeStruct((m, d), BF16)]
        out_specs = [tile, tile]
    return pl.pallas_call(
        functools.partial(_ffn_body, final=final, fc=fc),
        grid=(m // tm,),
        in_specs=[tile, _const_spec(g.shape), _const_spec(wg.shape), _const_spec(wu.shape),
                  _const_spec(wd.shape), _const_spec(gn.shape)],
        out_specs=out_specs,
        out_shape=out_shape,
        compiler_params=_cparams(1),
        name="ffn_final" if final else "ffn_in",
    )(x, g, wg, wu, wd, gn)


def _s5_param_body(are_ref, aim_ref, ldt_ref, bre_ref, bim_ref, abr_ref, abi_ref, bbr_ref, bbi_ref):
    lam_r = are_ref[...]
    lam_i = aim_ref[...]
    dt = jnp.exp(ldt_ref[...])
    mag = jnp.exp(lam_r * dt)
    ab_r = mag * jnp.cos(lam_i * dt)
    ab_i = mag * jnp.sin(lam_i * dt)
    den = lam_r * lam_r + lam_i * lam_i
    q_r = ((ab_r - 1.0) * lam_r + ab_i * lam_i) / den
    q_i = (ab_i * lam_r - (ab_r - 1.0) * lam_i) / den
    abr_ref[...] = ab_r
    abi_ref[...] = ab_i
    bre = bre_ref[...]
    bim = bim_ref[...]
    bbr_ref[...] = q_r[:, None, :] * bre - q_i[:, None, :] * bim
    bbi_ref[...] = q_r[:, None, :] * bim + q_i[:, None, :] * bre


def _s5_params(a_re, a_im, log_dt, b_re_t, b_im_t):
    g, p = a_re.shape
    hg = b_re_t.shape[1]
    return pl.pallas_call(
        _s5_param_body,
        out_shape=[jax.ShapeDtypeStruct((g, p), F32), jax.ShapeDtypeStruct((g, p), F32),
                   jax.ShapeDtypeStruct((g, hg, p), F32), jax.ShapeDtypeStruct((g, hg, p), F32)],
        name="s5_params",
    )(a_re, a_im, log_dt.reshape(g, 1), b_re_t, b_im_t)


def _block_diag(w, nblk):
    g, a, b = w.shape
    gb = g // nblk
    w4 = w.reshape(nblk, gb, a, b)
    eye = jnp.eye(gb, dtype=w.dtype)
    return jnp.einsum('jgab,gk->jgakb', w4, eye).reshape(nblk, gb * a, gb * b)


def _s5_body(h_ref, perm_ref, wu_ref, bb_ref, cm_ref, abr_ref, abi_ref, dsk_ref, wglu_ref, x0r_ref,
             x0i_ref, yb_ref, xr_out, xi_out, bu_scr, y_scr, xr_scr, xi_scr, *, nb, steps):
    c = pl.program_id(0)
    rows = nb * steps

    @pl.when(c == 0)
    def _():
        xr_scr[...] = x0r_ref[...]
        xi_scr[...] = x0i_ref[...]

    h = h_ref[...].reshape(rows, h_ref.shape[-1])
    if steps > 1:
        h = _dot(perm_ref[0], h).astype(BF16)
    u = _dot(h, wu_ref[...])
    ub = u.astype(BF16)
    nin = ub.shape[1] // S5_BLOCKS
    ns = abr_ref.shape[1] // S5_BLOCKS
    for j in range(S5_BLOCKS):
        bu_scr[j] = _dot(ub[:, j * nin:(j + 1) * nin], bb_ref[j])
    for j in range(S5_BLOCKS):
        ar = jnp.broadcast_to(abr_ref[:, j * ns:(j + 1) * ns], (nb, ns))
        ai = jnp.broadcast_to(abi_ref[:, j * ns:(j + 1) * ns], (nb, ns))
        xr = xr_scr[:, j * ns:(j + 1) * ns]
        xi = xi_scr[:, j * ns:(j + 1) * ns]
        for t in range(steps):
            tr = slice(t * nb, (t + 1) * nb)
            xr, xi = (ar * xr - ai * xi + bu_scr[j, tr, 0:ns],
                      ar * xi + ai * xr + bu_scr[j, tr, ns:2 * ns])
            bu_scr[j, tr, 0:ns] = xr
            bu_scr[j, tr, ns:2 * ns] = xi
        xr_scr[:, j * ns:(j + 1) * ns] = xr
        xi_scr[:, j * ns:(j + 1) * ns] = xi
        y_scr[:, j * nin:(j + 1) * nin] = _dot(bu_scr[j].astype(BF16), cm_ref[j])
    y = y_scr[...] + dsk_ref[...] * u
    z = jax.nn.gelu(y)
    yb = (z * jax.nn.sigmoid(_dot(z.astype(BF16), wglu_ref[...]))).astype(BF16)
    if steps > 1:
        yb = _dot(perm_ref[1], yb).astype(BF16)
    yb_ref[...] = yb.reshape(yb_ref.shape)
    xr_out[...] = xr_scr[...]
    xi_out[...] = xi_scr[...]


def _s5(h, w_u, bb, cm, abr, abi, dsk, wglu, x0r, x0i, *, nb, steps):
    g, seq, d = h.shape
    rows = nb * steps
    assert seq % (rows // g) == 0 and (g == nb or steps == 1)
    blk = rows // g
    su = w_u.shape[1]
    nst = abr.shape[1]
    r = jnp.arange(rows)
    to_tm = (r[:, None] % nb) * steps + r[:, None] // nb == r[None, :]
    perm = jnp.stack([to_tm, to_tm.T]).astype(BF16)
    return pl.pallas_call(
        functools.partial(_s5_body, nb=nb, steps=steps),
        grid=(seq // blk,),
        in_specs=[
            pl.BlockSpec((g, blk, d), lambda c: (0, c, 0)),
            _const_spec(perm.shape),
            _const_spec(w_u.shape), _const_spec(bb.shape), _const_spec(cm.shape),
            _const_spec(abr.shape), _const_spec(abi.shape), _const_spec(dsk.shape),
            _const_spec(wglu.shape), _const_spec(x0r.shape), _const_spec(x0i.shape),
        ],
        out_specs=[pl.BlockSpec((g, blk, su), lambda c: (0, c, 0)),
                   pl.BlockSpec((nb, nst), lambda c: (0, 0)),
                   pl.BlockSpec((nb, nst), lambda c: (0, 0))],
        out_shape=[jax.ShapeDtypeStruct((g, seq, su), BF16),
                   jax.ShapeDtypeStruct((nb, nst), F32),
                   jax.ShapeDtypeStruct((nb, nst), F32)],
        scratch_shapes=[pltpu.VMEM((S5_BLOCKS, rows, 2 * nst // S5_BLOCKS), F32),
                        pltpu.VMEM((rows, su), F32),
                        pltpu.VMEM((nb, nst), F32), pltpu.VMEM((nb, nst), F32)],
        compiler_params=_cparams(1),
        name="s5_branch",
    )(h, perm, w_u, bb, cm, abr, abi, dsk, wglu, x0r, x0i)


def _headsum(x, e):
    xb = x.astype(BF16)
    t = MXU_TILE_V7X
    return jnp.concatenate([_dot(xb[:, c:c + t], e[c:c + t, c:c + t]) for c in range(0, RW, t)], axis=1)


def _split3(x):
    hi = x.astype(BF16)
    r1 = x - hi.astype(F32)
    mid = r1.astype(BF16)
    lo = (r1 - mid.astype(F32)).astype(BF16)
    return hi, mid, lo


def _rwkv_pre(fm, w):
    r = fm[:, 0:RW]
    k = fm[:, RW:2 * RW]
    v = fm[:, 2 * RW:3 * RW]
    o0 = 3 * RW
    nw = w['w_up'].shape[0]
    na = w['a_up'].shape[0]
    wd = fm[:, o0:o0 + nw]
    ad = fm[:, o0 + nw:o0 + nw + na]
    gd = fm[:, o0 + nw + na:]
    w_log = -jax.nn.softplus(-(w['w0'][...] + _dot(jnp.tanh(wd).astype(BF16), w['w_up'][...]))) - 0.5
    logw = -jnp.exp(w_log)
    a = jax.nn.sigmoid(w['a0'][...] + _dot(ad.astype(BF16), w['a_up'][...]))
    g = _dot(jax.nn.sigmoid(gd).astype(BF16), w['g_up'][...])
    kk = k * w['k_k'][...]
    kk = kk * lax.rsqrt(jnp.maximum(_headsum(kk * kk, w['e']), 1e-24))
    k2 = k * (1.0 + (a - 1.0) * w['k_a'][...])
    return r, logw, k2, v, kk, kk * a, g


def _rwkv_post(o, r, k2, v, g, w):
    inv_n = 1.0 / HEAD
    mu = _headsum(o, w['e']) * inv_n
    d = o - mu
    var = _headsum(d * d, w['e']) * inv_n
    on = d * lax.rsqrt(var + LNX_EPS) * w['lnx_w'][...] + w['lnx_b'][...]
    bonus = _headsum(r * k2 * w['r_k'][...], w['e']) * v
    return (on + bonus) * g


def _sigmoid(x):
    return 0.5 * jnp.tanh(0.5 * x) + 0.5


def _merge_gates(h, y_b, w, c0=0, c1=None):
    dm = w['w_g'].shape[1] // 2
    c1 = dm if c1 is None else c1
    gate_a = _sigmoid(_dot(h, w['w_g'][:, c0:c1]))
    gate_b = _sigmoid(_dot(h, w['w_g'][:, dm + c0:dm + c1]))
    return gate_a, gate_b * _dot(y_b.astype(BF16), w['w_b_up'][:, c0:c1])


def _merge_out(x1, y_a, gate_a, gated_b, w):
    merged = gate_a * _dot(y_a.astype(BF16), w['w_a_up'][...]) + gated_b
    return x1 + _dot(merged.astype(BF16), w['w_out'][...])


_W_NAMES = ('w_f', 'w_g', 'mu', 'w0', 'w_up', 'a0', 'a_up', 'g_up', 'k_k', 'k_a', 'r_k',
            'lnx_w', 'lnx_b', 'e', 'w_a_up', 'w_b_up', 'w_out')


def _mixer_prompt_body(*refs, tc):
    h_ref, x1_ref, yb_ref = refs[0:3]
    nwt = len(_W_NAMES)
    wrefs = dict(zip(_W_NAMES, refs[3:3 + nwt]))
    tri_ref, mask_ref = refs[3 + nwt:5 + nwt]
    x2_ref, shift_ref, wkv_ref = refs[5 + nwt:8 + nwt]
    (f_scr, st_scr, lhs_scr, rhs_scr, sfx_scr, v_scr, gt_scr, o_scr, ga_scr,
     gb_scr) = refs[8 + nwt:]
    c = pl.program_id(1)
    w = wrefs

    @pl.when(c == 0)
    def _():
        f_scr[0:8, :] = jnp.zeros((8, f_scr.shape[1]), F32)
        st_scr[...] = jnp.zeros_like(st_scr)

    @pl.when(c > 0)
    def _():
        f_scr[7:8, :] = f_scr[tc + 7:tc + 8, :]

    h = h_ref[...]
    f = _dot(h, wrefs['w_f'][...])
    f_scr[8:8 + tc, :] = f
    shifted = f_scr[7:7 + tc, :]
    fm = f + w['mu'][...] * (shifted - f)
    r, logw, k2, v, kk, b, g = _rwkv_pre(fm, w)

    ns = tc // CHUNK
    dm = x1_ref.shape[1]
    yb = yb_ref[...]
    tri = tri_ref[...]
    for s in range(ns):
        sl = slice(s * CHUNK, (s + 1) * CHUNK)
        lw = logw[sl]
        hi, mid, lo = _split3(lw)
        lpre = _dot(tri, hi) + _dot(tri, mid) + _dot(tri, lo)
        lsuf = lpre[CHUNK - 1:CHUNK, :] - lpre
        e_pre = jnp.exp(lpre)
        e_neg = jnp.exp(-lpre)
        e_suf = jnp.exp(lsuf)
        lhs = jnp.concatenate([kk[sl] * jnp.exp(lpre - lw), r[sl] * e_pre], axis=0).astype(BF16)
        rhs = jnp.concatenate([b[sl] * e_neg, k2[sl] * e_neg], axis=0).astype(BF16)
        sfx_t = jnp.concatenate([b[sl] * e_suf, k2[sl] * e_suf], axis=0).T.astype(BF16)
        vb = v[sl].astype(BF16)
        for hd in range(HEADS):
            ls = slice(hd * HEAD, (hd + 1) * HEAD)
            lhs_scr[hd, s] = lhs[:, ls]
            rhs_scr[hd, s] = rhs[:, ls]
            sfx_scr[hd, s] = sfx_t[ls, :]
            v_scr[hd, s] = vb[:, ls]
        gt_scr[s] = jnp.broadcast_to(e_pre[CHUNK - 1:CHUNK, :], (8, RW))
        every = ns * MXU_TILE_V7X // dm
        if s % every == 0:
            c0 = s // every * MXU_TILE_V7X
            ga_scr[:, c0:c0 + MXU_TILE_V7X], gb_scr[:, c0:c0 + MXU_TILE_V7X] = _merge_gates(
                h, yb, w, c0, c0 + MXU_TILE_V7X)

    mask = mask_ref[...] > 0.5
    eye = (lax.broadcasted_iota(jnp.int32, (CHUNK, CHUNK), 0)
           == lax.broadcasted_iota(jnp.int32, (CHUNK, CHUNK), 1)).astype(F32)
    n_double = CHUNK.bit_length() - 2
    heads = range(HEADS)
    units = [(j, hd) for j in range(CHUNKS_PER_TRIP) for hd in heads]

    def chunk_step(i, carry):
        s0 = i * CHUNKS_PER_TRIP
        lhs = {u: lhs_scr[u[1], s0 + u[0]] for u in units}
        rhs = {u: rhs_scr[u[1], s0 + u[0]] for u in units}
        a_mat = {u: jnp.where(mask, _dot_nt(lhs[u], rhs[u]), 0.0) for u in units}
        stack = lambda top, bot: jnp.concatenate([top.astype(BF16), bot.astype(BF16)], axis=0)
        p = {u: -a_mat[u][:CHUNK, :CHUNK] for u in units}
        winv = {u: eye + p[u] for u in units}
        p = {u: _dot(p[u].astype(BF16), p[u].astype(BF16)) for u in units}
        for _ in range(n_double - 1):
            wp = {u: _dot(stack(winv[u], p[u]), p[u].astype(BF16)) for u in units}
            winv = {u: winv[u] + wp[u][:CHUNK] for u in units}
            p = {u: wp[u][CHUNK:] for u in units}
        wb = {u: (winv[u] + _dot(winv[u].astype(BF16), p[u].astype(BF16))).astype(BF16) for u in units}
        vv = {u: v_scr[u[1], s0 + u[0]] for u in units}
        mkv = {u: _dot(a_mat[u][:CHUNK, CHUNK:].astype(BF16), vv[u]).astype(BF16) for u in units}
        gb = {u: _dot(wb[u], lhs[u][:CHUNK]).astype(BF16) for u in units}
        uv = {u: jnp.concatenate([(-_dot(wb[u], mkv[u])).astype(BF16), vv[u]], axis=0)
              for u in units}
        sfx_t = {u: sfx_scr[u[1], s0 + u[0]] for u in units}
        n_bk = {u: a_mat[u][CHUNK:, :] for u in units}
        ng = {u: _dot(stack(n_bk[u][:, :CHUNK], sfx_t[u][:, :CHUNK]), gb[u]) for u in units}
        nuv = {u: _dot(stack(n_bk[u], sfx_t[u]), uv[u]) for u in units}
        gt8 = [gt_scr[s0 + j] for j in range(CHUNKS_PER_TRIP)]
        qphi = {u: stack(lhs[u][CHUNK:].astype(F32) - ng[u][:CHUNK],
                         eye * gt8[u[0]][0:1, u[1] * HEAD:(u[1] + 1) * HEAD] - ng[u][CHUNK:])
                for u in units}
        st = [st_scr[hd] for hd in heads]
        for j in range(CHUNKS_PER_TRIP):
            rs = pl.ds(pl.multiple_of((s0 + j) * CHUNK, CHUNK), CHUNK)
            os = [_dot(qphi[j, hd], st[hd].astype(BF16)) + nuv[j, hd] for hd in heads]
            for hd in heads:
                o_scr[rs, hd * HEAD:(hd + 1) * HEAD] = os[hd][:CHUNK]
            st = [os[hd][CHUNK:] for hd in heads]
        for hd in heads:
            st_scr[hd] = st[hd]
        return carry

    lax.fori_loop(0, ns // CHUNKS_PER_TRIP, chunk_step, 0)

    y_a = _rwkv_post(o_scr[...], r, k2, v, g, w)
    x2_ref[...] = _merge_out(x1_ref[...], y_a, ga_scr[...], gb_scr[...], w)

    @pl.when(c == pl.num_programs(1) - 1)
    def _():
        shift_ref[...] = f[tc - 1:tc, :]
        for hd in range(HEADS):
            wkv_ref[hd] = st_scr[hd].T


def _mixer_prompt(h, x1, yb, wts, tri, mask, *, tc):
    bsz, seq, d = h.shape
    assert seq % tc == 0 and tc % CHUNK == 0
    nf = wts['w_f'].shape[1]
    wlist = [wts[n] for n in _W_NAMES]
    tok = lambda width: pl.BlockSpec((None, tc, width), lambda b, c: (b, c, 0))
    return pl.pallas_call(
        functools.partial(_mixer_prompt_body, tc=tc),
        grid=(bsz, seq // tc),
        in_specs=[tok(d), tok(d), tok(RW)] + [_const_spec(a.shape) for a in wlist]
                 + [_const_spec(tri.shape), _const_spec(mask.shape)],
        out_specs=[tok(d),
                   pl.BlockSpec((None, 1, nf), lambda b, c: (b, 0, 0)),
                   pl.BlockSpec((None, HEADS, HEAD, HEAD), lambda b, c: (b, 0, 0, 0))],
        out_shape=[jax.ShapeDtypeStruct((bsz, seq, d), F32),
                   jax.ShapeDtypeStruct((bsz, 1, nf), F32),
                   jax.ShapeDtypeStruct((bsz, HEADS, HEAD, HEAD), F32)],
        scratch_shapes=[pltpu.VMEM((tc + 8, nf), F32),
                        pltpu.VMEM((HEADS, HEAD, HEAD), F32)]
                       + [pltpu.VMEM((HEADS, tc // CHUNK, 2 * CHUNK, HEAD), BF16)] * 2
                       + [pltpu.VMEM((HEADS, tc // CHUNK, HEAD, 2 * CHUNK), BF16),
                          pltpu.VMEM((HEADS, tc // CHUNK, CHUNK, HEAD), BF16),
                          pltpu.VMEM((tc // CHUNK, 8, RW), F32),
                          pltpu.VMEM((tc, RW), F32),
                          pltpu.VMEM((tc, d), F32), pltpu.VMEM((tc, d), F32)],
        compiler_params=_cparams(2),
        name="mixer_prompt",
    )(h, x1, yb, *wlist, tri, mask)


def _mixer_sample_body(*refs, sb):
    h_ref, x1_ref, yb_ref, shift_in_ref, wkv_in_ref = refs[0:5]
    nwt = len(_W_NAMES)
    wrefs = dict(zip(_W_NAMES, refs[5:5 + nwt]))
    x2_ref, shift_ref, wkv_ref = refs[5 + nwt:8 + nwt]
    (o_scr,) = refs[8 + nwt:]
    w = wrefs

    h = h_ref[...]
    f = _dot(h, wrefs['w_f'][...])
    fm = f + w['mu'][...] * (shift_in_ref[...] - f)
    shift_ref[...] = f
    r, logw, k2, v, kk, b, g = _rwkv_pre(fm, w)
    dec = jnp.exp(logw)
    row8 = lax.broadcasted_iota(jnp.int32, (8, HEAD), 0)

    idx = [(i, hd) for i in range(sb) for hd in range(HEADS)]
    cut = lambda a, i, hd: a[i:i + 1, hd * HEAD:(hd + 1) * HEAD]
    bc8 = lambda a, i, hd: jnp.broadcast_to(cut(a, i, hd), (8, HEAD))
    sa8 = {}
    for i, hd in idx:
        st = wkv_in_ref[i, hd]
        sa8[i, hd] = -_dot_nt(bc8(kk, i, hd).astype(BF16), st.astype(BF16))
    for i, hd in idx:
        xm = jnp.where(row8 == 0, bc8(v, i, hd), jnp.where(row8 == 1, sa8[i, hd], 0.0))
        ym = jnp.where(row8 == 0, bc8(k2, i, hd), jnp.where(row8 == 1, bc8(b, i, hd), 0.0))
        wkv_ref[i, hd] = (wkv_in_ref[i, hd] * cut(dec, i, hd)
                          + _dot_tn(xm, ym, precision=lax.Precision.HIGHEST))
    for i, hd in idx:
        o8 = _dot_nt(bc8(r, i, hd).astype(BF16), wkv_ref[i, hd].astype(BF16))
        o_scr[i:i + 1, hd * HEAD:(hd + 1) * HEAD] = o8[0:1, :]

    y_a = _rwkv_post(o_scr[...], r, k2, v, g, w)
    gate_a, gated_b = _merge_gates(h, yb_ref[...], w)
    x2_ref[...] = _merge_out(x1_ref[...], y_a, gate_a, gated_b, w)


def _mixer_sample(h, x1, yb, shift_in, wkv_in, wts, *, sb):
    n, d = h.shape
    assert n % sb == 0
    nf = wts['w_f'].shape[1]
    wlist = [wts[nm] for nm in _W_NAMES]
    row = lambda width: pl.BlockSpec((sb, width), lambda i: (i, 0))
    st_spec = pl.BlockSpec((sb, HEADS, HEAD, HEAD), lambda i: (i, 0, 0, 0))
    return pl.pallas_call(
        functools.partial(_mixer_sample_body, sb=sb),
        grid=(n // sb,),
        in_specs=[row(d), row(d), row(RW), row(nf), st_spec] + [_const_spec(a.shape) for a in wlist],
        out_specs=[row(d), row(nf), st_spec],
        out_shape=[jax.ShapeDtypeStruct((n, d), F32),
                   jax.ShapeDtypeStruct((n, nf), F32),
                   jax.ShapeDtypeStruct(wkv_in.shape, F32)],
        scratch_shapes=[pltpu.VMEM((sb, RW), F32)],
        compiler_params=_cparams(1),
        name="mixer_sample",
    )(h, x1, yb, shift_in, wkv_in, *wlist)


def _chunk_constants():
    t = jnp.arange(CHUNK)
    incl = (t[:, None] >= t[None, :])
    strict = (t[:, None] > t[None, :])
    tri = incl.astype(BF16)
    top = jnp.concatenate([strict, strict], axis=1)
    bot = jnp.concatenate([incl, incl], axis=1)
    mask = jnp.concatenate([top, bot], axis=0).astype(F32)
    return tri, mask


def kernel(x_prompt, x_sample, state_shift, state_wkv, state_s5_re, state_s5_im, g_ffn1, ffn1_gate, ffn1_up, ffn1_down, g_mix, w_in, mu_shift, w0, w_up, a0, a_up, g_up, k_k, k_a, r_k, lnx_w, lnx_b, A_re, A_im, log_dt, B_re, B_im, C_re, C_im, D_skip, w_glu, w_a_up, w_b_up, w_out, g_ffn2, ffn2_gate, ffn2_up, ffn2_down, g_final):
    depth = g_ffn1.shape[0]
    assert depth == 1
    bp, seq, d = x_prompt.shape
    bs = x_sample.shape[0]
    assert x_sample.shape[1] == 1
    ncols_f = mu_shift.shape[1]
    su = D_skip.shape[1]
    groups, pstate = A_re.shape[1:]
    nstate = groups * pstate
    bf = lambda a: a.astype(BF16)
    row = lambda a: a.reshape(1, -1).astype(F32)

    w_in0 = w_in[0]
    eye_h = jnp.eye(HEADS, dtype=F32)
    wts = {
        'w_f': bf(w_in0[:, :ncols_f]),
        'w_g': bf(w_in0[:, ncols_f + su:]),
        'mu': row(mu_shift[0]), 'w0': row(w0[0]), 'w_up': bf(w_up[0]), 'a0': row(a0[0]),
        'a_up': bf(a_up[0]), 'g_up': bf(g_up[0]), 'k_k': row(k_k[0]), 'k_a': row(k_a[0]),
        'r_k': row(r_k[0]), 'lnx_w': row(lnx_w[0]), 'lnx_b': row(lnx_b[0]),
        'e': bf(jnp.kron(eye_h, jnp.ones((HEAD, HEAD), F32))),
        'w_a_up': bf(w_a_up[0]), 'w_b_up': bf(w_b_up[0]), 'w_out': bf(w_out[0]),
    }
    w_u = bf(w_in0[:, ncols_f:ncols_f + su])
    abr, abi, bbr, bbi = _s5_params(A_re[0], A_im[0], log_dt[0],
                                    jnp.swapaxes(B_re[0], 1, 2), jnp.swapaxes(B_im[0], 1, 2))
    bb = bf(jnp.concatenate([_block_diag(bbr, S5_BLOCKS), _block_diag(bbi, S5_BLOCKS)], axis=2))
    cm = bf(jnp.concatenate([_block_diag(jnp.swapaxes(C_re[0], 1, 2), S5_BLOCKS),
                             _block_diag(-jnp.swapaxes(C_im[0], 1, 2), S5_BLOCKS)], axis=1))
    abr = abr.reshape(1, nstate)
    abi = abi.reshape(1, nstate)
    dsk = row(D_skip[0])
    wglu = bf(w_glu[0])
    ffn1 = (row(g_ffn1[0]), bf(ffn1_gate[0]), bf(ffn1_up[0]), bf(ffn1_down[0]), row(g_mix[0]))
    ffn2 = (row(g_ffn2[0]), bf(ffn2_gate[0]), bf(ffn2_up[0]), bf(ffn2_down[0]), row(g_final))
    fc = 4 * MXU_TILE_V7X
    tri, mask = _chunk_constants()

    xp = x_prompt.reshape(bp * seq, d)
    x1p, hnp = _ffn(xp, *ffn1, final=False, tm=1024, fc=fc)
    hnp = hnp.reshape(bp, seq, d)
    zeros_state = jnp.zeros((bp, nstate), F32)
    yb_p, s5r_p, s5i_p = _s5(hnp, w_u, bb, cm, abr, abi, dsk, wglu, zeros_state, zeros_state,
                             nb=bp, steps=64)
    x2p, shift_p, wkv_p = _mixer_prompt(hnp, x1p.reshape(bp, seq, d), yb_p, wts, tri, mask, tc=512)
    y_prompt = _ffn(x2p.reshape(bp * seq, d), *ffn2, final=True, tm=1024, fc=fc).reshape(bp, seq, d)

    xs = x_sample.reshape(bs, d)
    x1s, hns = _ffn(xs, *ffn1, final=False, tm=bs, fc=fc)
    yb_s, s5r_s, s5i_s = _s5(hns[None], w_u, bb, cm, abr, abi, dsk, wglu,
                             state_s5_re[0].reshape(bs, nstate), state_s5_im[0].reshape(bs, nstate),
                             nb=bs, steps=1)
    x2s, shift_s, wkv_s = _mixer_sample(hns, x1s, yb_s[0], state_shift[0], state_wkv[0], wts, sb=16)
    y_sample = _ffn(x2s, *ffn2, final=True, tm=bs, fc=fc).reshape(bs, 1, d)

    st5 = lambda a, n: a.reshape(1, n, groups, pstate)
    return (y_prompt, y_sample,
            shift_p.reshape(1, bp, ncols_f), wkv_p[None], st5(s5r_p, bp), st5(s5i_p, bp),
            shift_s[None], wkv_s[None], st5(s5r_s, bs), st5(s5i_s, bs))
```

```python
import functools

import jax
import jax.numpy as jnp
from jax import lax
from jax.experimental import pallas as pl
from jax.experimental.pallas import tpu as pltpu

F32 = jnp.float32
BF16 = jnp.bfloat16

NORM_EPS = 1e-6
LNX_EPS = 64e-5
HEAD = 64
HEADS = 8
RW = HEAD * HEADS
S5_BLOCKS = 4
CHUNK = 64
CHUNKS_PER_TRIP = 4
VMEM_LIMIT = 56 * 1024 * 1024
MXU_TILE_V7X = 256


def _dot(a, b):
    return jnp.dot(a, b, preferred_element_type=F32)


def _dot_nt(a, b):
    return lax.dot_general(a, b, (((1,), (1,)), ((), ())), preferred_element_type=F32)


def _rms(x, g):
    ms = jnp.mean(x * x, axis=-1, keepdims=True)
    return x * lax.rsqrt(ms + NORM_EPS) * g


def _const_spec(shape):
    nd = len(shape)
    return pl.BlockSpec(shape, lambda *_: (0,) * nd, pipeline_mode=pl.Buffered(1))


def _cparams(ngrid):
    return pltpu.CompilerParams(dimension_semantics=("arbitrary",) * ngrid,
                                vmem_limit_bytes=VMEM_LIMIT)


def _ffn_body(x_ref, g_ref, wg_ref, wu_ref, wd_ref, gn_ref, *outs, final, fc):
    x = x_ref[...]
    hn = _rms(x, g_ref[...]).astype(BF16)
    acc = None
    f = wg_ref.shape[1]
    for c0 in range(0, f, fc):
        c1 = min(c0 + fc, f)
        gate = _dot(hn, wg_ref[:, c0:c1])
        up = _dot(hn, wu_ref[:, c0:c1])
        act = (jax.nn.silu(gate) * up).astype(BF16)
        part = _dot(act, wd_ref[c0:c1, :])
        acc = part if acc is None else acc + part
    x1 = x + 0.5 * acc
    if final:
        outs[0][...] = _rms(x1, gn_ref[...])
    else:
        outs[0][...] = x1
        outs[1][...] = _rms(x1, gn_ref[...]).astype(BF16)


def _ffn(x, g, wg, wu, wd, gn, *, final, tm, fc):
    m, d = x.shape
    assert m % tm == 0 and fc % MXU_TILE_V7X == 0
    tile = pl.BlockSpec((tm, d), lambda i: (i, 0))
    if final:
        out_shape = jax.ShapeDtypeStruct((m, d), F32)
        out_specs = tile
    else:
        out_shape = [jax.ShapeDtypeStruct((m, d), F32), jax.ShapeDtypeStruct((m, d), BF16)]
        out_specs = [tile, tile]
    return pl.pallas_call(
        functools.partial(_ffn_body, final=final, fc=fc),
        grid=(m // tm,),
        in_specs=[tile, _const_spec(g.shape), _const_spec(wg.shape), _const_spec(wu.shape),
                  _const_spec(wd.shape), _const_spec(gn.shape)],
        out_specs=out_specs,
        out_shape=out_shape,
        compiler_params=_cparams(1),
        name="ffn_final" if final else "ffn_in",
    )(x, g, wg, wu, wd, gn)


def _s5_param_body(are_ref, aim_ref, ldt_ref, bre_ref, bim_ref, abr_ref, abi_ref, bbr_ref, bbi_ref):
    lam_r = are_ref[...]
    lam_i = aim_ref[...]
    dt = jnp.exp(ldt_ref[...])
    mag = jnp.exp(lam_r * dt)
    ab_r = mag * jnp.cos(lam_i * dt)
    ab_i = mag * jnp.sin(lam_i * dt)
    den = lam_r * lam_r + lam_i * lam_i
    q_r = ((ab_r - 1.0) * lam_r + ab_i * lam_i) / den
    q_i = (ab_i * lam_r - (ab_r - 1.0) * lam_i) / den
    abr_ref[...] = ab_r
    abi_ref[...] = ab_i
    bre = bre_ref[...]
    bim = bim_ref[...]
    bbr_ref[...] = q_r[:, None, :] * bre - q_i[:, None, :] * bim
    bbi_ref[...] = q_r[:, None, :] * bim + q_i[:, None, :] * bre


def _s5_params(a_re, a_im, log_dt, b_re_t, b_im_t):
    g, p = a_re.shape
    hg = b_re_t.shape[1]
    return pl.pallas_call(
        _s5_param_body,
        out_shape=[jax.ShapeDtypeStruct((g, p), F32), jax.ShapeDtypeStruct((g, p), F32),
                   jax.ShapeDtypeStruct((g, hg, p), F32), jax.ShapeDtypeStruct((g, hg, p), F32)],
        name="s5_params",
    )(a_re, a_im, log_dt.reshape(g, 1), b_re_t, b_im_t)


def _block_diag(w, nblk):
    g, a, b = w.shape
    gb = g // nblk
    w4 = w.reshape(nblk, gb, a, b)
    eye = jnp.eye(gb, dtype=w.dtype)
    return jnp.einsum('jgab,gk->jgakb', w4, eye).reshape(nblk, gb * a, gb * b)


def _s5_body(h_ref, perm_ref, wu_ref, bb_ref, cm_ref, abr_ref, abi_ref, dsk_ref, wglu_ref, x0r_ref,
             x0i_ref, yb_ref, xr_out, xi_out, bu_scr, y_scr, xr_scr, xi_scr, *, nb, steps):
    c = pl.program_id(0)
    rows = nb * steps

    @pl.when(c == 0)
    def _():
        xr_scr[...] = x0r_ref[...]
        xi_scr[...] = x0i_ref[...]

    h = h_ref[...].reshape(rows, h_ref.shape[-1])
    if steps > 1:
        h = _dot(perm_ref[0], h).astype(BF16)
    u = _dot(h, wu_ref[...])
    ub = u.astype(BF16)
    nin = ub.shape[1] // S5_BLOCKS
    ns = abr_ref.shape[1] // S5_BLOCKS
    for j in range(S5_BLOCKS):
        bu_scr[j] = _dot(ub[:, j * nin:(j + 1) * nin], bb_ref[j])
    for j in range(S5_BLOCKS):
        ar = jnp.broadcast_to(abr_ref[:, j * ns:(j + 1) * ns], (nb, ns))
        ai = jnp.broadcast_to(abi_ref[:, j * ns:(j + 1) * ns], (nb, ns))
        xr = xr_scr[:, j * ns:(j + 1) * ns]
        xi = xi_scr[:, j * ns:(j + 1) * ns]
        for t in range(steps):
            tr = slice(t * nb, (t + 1) * nb)
            xr, xi = (ar * xr - ai * xi + bu_scr[j, tr, 0:ns],
                      ar * xi + ai * xr + bu_scr[j, tr, ns:2 * ns])
            bu_scr[j, tr, 0:ns] = xr
            bu_scr[j, tr, ns:2 * ns] = xi
        xr_scr[:, j * ns:(j + 1) * ns] = xr
        xi_scr[:, j * ns:(j + 1) * ns] = xi
        y_scr[:, j * nin:(j + 1) * nin] = _dot(bu_scr[j].astype(BF16), cm_ref[j])
    y = y_scr[...] + dsk_ref[...] * u
    z = jax.nn.gelu(y)
    yb = (z * jax.nn.sigmoid(_dot(z.astype(BF16), wglu_ref[...]))).astype(BF16)
    if steps > 1:
        yb = _dot(perm_ref[1], yb).astype(BF16)
    yb_ref[...] = yb.reshape(yb_ref.shape)
    xr_out[...] = xr_scr[...]
    xi_out[...] = xi_scr[...]


def _s5(h, w_u, bb, cm, abr, abi, dsk, wglu, x0r, x0i, *, nb, steps):
    g, seq, d = h.shape
    rows = nb * steps
    assert seq % (rows // g) == 0 and (g == nb or steps == 1)
    blk = rows // g
    su = w_u.shape[1]
    nst = abr.shape[1]
    r = jnp.arange(rows)
    to_tm = (r[:, None] % nb) * steps + r[:, None] // nb == r[None, :]
    perm = jnp.stack([to_tm, to_tm.T]).astype(BF16)
    return pl.pallas_call(
        functools.partial(_s5_body, nb=nb, steps=steps),
        grid=(seq // blk,),
        in_specs=[
            pl.BlockSpec((g, blk, d), lambda c: (0, c, 0)),
            _const_spec(perm.shape),
            _const_spec(w_u.shape), _const_spec(bb.shape), _const_spec(cm.shape),
            _const_spec(abr.shape), _const_spec(abi.shape), _const_spec(dsk.shape),
            _const_spec(wglu.shape), _const_spec(x0r.shape), _const_spec(x0i.shape),
        ],
        out_specs=[pl.BlockSpec((g, blk, su), lambda c: (0, c, 0)),
                   pl.BlockSpec((nb, nst), lambda c: (0, 0)),
                   pl.BlockSpec((nb, nst), lambda c: (0, 0))],
        out_shape=[jax.ShapeDtypeStruct((g, seq, su), BF16),
                   jax.ShapeDtypeStruct((nb, nst), F32),
                   jax.ShapeDtypeStruct((nb, nst), F32)],
        scratch_shapes=[pltpu.VMEM((S5_BLOCKS, rows, 2 * nst // S5_BLOCKS), F32),
                        pltpu.VMEM((rows, su), F32),
                        pltpu.VMEM((nb, nst), F32), pltpu.VMEM((nb, nst), F32)],
        compiler_params=_cparams(1),
        name="s5_branch",
    )(h, perm, w_u, bb, cm, abr, abi, dsk, wglu, x0r, x0i)


def _headsum(x, e):
    xb = x.astype(BF16)
    t = MXU_TILE_V7X
    return jnp.concatenate([_dot(xb[:, c:c + t], e[c:c + t, c:c + t]) for c in range(0, RW, t)], axis=1)


def _split3(x):
    hi = x.astype(BF16)
    r1 = x - hi.astype(F32)
    mid = r1.astype(BF16)
    lo = (r1 - mid.astype(F32)).astype(BF16)
    return hi, mid, lo


def _rwkv_pre(fm, w):
    r = fm[:, 0:RW]
    k = fm[:, RW:2 * RW]
    v = fm[:, 2 * RW:3 * RW]
    o0 = 3 * RW
    nw = w['w_up'].shape[0]
    na = w['a_up'].shape[0]
    wd = fm[:, o0:o0 + nw]
    ad = fm[:, o0 + nw:o0 + nw + na]
    gd = fm[:, o0 + nw + na:]
    w_log = -jax.nn.softplus(-(w['w0'][...] + _dot(jnp.tanh(wd).astype(BF16), w['w_up'][...]))) - 0.5
    logw = -jnp.exp(w_log)
    a = jax.nn.sigmoid(w['a0'][...] + _dot(ad.astype(BF16), w['a_up'][...]))
    g = _dot(jax.nn.sigmoid(gd).astype(BF16), w['g_up'][...])
    kk = k * w['k_k'][...]
    kk = kk * lax.rsqrt(jnp.maximum(_headsum(kk * kk, w['e']), 1e-24))
    k2 = k * (1.0 + (a - 1.0) * w['k_a'][...])
    return r, logw, k2, v, kk, kk * a, g


def _rwkv_post(o, r, k2, v, g, w):
    inv_n = 1.0 / HEAD
    mu = _headsum(o, w['e']) * inv_n
    d = o - mu
    var = _headsum(d * d, w['e']) * inv_n
    on = d * lax.rsqrt(var + LNX_EPS) * w['lnx_w'][...] + w['lnx_b'][...]
    bonus = _headsum(r * k2 * w['r_k'][...], w['e']) * v
    return (on + bonus) * g


def _sigmoid(x):
    return 0.5 * jnp.tanh(0.5 * x) + 0.5


def _merge_gates(h, y_b, w, c0=0, c1=None):
    dm = w['w_g'].shape[1] // 2
    c1 = dm if c1 is None else c1
    gate_a = _sigmoid(_dot(h, w['w_g'][:, c0:c1]))
    gate_b = _sigmoid(_dot(h, w['w_g'][:, dm + c0:dm + c1]))
    return gate_a, gate_b * _dot(y_b.astype(BF16), w['w_b_up'][:, c0:c1])


def _merge_out(x1, y_a, gate_a, gated_b, w):
    merged = gate_a * _dot(y_a.astype(BF16), w['w_a_up'][...]) + gated_b
    return x1 + _dot(merged.astype(BF16), w['w_out'][...])


_W_NAMES = ('w_f', 'w_g', 'mu', 'w0', 'w_up', 'a0', 'a_up', 'g_up', 'k_k', 'k_a', 'r_k',
            'lnx_w', 'lnx_b', 'e', 'w_a_up', 'w_b_up', 'w_out')


def _mixer_prompt_body(*refs, tc):
    h_ref, x1_ref, yb_ref = refs[0:3]
    nwt = len(_W_NAMES)
    w = dict(zip(_W_NAMES, refs[3:3 + nwt]))
    tri_ref, mask_ref = refs[3 + nwt:5 + nwt]
    x2_ref, shift_ref, wkv_ref = refs[5 + nwt:8 + nwt]
    (f_scr, st_scr, lhs_scr, rhs_scr, sfx_scr, v_scr, gt_scr, o_scr, ga_scr,
     gb_scr) = refs[8 + nwt:]
    c = pl.program_id(1)

    @pl.when(c == 0)
    def _():
        f_scr[0:8, :] = jnp.zeros((8, f_scr.shape[1]), F32)
        st_scr[...] = jnp.zeros_like(st_scr)

    @pl.when(c > 0)
    def _():
        f_scr[7:8, :] = f_scr[tc + 7:tc + 8, :]

    h = h_ref[...]
    f = _dot(h, w['w_f'][...])
    f_scr[8:8 + tc, :] = f
    shifted = f_scr[7:7 + tc, :]
    fm = f + w['mu'][...] * (shifted - f)
    r, logw, k2, v, kk, b, g = _rwkv_pre(fm, w)

    ns = tc // CHUNK
    dm = x1_ref.shape[1]
    yb = yb_ref[...]
    tri = tri_ref[...]
    for s in range(ns):
        sl = slice(s * CHUNK, (s + 1) * CHUNK)
        lw = logw[sl]
        hi, mid, lo = _split3(lw)
        lpre = _dot(tri, hi) + _dot(tri, mid) + _dot(tri, lo)
        lsuf = lpre[CHUNK - 1:CHUNK, :] - lpre
        e_pre = jnp.exp(lpre)
        e_neg = jnp.exp(-lpre)
        e_suf = jnp.exp(lsuf)
        lhs = jnp.concatenate([kk[sl] * jnp.exp(lpre - lw), r[sl] * e_pre], axis=0).astype(BF16)
        rhs = jnp.concatenate([b[sl] * e_neg, k2[sl] * e_neg], axis=0).astype(BF16)
        sfx_t = jnp.concatenate([b[sl] * e_suf, k2[sl] * e_suf], axis=0).T.astype(BF16)
        vb = v[sl].astype(BF16)
        for hd in range(HEADS):
            ls = slice(hd * HEAD, (hd + 1) * HEAD)
            lhs_scr[hd, s] = lhs[:, ls]
            rhs_scr[hd, s] = rhs[:, ls]
            sfx_scr[hd, s] = sfx_t[ls, :]
            v_scr[hd, s] = vb[:, ls]
        gt_scr[s] = jnp.broadcast_to(e_pre[CHUNK - 1:CHUNK, :], (8, RW))
        every = ns * MXU_TILE_V7X // dm
        if s % every == 0:
            c0 = s // every * MXU_TILE_V7X
            ga_scr[:, c0:c0 + MXU_TILE_V7X], gb_scr[:, c0:c0 + MXU_TILE_V7X] = _merge_gates(
                h, yb, w, c0, c0 + MXU_TILE_V7X)

    mask = mask_ref[...] > 0.5
    eye = (lax.broadcasted_iota(jnp.int32, (CHUNK, CHUNK), 0)
           == lax.broadcasted_iota(jnp.int32, (CHUNK, CHUNK), 1)).astype(F32)
    n_double = CHUNK.bit_length() - 2
    heads = range(HEADS)
    units = [(j, hd) for j in range(CHUNKS_PER_TRIP) for hd in heads]

    def chunk_step(i, carry):
        s0 = i * CHUNKS_PER_TRIP
        lhs = {u: lhs_scr[u[1], s0 + u[0]] for u in units}
        rhs = {u: rhs_scr[u[1], s0 + u[0]] for u in units}
        a_mat = {u: jnp.where(mask, _dot_nt(lhs[u], rhs[u]), 0.0) for u in units}
        stack = lambda top, bot: jnp.concatenate([top.astype(BF16), bot.astype(BF16)], axis=0)
        p = {u: -a_mat[u][:CHUNK, :CHUNK] for u in units}
        winv = {u: eye + p[u] for u in units}
        p = {u: _dot(p[u].astype(BF16), p[u].astype(BF16)) for u in units}
        for _ in range(n_double - 1):
            wp = {u: _dot(stack(winv[u], p[u]), p[u].astype(BF16)) for u in units}
            winv = {u: winv[u] + wp[u][:CHUNK] for u in units}
            p = {u: wp[u][CHUNK:] for u in units}
        wb = {u: (winv[u] + _dot(winv[u].astype(BF16), p[u].astype(BF16))).astype(BF16) for u in units}
        vv = {u: v_scr[u[1], s0 + u[0]] for u in units}
        mkv = {u: _dot(a_mat[u][:CHUNK, CHUNK:].astype(BF16), vv[u]).astype(BF16) for u in units}
        gb = {u: _dot(wb[u], lhs[u][:CHUNK]).astype(BF16) for u in units}
        uv = {u: jnp.concatenate([(-_dot(wb[u], mkv[u])).astype(BF16), vv[u]], axis=0)
              for u in units}
        sfx_t = {u: sfx_scr[u[1], s0 + u[0]] for u in units}
        n_bk = {u: a_mat[u][CHUNK:, :] for u in units}
        ng = {u: _dot(stack(n_bk[u][:, :CHUNK], sfx_t[u][:, :CHUNK]), gb[u]) for u in units}
        nuv = {u: _dot(stack(n_bk[u], sfx_t[u]), uv[u]) for u in units}
        gt8 = [gt_scr[s0 + j] for j in range(CHUNKS_PER_TRIP)]
        qphi = {u: stack(lhs[u][CHUNK:].astype(F32) - ng[u][:CHUNK],
                         eye * gt8[u[0]][0:1, u[1] * HEAD:(u[1] + 1) * HEAD] - ng[u][CHUNK:])
                for u in units}
        st = [st_scr[hd] for hd in heads]
        for j in range(CHUNKS_PER_TRIP):
            rs = pl.ds(pl.multiple_of((s0 + j) * CHUNK, CHUNK), CHUNK)
            os = [_dot(qphi[j, hd], st[hd].astype(BF16)) + nuv[j, hd] for hd in heads]
            for hd in heads:
                o_scr[rs, hd * HEAD:(hd + 1) * HEAD] = os[hd][:CHUNK]
            st = [os[hd][CHUNK:] for hd in heads]
        for hd in heads:
            st_scr[hd] = st[hd]
        return carry

    lax.fori_loop(0, ns // CHUNKS_PER_TRIP, chunk_step, 0)

    y_a = _rwkv_post(o_scr[...], r, k2, v, g, w)
    x2_ref[...] = _merge_out(x1_ref[...], y_a, ga_scr[...], gb_scr[...], w)

    @pl.when(c == pl.num_programs(1) - 1)
    def _():
        shift_ref[...] = f[tc - 1:tc, :]
        for hd in range(HEADS):
            wkv_ref[hd] = st_scr[hd].T


def _mixer_prompt(h, x1, yb, wts, tri, mask, *, tc):
    bsz, seq, d = h.shape
    assert seq % tc == 0 and tc % (CHUNK * CHUNKS_PER_TRIP) == 0
    nf = wts['w_f'].shape[1]
    wlist = [wts[n] for n in _W_NAMES]
    tok = lambda width: pl.BlockSpec((None, tc, width), lambda b, c: (b, c, 0))
    return pl.pallas_call(
        functools.partial(_mixer_prompt_body, tc=tc),
        grid=(bsz, seq // tc),
        in_specs=[tok(d), tok(d), tok(RW)] + [_const_spec(a.shape) for a in wlist]
                 + [_const_spec(tri.shape), _const_spec(mask.shape)],
        out_specs=[tok(d),
                   pl.BlockSpec((None, 1, nf), lambda b, c: (b, 0, 0)),
                   pl.BlockSpec((None, HEADS, HEAD, HEAD), lambda b, c: (b, 0, 0, 0))],
        out_shape=[jax.ShapeDtypeStruct((bsz, seq, d), F32),
                   jax.ShapeDtypeStruct((bsz, 1, nf), F32),
                   jax.ShapeDtypeStruct((bsz, HEADS, HEAD, HEAD), F32)],
        scratch_shapes=[pltpu.VMEM((tc + 8, nf), F32),
                        pltpu.VMEM((HEADS, HEAD, HEAD), F32)]
                       + [pltpu.VMEM((HEADS, tc // CHUNK, 2 * CHUNK, HEAD), BF16)] * 2
                       + [pltpu.VMEM((HEADS, tc // CHUNK, HEAD, 2 * CHUNK), BF16),
                          pltpu.VMEM((HEADS, tc // CHUNK, CHUNK, HEAD), BF16),
                          pltpu.VMEM((tc // CHUNK, 8, RW), F32),
                          pltpu.VMEM((tc, RW), F32),
                          pltpu.VMEM((tc, d), F32), pltpu.VMEM((tc, d), F32)],
        compiler_params=_cparams(2),
        name="mixer_prompt",
    )(h, x1, yb, *wlist, tri, mask)


def _mixer_sample_body(*refs):
    h_ref, x1_ref, yb_ref, shift_in_ref, s_ref = refs[0:5]
    nwt = len(_W_NAMES)
    w = dict(zip(_W_NAMES, refs[5:5 + nwt]))
    x2_ref, shift_ref, s_out_ref = refs[5 + nwt:8 + nwt]
    t_scr, row_scr, ot_scr = refs[8 + nwt:]
    hd = pl.program_id(0)

    @pl.when(hd == 0)
    def _():
        f = _dot(h_ref[...], w['w_f'][...])
        fm = f + w['mu'][...] * (shift_in_ref[...] - f)
        shift_ref[...] = f
        r, logw, k2, v, kk, b, g = _rwkv_pre(fm, w)
        for i, a in enumerate((kk, jnp.exp(logw), b, k2, r, v)):
            t_scr[i] = a.T
        for i, a in enumerate((r, k2, v, g)):
            row_scr[i] = a

    off = pl.multiple_of(hd * HEAD, HEAD)
    hs = pl.ds(off, HEAD)
    kk_t, dec_t, b_t, k_t, r_t = (t_scr[i, hs, :] for i in range(5))

    def v_group(vg, carry):
        v0 = pl.multiple_of(vg * 8, 8)
        s8 = s_ref[pl.ds(v0, 8)]
        v8 = t_scr[5, pl.ds(off + v0, 8), :]
        o_rows = []
        for j in range(8):
            sa = -jnp.sum(s8[j] * kk_t, axis=0, keepdims=True)
            sn = s8[j] * dec_t + sa * b_t + v8[j:j + 1, :] * k_t
            s_out_ref[v0 + j] = sn
            o_rows.append(jnp.sum(sn * r_t, axis=0, keepdims=True))
        ot_scr[pl.ds(off + v0, 8), :] = jnp.concatenate(o_rows, axis=0)
        return carry

    lax.fori_loop(0, HEAD // 8, v_group, 0)

    @pl.when(hd == pl.num_programs(0) - 1)
    def _():
        r, k2, v, g = (row_scr[i] for i in range(4))
        y_a = _rwkv_post(ot_scr[...].T, r, k2, v, g, w)
        gate_a, gated_b = _merge_gates(h_ref[...], yb_ref[...], w)
        x2_ref[...] = _merge_out(x1_ref[...], y_a, gate_a, gated_b, w)


def _mixer_sample(h, x1, yb, shift_in, wkv_t, wts):
    n, d = h.shape
    nf = wts['w_f'].shape[1]
    wlist = [wts[nm] for nm in _W_NAMES]
    st_spec = pl.BlockSpec((None, HEAD, HEAD, n), lambda hd: (hd, 0, 0, 0))
    return pl.pallas_call(
        _mixer_sample_body,
        grid=(HEADS,),
        in_specs=[_const_spec(a.shape) for a in (h, x1, yb, shift_in)] + [st_spec]
                 + [_const_spec(a.shape) for a in wlist],
        out_specs=[pl.BlockSpec((n, d), lambda hd: (0, 0)),
                   pl.BlockSpec((n, nf), lambda hd: (0, 0)), st_spec],
        out_shape=[jax.ShapeDtypeStruct((n, d), F32),
                   jax.ShapeDtypeStruct((n, nf), F32),
                   jax.ShapeDtypeStruct(wkv_t.shape, F32)],
        scratch_shapes=[pltpu.VMEM((6, RW, n), F32), pltpu.VMEM((4, n, RW), F32),
                        pltpu.VMEM((RW, n), F32)],
        compiler_params=_cparams(1),
        name="mixer_sample",
    )(h, x1, yb, shift_in, wkv_t, *wlist)


def _chunk_constants():
    t = jnp.arange(CHUNK)
    incl = (t[:, None] >= t[None, :])
    strict = (t[:, None] > t[None, :])
    tri = incl.astype(BF16)
    top = jnp.concatenate([strict, strict], axis=1)
    bot = jnp.concatenate([incl, incl], axis=1)
    mask = jnp.concatenate([top, bot], axis=0).astype(F32)
    return tri, mask


def kernel(x_prompt, x_sample, state_shift, state_wkv, state_s5_re, state_s5_im, g_ffn1, ffn1_gate, ffn1_up, ffn1_down, g_mix, w_in, mu_shift, w0, w_up, a0, a_up, g_up, k_k, k_a, r_k, lnx_w, lnx_b, A_re, A_im, log_dt, B_re, B_im, C_re, C_im, D_skip, w_glu, w_a_up, w_b_up, w_out, g_ffn2, ffn2_gate, ffn2_up, ffn2_down, g_final):
    depth = g_ffn1.shape[0]
    assert depth == 1
    bp, seq, d = x_prompt.shape
    bs = x_sample.shape[0]
    assert x_sample.shape[1] == 1
    ncols_f = mu_shift.shape[1]
    su = D_skip.shape[1]
    groups, pstate = A_re.shape[1:]
    nstate = groups * pstate
    bf = lambda a: a.astype(BF16)
    row = lambda a: a.reshape(1, -1).astype(F32)

    w_in0 = w_in[0]
    eye_h = jnp.eye(HEADS, dtype=F32)
    wts = {
        'w_f': bf(w_in0[:, :ncols_f]),
        'w_g': bf(w_in0[:, ncols_f + su:]),
        'mu': row(mu_shift[0]), 'w0': row(w0[0]), 'w_up': bf(w_up[0]), 'a0': row(a0[0]),
        'a_up': bf(a_up[0]), 'g_up': bf(g_up[0]), 'k_k': row(k_k[0]), 'k_a': row(k_a[0]),
        'r_k': row(r_k[0]), 'lnx_w': row(lnx_w[0]), 'lnx_b': row(lnx_b[0]),
        'e': bf(jnp.kron(eye_h, jnp.ones((HEAD, HEAD), F32))),
        'w_a_up': bf(w_a_up[0]), 'w_b_up': bf(w_b_up[0]), 'w_out': bf(w_out[0]),
    }
    w_u = bf(w_in0[:, ncols_f:ncols_f + su])
    abr, abi, bbr, bbi = _s5_params(A_re[0], A_im[0], log_dt[0],
                                    jnp.swapaxes(B_re[0], 1, 2), jnp.swapaxes(B_im[0], 1, 2))
    bb = bf(jnp.concatenate([_block_diag(bbr, S5_BLOCKS), _block_diag(bbi, S5_BLOCKS)], axis=2))
    cm = bf(jnp.concatenate([_block_diag(jnp.swapaxes(C_re[0], 1, 2), S5_BLOCKS),
                             _block_diag(-jnp.swapaxes(C_im[0], 1, 2), S5_BLOCKS)], axis=1))
    abr = abr.reshape(1, nstate)
    abi = abi.reshape(1, nstate)
    dsk = row(D_skip[0])
    wglu = bf(w_glu[0])
    ffn1 = (row(g_ffn1[0]), bf(ffn1_gate[0]), bf(ffn1_up[0]), bf(ffn1_down[0]), row(g_mix[0]))
    ffn2 = (row(g_ffn2[0]), bf(ffn2_gate[0]), bf(ffn2_up[0]), bf(ffn2_down[0]), row(g_final))
    fc = 4 * MXU_TILE_V7X
    tri, mask = _chunk_constants()

    xp = x_prompt.reshape(bp * seq, d)
    x1p, hnp = _ffn(xp, *ffn1, final=False, tm=1024, fc=fc)
    hnp = hnp.reshape(bp, seq, d)
    zeros_state = jnp.zeros((bp, nstate), F32)
    yb_p, s5r_p, s5i_p = _s5(hnp, w_u, bb, cm, abr, abi, dsk, wglu, zeros_state, zeros_state,
                             nb=bp, steps=64)
    x2p, shift_p, wkv_p = _mixer_prompt(hnp, x1p.reshape(bp, seq, d), yb_p, wts, tri, mask, tc=512)
    y_prompt = _ffn(x2p.reshape(bp * seq, d), *ffn2, final=True, tm=1024, fc=fc).reshape(bp, seq, d)

    xs = x_sample.reshape(bs, d)
    x1s, hns = _ffn(xs, *ffn1, final=False, tm=bs, fc=fc)
    yb_s, s5r_s, s5i_s = _s5(hns[None], w_u, bb, cm, abr, abi, dsk, wglu,
                             state_s5_re[0].reshape(bs, nstate), state_s5_im[0].reshape(bs, nstate),
                             nb=bs, steps=1)
    wkv_t = jnp.transpose(state_wkv[0], (1, 2, 3, 0))
    x2s, shift_s, wkv_ts = _mixer_sample(hns, x1s, yb_s[0], state_shift[0], wkv_t, wts)
    wkv_s = jnp.transpose(wkv_ts, (3, 0, 1, 2))
    y_sample = _ffn(x2s, *ffn2, final=True, tm=bs, fc=fc).reshape(bs, 1, d)

    st5 = lambda a, n: a.reshape(1, n, groups, pstate)
    return (y_prompt, y_sample,
            shift_p.reshape(1, bp, ncols_f), wkv_p[None], st5(s5r_p, bp), st5(s5i_p, bp),
            shift_s[None], wkv_s[None], st5(s5r_s, bs), st5(s5i_s, bs))
```

```python
import functools

import jax
import jax.numpy as jnp
from jax import lax
from jax.experimental import pallas as pl
from jax.experimental.pallas import tpu as pltpu

F32 = jnp.float32
BF16 = jnp.bfloat16

NORM_EPS = 1e-6
LNX_EPS = 64e-5
HEAD = 64
HEADS = 8
PAIRS = HEADS // 2
RW = HEAD * HEADS
S5_BLOCKS = 4
CHUNK = 64
CHUNKS_PER_TRIP = 8
VMEM_LIMIT = 56 * 1024 * 1024
MXU_TILE_V7X = 256


def _dot(a, b):
    return jnp.dot(a, b, preferred_element_type=F32)


def _dot_nt(a, b):
    return lax.dot_general(a, b, (((1,), (1,)), ((), ())), preferred_element_type=F32)


def _rms(x, g):
    ms = jnp.mean(x * x, axis=-1, keepdims=True)
    return x * lax.rsqrt(ms + NORM_EPS) * g


def _const_spec(shape):
    nd = len(shape)
    return pl.BlockSpec(shape, lambda *_: (0,) * nd, pipeline_mode=pl.Buffered(1))


def _cparams(ngrid):
    return pltpu.CompilerParams(dimension_semantics=("arbitrary",) * ngrid,
                                vmem_limit_bytes=VMEM_LIMIT)


def _ffn_body(x_ref, g_ref, wg_ref, wu_ref, wd_ref, gn_ref, *outs, final, fc):
    x = x_ref[...]
    hn = _rms(x, g_ref[...]).astype(BF16)
    acc = None
    f = wg_ref.shape[1]
    for c0 in range(0, f, fc):
        c1 = min(c0 + fc, f)
        gate = _dot(hn, wg_ref[:, c0:c1])
        up = _dot(hn, wu_ref[:, c0:c1])
        act = (jax.nn.silu(gate) * up).astype(BF16)
        part = _dot(act, wd_ref[c0:c1, :])
        acc = part if acc is None else acc + part
    x1 = x + 0.5 * acc
    if final:
        outs[0][...] = _rms(x1, gn_ref[...])
    else:
        outs[0][...] = x1
        outs[1][...] = _rms(x1, gn_ref[...]).astype(BF16)


def _ffn(x, g, wg, wu, wd, gn, *, final, tm, fc):
    m, d = x.shape
    assert m % tm == 0 and fc % MXU_TILE_V7X == 0
    tile = pl.BlockSpec((tm, d), lambda i: (i, 0))
    if final:
        out_shape = jax.ShapeDtypeStruct((m, d), F32)
        out_specs = tile
    else:
        out_shape = [jax.ShapeDtypeStruct((m, d), F32), jax.ShapeDtypeStruct((m, d), BF16)]
        out_specs = [tile, tile]
    return pl.pallas_call(
        functools.partial(_ffn_body, final=final, fc=fc),
        grid=(m // tm,),
        in_specs=[tile, _const_spec(g.shape), _const_spec(wg.shape), _const_spec(wu.shape),
                  _const_spec(wd.shape), _const_spec(gn.shape)],
        out_specs=out_specs,
        out_shape=out_shape,
        compiler_params=_cparams(1),
        name="ffn_final" if final else "ffn_in",
    )(x, g, wg, wu, wd, gn)


def _s5_param_body(are_ref, aim_ref, ldt_ref, bre_ref, bim_ref, abr_ref, abi_ref, bbr_ref, bbi_ref):
    lam_r = are_ref[...]
    lam_i = aim_ref[...]
    dt = jnp.exp(ldt_ref[...])
    mag = jnp.exp(lam_r * dt)
    ab_r = mag * jnp.cos(lam_i * dt)
    ab_i = mag * jnp.sin(lam_i * dt)
    den = lam_r * lam_r + lam_i * lam_i
    q_r = ((ab_r - 1.0) * lam_r + ab_i * lam_i) / den
    q_i = (ab_i * lam_r - (ab_r - 1.0) * lam_i) / den
    abr_ref[...] = ab_r
    abi_ref[...] = ab_i
    bre = bre_ref[...]
    bim = bim_ref[...]
    bbr_ref[...] = q_r[:, None, :] * bre - q_i[:, None, :] * bim
    bbi_ref[...] = q_r[:, None, :] * bim + q_i[:, None, :] * bre


def _s5_params(a_re, a_im, log_dt, b_re_t, b_im_t):
    g, p = a_re.shape
    hg = b_re_t.shape[1]
    return pl.pallas_call(
        _s5_param_body,
        out_shape=[jax.ShapeDtypeStruct((g, p), F32), jax.ShapeDtypeStruct((g, p), F32),
                   jax.ShapeDtypeStruct((g, hg, p), F32), jax.ShapeDtypeStruct((g, hg, p), F32)],
        name="s5_params",
    )(a_re, a_im, log_dt.reshape(g, 1), b_re_t, b_im_t)


def _block_diag(w, nblk):
    g, a, b = w.shape
    gb = g // nblk
    w4 = w.reshape(nblk, gb, a, b)
    eye = jnp.eye(gb, dtype=w.dtype)
    return jnp.einsum('jgab,gk->jgakb', w4, eye).reshape(nblk, gb * a, gb * b)


def _s5_body(h_ref, perm_ref, wu_ref, bb_ref, cm_ref, abr_ref, abi_ref, dsk_ref, wglu_ref, x0r_ref,
             x0i_ref, yb_ref, xr_out, xi_out, bu_scr, y_scr, xr_scr, xi_scr, *, nb, steps):
    c = pl.program_id(0)
    rows = nb * steps

    @pl.when(c == 0)
    def _():
        xr_scr[...] = x0r_ref[...]
        xi_scr[...] = x0i_ref[...]

    h = h_ref[...].reshape(rows, h_ref.shape[-1])
    if steps > 1:
        h = _dot(perm_ref[0], h).astype(BF16)
    u = _dot(h, wu_ref[...])
    ub = u.astype(BF16)
    nin = ub.shape[1] // S5_BLOCKS
    ns = abr_ref.shape[1] // S5_BLOCKS
    for j in range(S5_BLOCKS):
        bu_scr[j] = _dot(ub[:, j * nin:(j + 1) * nin], bb_ref[j])
    for j in range(S5_BLOCKS):
        ar = jnp.broadcast_to(abr_ref[:, j * ns:(j + 1) * ns], (nb, ns))
        ai = jnp.broadcast_to(abi_ref[:, j * ns:(j + 1) * ns], (nb, ns))
        xr = xr_scr[:, j * ns:(j + 1) * ns]
        xi = xi_scr[:, j * ns:(j + 1) * ns]
        for t in range(steps):
            tr = slice(t * nb, (t + 1) * nb)
            xr, xi = (ar * xr - ai * xi + bu_scr[j, tr, 0:ns],
                      ar * xi + ai * xr + bu_scr[j, tr, ns:2 * ns])
            bu_scr[j, tr, 0:ns] = xr
            bu_scr[j, tr, ns:2 * ns] = xi
        xr_scr[:, j * ns:(j + 1) * ns] = xr
        xi_scr[:, j * ns:(j + 1) * ns] = xi
        y_scr[:, j * nin:(j + 1) * nin] = _dot(bu_scr[j].astype(BF16), cm_ref[j])
    y = y_scr[...] + dsk_ref[...] * u
    z = jax.nn.gelu(y)
    yb = (z * jax.nn.sigmoid(_dot(z.astype(BF16), wglu_ref[...]))).astype(BF16)
    if steps > 1:
        yb = _dot(perm_ref[1], yb).astype(BF16)
    yb_ref[...] = yb.reshape(yb_ref.shape)
    xr_out[...] = xr_scr[...]
    xi_out[...] = xi_scr[...]


def _s5(h, w_u, bb, cm, abr, abi, dsk, wglu, x0r, x0i, *, nb, steps):
    g, seq, d = h.shape
    rows = nb * steps
    assert seq % (rows // g) == 0 and (g == nb or steps == 1)
    blk = rows // g
    su = w_u.shape[1]
    nst = abr.shape[1]
    r = jnp.arange(rows)
    to_tm = (r[:, None] % nb) * steps + r[:, None] // nb == r[None, :]
    perm = jnp.stack([to_tm, to_tm.T]).astype(BF16)
    return pl.pallas_call(
        functools.partial(_s5_body, nb=nb, steps=steps),
        grid=(seq // blk,),
        in_specs=[
            pl.BlockSpec((g, blk, d), lambda c: (0, c, 0)),
            _const_spec(perm.shape),
            _const_spec(w_u.shape), _const_spec(bb.shape), _const_spec(cm.shape),
            _const_spec(abr.shape), _const_spec(abi.shape), _const_spec(dsk.shape),
            _const_spec(wglu.shape), _const_spec(x0r.shape), _const_spec(x0i.shape),
        ],
        out_specs=[pl.BlockSpec((g, blk, su), lambda c: (0, c, 0)),
                   pl.BlockSpec((nb, nst), lambda c: (0, 0)),
                   pl.BlockSpec((nb, nst), lambda c: (0, 0))],
        out_shape=[jax.ShapeDtypeStruct((g, seq, su), BF16),
                   jax.ShapeDtypeStruct((nb, nst), F32),
                   jax.ShapeDtypeStruct((nb, nst), F32)],
        scratch_shapes=[pltpu.VMEM((S5_BLOCKS, rows, 2 * nst // S5_BLOCKS), F32),
                        pltpu.VMEM((rows, su), F32),
                        pltpu.VMEM((nb, nst), F32), pltpu.VMEM((nb, nst), F32)],
        compiler_params=_cparams(1),
        name="s5_branch",
    )(h, perm, w_u, bb, cm, abr, abi, dsk, wglu, x0r, x0i)


def _headsum(x, e):
    xb = x.astype(BF16)
    t = MXU_TILE_V7X
    return jnp.concatenate([_dot(xb[:, c:c + t], e[c:c + t, c:c + t]) for c in range(0, RW, t)], axis=1)


def _split3(x):
    hi = x.astype(BF16)
    r1 = x - hi.astype(F32)
    mid = r1.astype(BF16)
    lo = (r1 - mid.astype(F32)).astype(BF16)
    return hi, mid, lo


def _rwkv_pre(fm, w):
    r = fm[:, 0:RW]
    k = fm[:, RW:2 * RW]
    v = fm[:, 2 * RW:3 * RW]
    o0 = 3 * RW
    nw = w['w_up'].shape[0]
    na = w['a_up'].shape[0]
    wd = fm[:, o0:o0 + nw]
    ad = fm[:, o0 + nw:o0 + nw + na]
    gd = fm[:, o0 + nw + na:]
    w_log = -jax.nn.softplus(-(w['w0'][...] + _dot(jnp.tanh(wd).astype(BF16), w['w_up'][...]))) - 0.5
    logw = -jnp.exp(w_log)
    a = jax.nn.sigmoid(w['a0'][...] + _dot(ad.astype(BF16), w['a_up'][...]))
    g = _dot(jax.nn.sigmoid(gd).astype(BF16), w['g_up'][...])
    kk = k * w['k_k'][...]
    kk = kk * lax.rsqrt(jnp.maximum(_headsum(kk * kk, w['e']), 1e-24))
    k2 = k * (1.0 + (a - 1.0) * w['k_a'][...])
    return r, logw, k2, v, kk, kk * a, g


def _rwkv_post(o, r, k2, v, g, w):
    inv_n = 1.0 / HEAD
    mu = _headsum(o, w['e']) * inv_n
    d = o - mu
    var = _headsum(d * d, w['e']) * inv_n
    on = d * lax.rsqrt(var + LNX_EPS) * w['lnx_w'][...] + w['lnx_b'][...]
    bonus = _headsum(r * k2 * w['r_k'][...], w['e']) * v
    return (on + bonus) * g


def _sigmoid(x):
    return 0.5 * jnp.tanh(0.5 * x) + 0.5


def _merge_gates(h, y_b, w, c0=0, c1=None):
    dm = w['w_g'].shape[1] // 2
    c1 = dm if c1 is None else c1
    gate_a = _sigmoid(_dot(h, w['w_g'][:, c0:c1]))
    gate_b = _sigmoid(_dot(h, w['w_g'][:, dm + c0:dm + c1]))
    return gate_a, gate_b * _dot(y_b.astype(BF16), w['w_b_up'][:, c0:c1])


def _merge_out(x1, y_a, gate_a, gated_b, w):
    merged = gate_a * _dot(y_a.astype(BF16), w['w_a_up'][...]) + gated_b
    return x1 + _dot(merged.astype(BF16), w['w_out'][...])


_W_NAMES = ('w_f', 'w_g', 'mu', 'w0', 'w_up', 'a0', 'a_up', 'g_up', 'k_k', 'k_a', 'r_k',
            'lnx_w', 'lnx_b', 'e', 'w_a_up', 'w_b_up', 'w_out')


def _mixer_prompt_body(*refs, tc):
    h_ref, x1_ref, yb_ref = refs[0:3]
    nwt = len(_W_NAMES)
    w = dict(zip(_W_NAMES, refs[3:3 + nwt]))
    tri_ref, mask_ref = refs[3 + nwt:5 + nwt]
    x2_ref, shift_ref, wkv_ref = refs[5 + nwt:8 + nwt]
    (f_scr, st_scr, lhs_scr, rhs_scr, sfx_scr, v_scr, gt_scr, o_scr, ga_scr,
     gb_scr) = refs[8 + nwt:]
    c = pl.program_id(1)

    @pl.when(c == 0)
    def _():
        f_scr[0:8, :] = jnp.zeros((8, f_scr.shape[1]), F32)
        st_scr[...] = jnp.zeros_like(st_scr)

    @pl.when(c > 0)
    def _():
        f_scr[7:8, :] = f_scr[tc + 7:tc + 8, :]

    h = h_ref[...]
    f = _dot(h, w['w_f'][...])
    f_scr[8:8 + tc, :] = f
    shifted = f_scr[7:7 + tc, :]
    fm = f + w['mu'][...] * (shifted - f)
    r, logw, k2, v, kk, b, g = _rwkv_pre(fm, w)

    ns = tc // CHUNK
    dm = x1_ref.shape[1]
    yb = yb_ref[...]
    tri = tri_ref[...]
    for s in range(ns):
        sl = slice(s * CHUNK, (s + 1) * CHUNK)
        lw = logw[sl]
        hi, mid, lo = _split3(lw)
        lpre = _dot(tri, hi) + _dot(tri, mid) + _dot(tri, lo)
        lsuf = lpre[CHUNK - 1:CHUNK, :] - lpre
        e_pre = jnp.exp(lpre)
        e_neg = jnp.exp(-lpre)
        e_suf = jnp.exp(lsuf)
        lhs = jnp.concatenate([kk[sl] * jnp.exp(lpre - lw), r[sl] * e_pre], axis=0).astype(BF16)
        rhs = jnp.concatenate([b[sl] * e_neg, k2[sl] * e_neg], axis=0).astype(BF16)
        sfx_t = jnp.concatenate([b[sl] * e_suf, k2[sl] * e_suf], axis=0).T.astype(BF16)
        vb = v[sl].astype(BF16)
        for hd in range(HEADS):
            sfx_scr[hd, s] = sfx_t[hd * HEAD:(hd + 1) * HEAD, :]
        for pr in range(PAIRS):
            ls = slice(pr * 2 * HEAD, (pr + 1) * 2 * HEAD)
            lhs_scr[pr, s] = lhs[:, ls]
            rhs_scr[pr, s] = rhs[:, ls]
            v_scr[pr, s] = vb[:, ls]
        gt_scr[s] = jnp.broadcast_to(e_pre[CHUNK - 1:CHUNK, :], (8, RW))
        every = ns * MXU_TILE_V7X // dm
        if s % every == 0:
            c0 = s // every * MXU_TILE_V7X
            ga_scr[:, c0:c0 + MXU_TILE_V7X], gb_scr[:, c0:c0 + MXU_TILE_V7X] = _merge_gates(
                h, yb, w, c0, c0 + MXU_TILE_V7X)

    mask = mask_ref[...] > 0.5
    pw = 2 * HEAD
    lane = lambda rows: lax.broadcasted_iota(jnp.int32, (rows, pw), 1)
    low = lambda a: lane(a.shape[0]) < HEAD
    eye2 = (lax.broadcasted_iota(jnp.int32, (CHUNK, pw), 0) == lane(CHUNK) % HEAD).astype(F32)
    n_double = CHUNK.bit_length() - 2
    pairs = range(PAIRS)
    units = [(j, pr) for j in range(CHUNKS_PER_TRIP) for pr in pairs]
    first = lambda a: jnp.where(low(a), a, jnp.zeros_like(a))
    second = lambda a: jnp.where(low(a), jnp.zeros_like(a), a)
    bd = lambda y: jnp.concatenate([first(y), second(y)], axis=0)
    ad = lambda y: jnp.concatenate([second(y), first(y)], axis=0)
    pick = lambda x, y: jnp.where(low(x), x, y)
    stack = lambda top, bot: jnp.concatenate([top.astype(BF16), bot.astype(BF16)], axis=0)
    bf = lambda a: a.astype(BF16)

    def chunk_step(i, carry):
        s0 = i * CHUNKS_PER_TRIP
        lhs = {u: lhs_scr[u[1], s0 + u[0]] for u in units}
        rhs = {u: rhs_scr[u[1], s0 + u[0]] for u in units}
        a0 = {u: jnp.where(mask, _dot_nt(first(lhs[u]), rhs[u]), 0.0) for u in units}
        a1 = {u: jnp.where(mask, _dot_nt(second(lhs[u]),
                                         jnp.concatenate([rhs[u][CHUNK:], rhs[u][:CHUNK]], axis=0)), 0.0)
              for u in units}
        m_b = {u: pick(a0[u][:CHUNK], a1[u][:CHUNK]) for u in units}
        m_ks = {u: pick(a1[u][:CHUNK], a0[u][:CHUNK]) for u in units}
        n_b = {u: pick(a0[u][CHUNK:], a1[u][CHUNK:]) for u in units}
        n_ks = {u: pick(a1[u][CHUNK:], a0[u][CHUNK:]) for u in units}
        p = {u: -m_b[u] for u in units}
        winv = {u: eye2 + p[u] for u in units}
        p = {u: _dot(bf(p[u]), bd(bf(p[u]))) for u in units}
        for _ in range(n_double - 1):
            wp = {u: _dot(stack(winv[u], p[u]), bd(bf(p[u]))) for u in units}
            winv = {u: winv[u] + wp[u][:CHUNK] for u in units}
            p = {u: wp[u][CHUNK:] for u in units}
        wb = {u: bf(winv[u] + _dot(bf(winv[u]), bd(bf(p[u])))) for u in units}
        vv = {u: v_scr[u[1], s0 + u[0]] for u in units}
        adv = {u: ad(vv[u]) for u in units}
        mkv = {u: bf(_dot(bf(m_ks[u]), adv[u])) for u in units}
        gu = {u: _dot(wb[u], jnp.concatenate([bd(lhs[u][:CHUNK]), bd(mkv[u])], axis=1)) for u in units}
        bdg = {u: bd(bf(gu[u][:, :pw])) for u in units}
        bdu = {u: bd(bf(-gu[u][:, pw:])) for u in units}
        q = {u: lhs[u][CHUNK:].astype(F32) - _dot(bf(n_b[u]), bdg[u]) for u in units}
        o0 = {u: _dot(jnp.concatenate([bf(n_b[u]), bf(n_ks[u])], axis=1),
                      jnp.concatenate([bdu[u], adv[u]], axis=0)) for u in units}
        sfx_t = {u: jnp.concatenate([sfx_scr[2 * u[1], s0 + u[0]], sfx_scr[2 * u[1] + 1, s0 + u[0]]],
                                    axis=1) for u in units}
        zc = jnp.zeros((CHUNK, pw), BF16)
        gv = {u: jnp.concatenate(
            [jnp.concatenate([bdg[u][:CHUNK], bdu[u][:CHUNK]], axis=1),
             jnp.concatenate([zc, adv[u][CHUNK:]], axis=1),
             jnp.concatenate([bdg[u][CHUNK:], bdu[u][CHUNK:]], axis=1),
             jnp.concatenate([zc, adv[u][:CHUNK]], axis=1)], axis=0) for u in units}
        sg = {u: _dot(sfx_t[u], gv[u]) for u in units}
        gt8 = [gt_scr[s0 + j] for j in range(CHUNKS_PER_TRIP)]
        qphi = {u: stack(q[u], eye2 * gt8[u[0]][0:1, u[1] * pw:(u[1] + 1) * pw] - sg[u][:, :pw])
                for u in units}
        opsi = {u: jnp.concatenate([o0[u], sg[u][:, pw:]], axis=0) for u in units}
        st = [st_scr[pr] for pr in pairs]
        for j in range(CHUNKS_PER_TRIP):
            rs = pl.ds(pl.multiple_of((s0 + j) * CHUNK, CHUNK), CHUNK)
            os = [_dot(qphi[j, pr], bd(bf(st[pr]))) + opsi[j, pr] for pr in pairs]
            for pr in pairs:
                o_scr[rs, pr * pw:(pr + 1) * pw] = os[pr][:CHUNK]
            st = [os[pr][CHUNK:] for pr in pairs]
        for pr in pairs:
            st_scr[pr] = st[pr]
        return carry

    lax.fori_loop(0, ns // CHUNKS_PER_TRIP, chunk_step, 0)

    y_a = _rwkv_post(o_scr[...], r, k2, v, g, w)
    x2_ref[...] = _merge_out(x1_ref[...], y_a, ga_scr[...], gb_scr[...], w)

    @pl.when(c == pl.num_programs(1) - 1)
    def _():
        shift_ref[...] = f[tc - 1:tc, :]
        for hd in range(HEADS):
            wkv_ref[hd] = st_scr[hd // 2][:, (hd % 2) * HEAD:(hd % 2 + 1) * HEAD].T


def _mixer_prompt(h, x1, yb, wts, tri, mask, *, tc):
    bsz, seq, d = h.shape
    assert seq % tc == 0 and tc % (CHUNK * CHUNKS_PER_TRIP) == 0
    nf = wts['w_f'].shape[1]
    wlist = [wts[n] for n in _W_NAMES]
    tok = lambda width: pl.BlockSpec((None, tc, width), lambda b, c: (b, c, 0))
    return pl.pallas_call(
        functools.partial(_mixer_prompt_body, tc=tc),
        grid=(bsz, seq // tc),
        in_specs=[tok(d), tok(d), tok(RW)] + [_const_spec(a.shape) for a in wlist]
                 + [_const_spec(tri.shape), _const_spec(mask.shape)],
        out_specs=[tok(d),
                   pl.BlockSpec((None, 1, nf), lambda b, c: (b, 0, 0)),
                   pl.BlockSpec((None, HEADS, HEAD, HEAD), lambda b, c: (b, 0, 0, 0))],
        out_shape=[jax.ShapeDtypeStruct((bsz, seq, d), F32),
                   jax.ShapeDtypeStruct((bsz, 1, nf), F32),
                   jax.ShapeDtypeStruct((bsz, HEADS, HEAD, HEAD), F32)],
        scratch_shapes=[pltpu.VMEM((tc + 8, nf), F32),
                        pltpu.VMEM((PAIRS, HEAD, 2 * HEAD), F32)]
                       + [pltpu.VMEM((PAIRS, tc // CHUNK, 2 * CHUNK, 2 * HEAD), BF16)] * 2
                       + [pltpu.VMEM((HEADS, tc // CHUNK, HEAD, 2 * CHUNK), BF16),
                          pltpu.VMEM((PAIRS, tc // CHUNK, CHUNK, 2 * HEAD), BF16),
                          pltpu.VMEM((tc // CHUNK, 8, RW), F32),
                          pltpu.VMEM((tc, RW), F32),
                          pltpu.VMEM((tc, d), F32), pltpu.VMEM((tc, d), F32)],
        compiler_params=_cparams(2),
        name="mixer_prompt",
    )(h, x1, yb, *wlist, tri, mask)


def _mixer_sample_body(*refs):
    h_ref, x1_ref, yb_ref, shift_in_ref, s_ref = refs[0:5]
    nwt = len(_W_NAMES)
    w = dict(zip(_W_NAMES, refs[5:5 + nwt]))
    x2_ref, shift_ref, s_out_ref = refs[5 + nwt:8 + nwt]
    t_scr, row_scr, ot_scr = refs[8 + nwt:]
    hd = pl.program_id(0)

    @pl.when(hd == 0)
    def _():
        f = _dot(h_ref[...], w['w_f'][...])
        fm = f + w['mu'][...] * (shift_in_ref[...] - f)
        shift_ref[...] = f
        r, logw, k2, v, kk, b, g = _rwkv_pre(fm, w)
        for i, a in enumerate((kk, jnp.exp(logw), b, k2, r, v)):
            t_scr[i] = a.T
        for i, a in enumerate((r, k2, v, g)):
            row_scr[i] = a

    off = pl.multiple_of(hd * HEAD, HEAD)
    hs = pl.ds(off, HEAD)
    kk_t, dec_t, b_t, k_t, r_t = (t_scr[i, hs, :] for i in range(5))

    def v_group(vg, carry):
        v0 = pl.multiple_of(vg * 8, 8)
        s8 = s_ref[pl.ds(v0, 8)]
        v8 = t_scr[5, pl.ds(off + v0, 8), :]
        o_rows = []
        for j in range(8):
            sa = -jnp.sum(s8[j] * kk_t, axis=0, keepdims=True)
            sn = s8[j] * dec_t + sa * b_t + v8[j:j + 1, :] * k_t
            s_out_ref[v0 + j] = sn
            o_rows.append(jnp.sum(sn * r_t, axis=0, keepdims=True))
        ot_scr[pl.ds(off + v0, 8), :] = jnp.concatenate(o_rows, axis=0)
        return carry

    lax.fori_loop(0, HEAD // 8, v_group, 0)

    @pl.when(hd == pl.num_programs(0) - 1)
    def _():
        r, k2, v, g = (row_scr[i] for i in range(4))
        y_a = _rwkv_post(ot_scr[...].T, r, k2, v, g, w)
        gate_a, gated_b = _merge_gates(h_ref[...], yb_ref[...], w)
        x2_ref[...] = _merge_out(x1_ref[...], y_a, gate_a, gated_b, w)


def _mixer_sample(h, x1, yb, shift_in, wkv_t, wts):
    n, d = h.shape
    nf = wts['w_f'].shape[1]
    wlist = [wts[nm] for nm in _W_NAMES]
    st_spec = pl.BlockSpec((None, HEAD, HEAD, n), lambda hd: (hd, 0, 0, 0))
    return pl.pallas_call(
        _mixer_sample_body,
        grid=(HEADS,),
        in_specs=[_const_spec(a.shape) for a in (h, x1, yb, shift_in)] + [st_spec]
                 + [_const_spec(a.shape) for a in wlist],
        out_specs=[pl.BlockSpec((n, d), lambda hd: (0, 0)),
                   pl.BlockSpec((n, nf), lambda hd: (0, 0)), st_spec],
        out_shape=[jax.ShapeDtypeStruct((n, d), F32),
                   jax.ShapeDtypeStruct((n, nf), F32),
                   jax.ShapeDtypeStruct(wkv_t.shape, F32)],
        scratch_shapes=[pltpu.VMEM((6, RW, n), F32), pltpu.VMEM((4, n, RW), F32),
                        pltpu.VMEM((RW, n), F32)],
        compiler_params=_cparams(1),
        name="mixer_sample",
    )(h, x1, yb, shift_in, wkv_t, *wlist)


def _chunk_constants():
    t = jnp.arange(CHUNK)
    incl = (t[:, None] >= t[None, :])
    strict = (t[:, None] > t[None, :])
    tri = incl.astype(BF16)
    top = jnp.concatenate([strict, strict], axis=1)
    bot = jnp.concatenate([incl, incl], axis=1)
    mask = jnp.concatenate([top, bot], axis=0).astype(F32)
    return tri, mask


def kernel(x_prompt, x_sample, state_shift, state_wkv, state_s5_re, state_s5_im, g_ffn1, ffn1_gate, ffn1_up, ffn1_down, g_mix, w_in, mu_shift, w0, w_up, a0, a_up, g_up, k_k, k_a, r_k, lnx_w, lnx_b, A_re, A_im, log_dt, B_re, B_im, C_re, C_im, D_skip, w_glu, w_a_up, w_b_up, w_out, g_ffn2, ffn2_gate, ffn2_up, ffn2_down, g_final):
    depth = g_ffn1.shape[0]
    assert depth == 1
    bp, seq, d = x_prompt.shape
    bs = x_sample.shape[0]
    assert x_sample.shape[1] == 1
    ncols_f = mu_shift.shape[1]
    su = D_skip.shape[1]
    groups, pstate = A_re.shape[1:]
    nstate = groups * pstate
    bf = lambda a: a.astype(BF16)
    row = lambda a: a.reshape(1, -1).astype(F32)

    w_in0 = w_in[0]
    eye_h = jnp.eye(HEADS, dtype=F32)
    wts = {
        'w_f': bf(w_in0[:, :ncols_f]),
        'w_g': bf(w_in0[:, ncols_f + su:]),
        'mu': row(mu_shift[0]), 'w0': row(w0[0]), 'w_up': bf(w_up[0]), 'a0': row(a0[0]),
        'a_up': bf(a_up[0]), 'g_up': bf(g_up[0]), 'k_k': row(k_k[0]), 'k_a': row(k_a[0]),
        'r_k': row(r_k[0]), 'lnx_w': row(lnx_w[0]), 'lnx_b': row(lnx_b[0]),
        'e': bf(jnp.kron(eye_h, jnp.ones((HEAD, HEAD), F32))),
        'w_a_up': bf(w_a_up[0]), 'w_b_up': bf(w_b_up[0]), 'w_out': bf(w_out[0]),
    }
    w_u = bf(w_in0[:, ncols_f:ncols_f + su])
    abr, abi, bbr, bbi = _s5_params(A_re[0], A_im[0], log_dt[0],
                                    jnp.swapaxes(B_re[0], 1, 2), jnp.swapaxes(B_im[0], 1, 2))
    bb = bf(jnp.concatenate([_block_diag(bbr, S5_BLOCKS), _block_diag(bbi, S5_BLOCKS)], axis=2))
    cm = bf(jnp.concatenate([_block_diag(jnp.swapaxes(C_re[0], 1, 2), S5_BLOCKS),
                             _block_diag(-jnp.swapaxes(C_im[0], 1, 2), S5_BLOCKS)], axis=1))
    abr = abr.reshape(1, nstate)
    abi = abi.reshape(1, nstate)
    dsk = row(D_skip[0])
    wglu = bf(w_glu[0])
    ffn1 = (row(g_ffn1[0]), bf(ffn1_gate[0]), bf(ffn1_up[0]), bf(ffn1_down[0]), row(g_mix[0]))
    ffn2 = (row(g_ffn2[0]), bf(ffn2_gate[0]), bf(ffn2_up[0]), bf(ffn2_down[0]), row(g_final))
    fc = 4 * MXU_TILE_V7X
    tri, mask = _chunk_constants()

    xp = x_prompt.reshape(bp * seq, d)
    x1p, hnp = _ffn(xp, *ffn1, final=False, tm=1024, fc=fc)
    hnp = hnp.reshape(bp, seq, d)
    zeros_state = jnp.zeros((bp, nstate), F32)
    yb_p, s5r_p, s5i_p = _s5(hnp, w_u, bb, cm, abr, abi, dsk, wglu, zeros_state, zeros_state,
                             nb=bp, steps=64)
    x2p, shift_p, wkv_p = _mixer_prompt(hnp, x1p.reshape(bp, seq, d), yb_p, wts, tri, mask, tc=512)
    y_prompt = _ffn(x2p.reshape(bp * seq, d), *ffn2, final=True, tm=1024, fc=fc).reshape(bp, seq, d)

    xs = x_sample.reshape(bs, d)
    x1s, hns = _ffn(xs, *ffn1, final=False, tm=bs, fc=fc)
    yb_s, s5r_s, s5i_s = _s5(hns[None], w_u, bb, cm, abr, abi, dsk, wglu,
                             state_s5_re[0].reshape(bs, nstate), state_s5_im[0].reshape(bs, nstate),
                             nb=bs, steps=1)
    wkv_t = jnp.transpose(state_wkv[0], (1, 2, 3, 0))
    x2s, shift_s, wkv_ts = _mixer_sample(hns, x1s, yb_s[0], state_shift[0], wkv_t, wts)
    wkv_s = jnp.transpose(wkv_ts, (3, 0, 1, 2))
    y_sample = _ffn(x2s, *ffn2, final=True, tm=bs, fc=fc).reshape(bs, 1, d)

    st5 = lambda a, n: a.reshape(1, n, groups, pstate)
    return (y_prompt, y_sample,
            shift_p.reshape(1, bp, ncols_f), wkv_p[None], st5(s5r_p, bp), st5(s5i_p, bp),
            shift_s[None], wkv_s[None], st5(s5r_s, bs), st5(s5i_s, bs))
```

```python
import functools

import jax
import jax.numpy as jnp
from jax import lax
from jax.experimental import pallas as pl
from jax.experimental.pallas import tpu as pltpu

F32 = jnp.float32
BF16 = jnp.bfloat16

NORM_EPS = 1e-6
LNX_EPS = 64e-5
HEAD = 64
HEADS = 8
PAIRS = HEADS // 2
RW = HEAD * HEADS
S5_BLOCKS = 4
CHUNK = 64
CHUNKS_PER_GROUP = 8
VMEM_LIMIT = 56 * 1024 * 1024
MXU_TILE_V7X = 256


def _dot(a, b):
    return jnp.dot(a, b, preferred_element_type=F32)


def _dot_nt(a, b):
    return lax.dot_general(a, b, (((1,), (1,)), ((), ())), preferred_element_type=F32)


def _rms(x, g):
    ms = jnp.mean(x * x, axis=-1, keepdims=True)
    return x * lax.rsqrt(ms + NORM_EPS) * g


def _const_spec(shape):
    nd = len(shape)
    return pl.BlockSpec(shape, lambda *_: (0,) * nd, pipeline_mode=pl.Buffered(1))


def _cparams(ngrid):
    return pltpu.CompilerParams(dimension_semantics=("arbitrary",) * ngrid,
                                vmem_limit_bytes=VMEM_LIMIT)


def _ffn_body(x_ref, g_ref, wg_ref, wu_ref, wd_ref, gn_ref, *outs, final, fc):
    x = x_ref[...]
    hn = _rms(x, g_ref[...]).astype(BF16)
    acc = None
    f = wg_ref.shape[1]
    for c0 in range(0, f, fc):
        c1 = min(c0 + fc, f)
        gate = _dot(hn, wg_ref[:, c0:c1])
        up = _dot(hn, wu_ref[:, c0:c1])
        act = (jax.nn.silu(gate) * up).astype(BF16)
        part = _dot(act, wd_ref[c0:c1, :])
        acc = part if acc is None else acc + part
    x1 = x + 0.5 * acc
    if final:
        outs[0][...] = _rms(x1, gn_ref[...])
    else:
        outs[0][...] = x1
        outs[1][...] = _rms(x1, gn_ref[...]).astype(BF16)


def _ffn(x, g, wg, wu, wd, gn, *, final, tm, fc):
    m, d = x.shape
    assert m % tm == 0 and fc % MXU_TILE_V7X == 0
    tile = pl.BlockSpec((tm, d), lambda i: (i, 0))
    if final:
        out_shape = jax.ShapeDtypeStruct((m, d), F32)
        out_specs = tile
    else:
        out_shape = [jax.ShapeDtypeStruct((m, d), F32), jax.ShapeDtypeStruct((m, d), BF16)]
        out_specs = [tile, tile]
    return pl.pallas_call(
        functools.partial(_ffn_body, final=final, fc=fc),
        grid=(m // tm,),
        in_specs=[tile, _const_spec(g.shape), _const_spec(wg.shape), _const_spec(wu.shape),
                  _const_spec(wd.shape), _const_spec(gn.shape)],
        out_specs=out_specs,
        out_shape=out_shape,
        compiler_params=_cparams(1),
        name="ffn_final" if final else "ffn_in",
    )(x, g, wg, wu, wd, gn)


def _s5_param_body(are_ref, aim_ref, ldt_ref, bre_ref, bim_ref, abr_ref, abi_ref, bbr_ref, bbi_ref):
    lam_r = are_ref[...]
    lam_i = aim_ref[...]
    dt = jnp.exp(ldt_ref[...])
    mag = jnp.exp(lam_r * dt)
    ab_r = mag * jnp.cos(lam_i * dt)
    ab_i = mag * jnp.sin(lam_i * dt)
    den = lam_r * lam_r + lam_i * lam_i
    q_r = ((ab_r - 1.0) * lam_r + ab_i * lam_i) / den
    q_i = (ab_i * lam_r - (ab_r - 1.0) * lam_i) / den
    abr_ref[...] = ab_r
    abi_ref[...] = ab_i
    bre = bre_ref[...]
    bim = bim_ref[...]
    bbr_ref[...] = q_r[:, None, :] * bre - q_i[:, None, :] * bim
    bbi_ref[...] = q_r[:, None, :] * bim + q_i[:, None, :] * bre


def _s5_params(a_re, a_im, log_dt, b_re_t, b_im_t):
    g, p = a_re.shape
    hg = b_re_t.shape[1]
    return pl.pallas_call(
        _s5_param_body,
        out_shape=[jax.ShapeDtypeStruct((g, p), F32), jax.ShapeDtypeStruct((g, p), F32),
                   jax.ShapeDtypeStruct((g, hg, p), F32), jax.ShapeDtypeStruct((g, hg, p), F32)],
        name="s5_params",
    )(a_re, a_im, log_dt.reshape(g, 1), b_re_t, b_im_t)


def _block_diag(w, nblk):
    g, a, b = w.shape
    gb = g // nblk
    w4 = w.reshape(nblk, gb, a, b)
    eye = jnp.eye(gb, dtype=w.dtype)
    return jnp.einsum('jgab,gk->jgakb', w4, eye).reshape(nblk, gb * a, gb * b)


def _s5_body(h_ref, perm_ref, wu_ref, bb_ref, cm_ref, abr_ref, abi_ref, dsk_ref, wglu_ref, x0r_ref,
             x0i_ref, yb_ref, xr_out, xi_out, bu_scr, xr_scr, xi_scr, *, nb, steps, sub):
    c = pl.program_id(0)
    rows = nb * steps
    blk = h_ref.shape[1] // sub

    @pl.when(c == 0)
    def _():
        xr_scr[...] = x0r_ref[...]
        xi_scr[...] = x0i_ref[...]

    nin = wu_ref.shape[1] // S5_BLOCKS
    ns = abr_ref.shape[1] // S5_BLOCKS
    us = []
    for i in range(sub):
        h = h_ref[:, i * blk:(i + 1) * blk, :].reshape(rows, h_ref.shape[-1])
        if steps > 1:
            h = _dot(perm_ref[0], h).astype(BF16)
        u = _dot(h, wu_ref[...])
        us.append(u)
        ub = u.astype(BF16)
        for j in range(S5_BLOCKS):
            bu_scr[i, j] = _dot(ub[:, j * nin:(j + 1) * nin], bb_ref[j])
    xr = [xr_scr[:, j * ns:(j + 1) * ns] for j in range(S5_BLOCKS)]
    xi = [xi_scr[:, j * ns:(j + 1) * ns] for j in range(S5_BLOCKS)]
    for i in range(sub):
        ys = []
        for j in range(S5_BLOCKS):
            ar = jnp.broadcast_to(abr_ref[:, j * ns:(j + 1) * ns], (nb, ns))
            ai = jnp.broadcast_to(abi_ref[:, j * ns:(j + 1) * ns], (nb, ns))
            for t in range(steps):
                tr = slice(t * nb, (t + 1) * nb)
                xr[j], xi[j] = (ar * xr[j] - ai * xi[j] + bu_scr[i, j, tr, 0:ns],
                                ar * xi[j] + ai * xr[j] + bu_scr[i, j, tr, ns:2 * ns])
                bu_scr[i, j, tr, 0:ns] = xr[j]
                bu_scr[i, j, tr, ns:2 * ns] = xi[j]
            ys.append(_dot(bu_scr[i, j].astype(BF16), cm_ref[j]))
        y = jnp.concatenate(ys, axis=1) + dsk_ref[...] * us[i]
        z = jax.nn.gelu(y)
        yb = (z * jax.nn.sigmoid(_dot(z.astype(BF16), wglu_ref[...]))).astype(BF16)
        if steps > 1:
            yb = _dot(perm_ref[1], yb).astype(BF16)
        yb_ref[:, i * blk:(i + 1) * blk, :] = yb.reshape(yb_ref.shape[0], blk, yb_ref.shape[2])
    for j in range(S5_BLOCKS):
        xr_scr[:, j * ns:(j + 1) * ns] = xr[j]
        xi_scr[:, j * ns:(j + 1) * ns] = xi[j]
    xr_out[...] = xr_scr[...]
    xi_out[...] = xi_scr[...]


def _s5(h, w_u, bb, cm, abr, abi, dsk, wglu, x0r, x0i, *, nb, steps, sub=1):
    g, seq, d = h.shape
    rows = nb * steps
    blk = sub * rows // g
    assert seq % blk == 0 and (g == nb or steps == 1)
    su = w_u.shape[1]
    nst = abr.shape[1]
    r = jnp.arange(rows)
    to_tm = (r[:, None] % nb) * steps + r[:, None] // nb == r[None, :]
    perm = jnp.stack([to_tm, to_tm.T]).astype(BF16)
    return pl.pallas_call(
        functools.partial(_s5_body, nb=nb, steps=steps, sub=sub),
        grid=(seq // blk,),
        in_specs=[
            pl.BlockSpec((g, blk, d), lambda c: (0, c, 0)),
            _const_spec(perm.shape),
            _const_spec(w_u.shape), _const_spec(bb.shape), _const_spec(cm.shape),
            _const_spec(abr.shape), _const_spec(abi.shape), _const_spec(dsk.shape),
            _const_spec(wglu.shape), _const_spec(x0r.shape), _const_spec(x0i.shape),
        ],
        out_specs=[pl.BlockSpec((g, blk, su), lambda c: (0, c, 0)),
                   pl.BlockSpec((nb, nst), lambda c: (0, 0)),
                   pl.BlockSpec((nb, nst), lambda c: (0, 0))],
        out_shape=[jax.ShapeDtypeStruct((g, seq, su), BF16),
                   jax.ShapeDtypeStruct((nb, nst), F32),
                   jax.ShapeDtypeStruct((nb, nst), F32)],
        scratch_shapes=[pltpu.VMEM((sub, S5_BLOCKS, rows, 2 * nst // S5_BLOCKS), F32),
                        pltpu.VMEM((nb, nst), F32), pltpu.VMEM((nb, nst), F32)],
        compiler_params=_cparams(1),
        name="s5_branch",
    )(h, perm, w_u, bb, cm, abr, abi, dsk, wglu, x0r, x0i)


def _headsum(x, e):
    xb = x.astype(BF16)
    t = MXU_TILE_V7X
    return jnp.concatenate([_dot(xb[:, c:c + t], e[c:c + t, c:c + t]) for c in range(0, RW, t)], axis=1)


def _split3(x):
    hi = x.astype(BF16)
    r1 = x - hi.astype(F32)
    mid = r1.astype(BF16)
    lo = (r1 - mid.astype(F32)).astype(BF16)
    return hi, mid, lo


def _rwkv_pre(fm, w):
    r = fm[:, 0:RW]
    k = fm[:, RW:2 * RW]
    v = fm[:, 2 * RW:3 * RW]
    o0 = 3 * RW
    nw = w['w_up'].shape[0]
    na = w['a_up'].shape[0]
    wd = fm[:, o0:o0 + nw]
    ad = fm[:, o0 + nw:o0 + nw + na]
    gd = fm[:, o0 + nw + na:]
    w_log = -jax.nn.softplus(-(w['w0'][...] + _dot(jnp.tanh(wd).astype(BF16), w['w_up'][...]))) - 0.5
    logw = -jnp.exp(w_log)
    a = jax.nn.sigmoid(w['a0'][...] + _dot(ad.astype(BF16), w['a_up'][...]))
    g = _dot(jax.nn.sigmoid(gd).astype(BF16), w['g_up'][...])
    kk = k * w['k_k'][...]
    kk = kk * lax.rsqrt(jnp.maximum(_headsum(kk * kk, w['e']), 1e-24))
    k2 = k * (1.0 + (a - 1.0) * w['k_a'][...])
    return r, logw, k2, v, kk, kk * a, g


def _rwkv_post(o, r, k2, v, g, w):
    inv_n = 1.0 / HEAD
    mu = _headsum(o, w['e']) * inv_n
    d = o - mu
    var = _headsum(d * d, w['e']) * inv_n
    on = d * lax.rsqrt(var + LNX_EPS) * w['lnx_w'][...] + w['lnx_b'][...]
    bonus = _headsum(r * k2 * w['r_k'][...], w['e']) * v
    return (on + bonus) * g


def _sigmoid(x):
    return 0.5 * jnp.tanh(0.5 * x) + 0.5


def _merge_gates(h, y_b, w, c0=0, c1=None):
    dm = w['w_g'].shape[1] // 2
    c1 = dm if c1 is None else c1
    gate_a = _sigmoid(_dot(h, w['w_g'][:, c0:c1]))
    gate_b = _sigmoid(_dot(h, w['w_g'][:, dm + c0:dm + c1]))
    return gate_a, gate_b * _dot(y_b.astype(BF16), w['w_b_up'][:, c0:c1])


def _merge_out(x1, y_a, gate_a, gated_b, w):
    merged = gate_a * _dot(y_a.astype(BF16), w['w_a_up'][...]) + gated_b
    return x1 + _dot(merged.astype(BF16), w['w_out'][...])


_W_NAMES = ('w_f', 'w_g', 'mu', 'w0', 'w_up', 'a0', 'a_up', 'g_up', 'k_k', 'k_a', 'r_k',
            'lnx_w', 'lnx_b', 'e', 'w_a_up', 'w_b_up', 'w_out')


def _mixer_prompt_body(*refs, tc):
    h_ref, x1_ref, yb_ref = refs[0:3]
    nwt = len(_W_NAMES)
    w = dict(zip(_W_NAMES, refs[3:3 + nwt]))
    tri_ref, mask_ref = refs[3 + nwt:5 + nwt]
    x2_ref, shift_ref, wkv_ref = refs[5 + nwt:8 + nwt]
    f_scr, st_scr, o_scr, ga_scr, gb_scr = refs[8 + nwt:]
    c = pl.program_id(1)

    @pl.when(c == 0)
    def _():
        f_scr[0:8, :] = jnp.zeros((8, f_scr.shape[1]), F32)
        st_scr[...] = jnp.zeros_like(st_scr)

    @pl.when(c > 0)
    def _():
        f_scr[7:8, :] = f_scr[tc + 7:tc + 8, :]

    h = h_ref[...]
    f = _dot(h, w['w_f'][...])
    f_scr[8:8 + tc, :] = f
    shifted = f_scr[7:7 + tc, :]
    fm = f + w['mu'][...] * (shifted - f)
    r, logw, k2, v, kk, b, g = _rwkv_pre(fm, w)

    ns = tc // CHUNK
    dm = x1_ref.shape[1]
    yb = yb_ref[...]
    tri = tri_ref[...]
    mask = mask_ref[...] > 0.5
    pw = 2 * HEAD
    lane = lambda rows: lax.broadcasted_iota(jnp.int32, (rows, pw), 1)
    low = lambda a: lane(a.shape[0]) < HEAD
    eye2 = (lax.broadcasted_iota(jnp.int32, (CHUNK, pw), 0) == lane(CHUNK) % HEAD).astype(F32)
    n_double = CHUNK.bit_length() - 2
    pairs = range(PAIRS)
    first = lambda a: jnp.where(low(a), a, jnp.zeros_like(a))
    second = lambda a: jnp.where(low(a), jnp.zeros_like(a), a)
    bd = lambda y: jnp.concatenate([first(y), second(y)], axis=0)
    ad = lambda y: jnp.concatenate([second(y), first(y)], axis=0)
    pick = lambda x, y: jnp.where(low(x), x, y)
    stack = lambda top, bot: jnp.concatenate([top.astype(BF16), bot.astype(BF16)], axis=0)
    bf = lambda a: a.astype(BF16)
    zc = jnp.zeros((CHUNK, pw), BF16)

    def chunk_setup(s):
        sl = slice(s * CHUNK, (s + 1) * CHUNK)
        lw = logw[sl]
        hi, mid, lo = _split3(lw)
        lpre = _dot(tri, hi) + _dot(tri, mid) + _dot(tri, lo)
        lsuf = lpre[CHUNK - 1:CHUNK, :] - lpre
        e_pre = jnp.exp(lpre)
        e_neg = jnp.exp(-lpre)
        e_suf = jnp.exp(lsuf)
        lhs = jnp.concatenate([kk[sl] * jnp.exp(lpre - lw), r[sl] * e_pre], axis=0).astype(BF16)
        rhs = jnp.concatenate([b[sl] * e_neg, k2[sl] * e_neg], axis=0).astype(BF16)
        sfx_t = jnp.concatenate([b[sl] * e_suf, k2[sl] * e_suf], axis=0).T.astype(BF16)
        every = ns * MXU_TILE_V7X // dm
        if s % every == 0:
            c0 = s // every * MXU_TILE_V7X
            ga_scr[:, c0:c0 + MXU_TILE_V7X], gb_scr[:, c0:c0 + MXU_TILE_V7X] = _merge_gates(
                h, yb, w, c0, c0 + MXU_TILE_V7X)
        return lhs, rhs, sfx_t, v[sl].astype(BF16), e_pre[CHUNK - 1:CHUNK, :]

    def chunk_units(setups):
        units = [(s, pr) for s in setups for pr in pairs]
        cut = lambda a, pr: a[:, pr * pw:(pr + 1) * pw]
        lhs = {u: cut(setups[u[0]][0], u[1]) for u in units}
        rhs = {u: cut(setups[u[0]][1], u[1]) for u in units}
        a0 = {u: jnp.where(mask, _dot_nt(first(lhs[u]), rhs[u]), 0.0) for u in units}
        a1 = {u: jnp.where(mask, _dot_nt(second(lhs[u]),
                                         jnp.concatenate([rhs[u][CHUNK:], rhs[u][:CHUNK]], axis=0)), 0.0)
              for u in units}
        m_b = {u: pick(a0[u][:CHUNK], a1[u][:CHUNK]) for u in units}
        m_ks = {u: pick(a1[u][:CHUNK], a0[u][:CHUNK]) for u in units}
        n_b = {u: pick(a0[u][CHUNK:], a1[u][CHUNK:]) for u in units}
        n_ks = {u: pick(a1[u][CHUNK:], a0[u][CHUNK:]) for u in units}
        p = {u: -m_b[u] for u in units}
        winv = {u: eye2 + p[u] for u in units}
        p = {u: _dot(bf(p[u]), bd(bf(p[u]))) for u in units}
        for _ in range(n_double - 1):
            wp = {u: _dot(stack(winv[u], p[u]), bd(bf(p[u]))) for u in units}
            winv = {u: winv[u] + wp[u][:CHUNK] for u in units}
            p = {u: wp[u][CHUNK:] for u in units}
        wb = {u: bf(winv[u] + _dot(bf(winv[u]), bd(bf(p[u])))) for u in units}
        adv = {u: ad(cut(setups[u[0]][3], u[1])) for u in units}
        mkv = {u: bf(_dot(bf(m_ks[u]), adv[u])) for u in units}
        gu = {u: _dot(wb[u], jnp.concatenate([bd(lhs[u][:CHUNK]), bd(mkv[u])], axis=1)) for u in units}
        bdg = {u: bd(bf(gu[u][:, :pw])) for u in units}
        bdu = {u: bd(bf(-gu[u][:, pw:])) for u in units}
        q = {u: lhs[u][CHUNK:].astype(F32) - _dot(bf(n_b[u]), bdg[u]) for u in units}
        o0 = {u: _dot(jnp.concatenate([bf(n_b[u]), bf(n_ks[u])], axis=1),
                      jnp.concatenate([bdu[u], adv[u]], axis=0)) for u in units}
        sfx_t = {u: jnp.concatenate([setups[u[0]][2][(2 * u[1] + i) * HEAD:(2 * u[1] + i + 1) * HEAD, :]
                                     for i in range(2)], axis=1)
                 for u in units}
        gv = {u: jnp.concatenate(
            [jnp.concatenate([bdg[u][:CHUNK], bdu[u][:CHUNK]], axis=1),
             jnp.concatenate([zc, adv[u][CHUNK:]], axis=1),
             jnp.concatenate([bdg[u][CHUNK:], bdu[u][CHUNK:]], axis=1),
             jnp.concatenate([zc, adv[u][:CHUNK]], axis=1)], axis=0) for u in units}
        sg = {u: _dot(sfx_t[u], gv[u]) for u in units}
        qphi = {u: stack(q[u], eye2 * cut(setups[u[0]][4], u[1]) - sg[u][:, :pw]) for u in units}
        opsi = {u: jnp.concatenate([o0[u], sg[u][:, pw:]], axis=0) for u in units}
        return qphi, opsi

    qphi, opsi = {}, {}
    for s0 in range(0, ns, CHUNKS_PER_GROUP):
        qp, op = chunk_units({s: chunk_setup(s) for s in range(s0, s0 + CHUNKS_PER_GROUP)})
        qphi.update(qp)
        opsi.update(op)
    st = [st_scr[pr] for pr in pairs]
    for s in range(ns):
        os = [_dot(qphi[s, pr], bd(bf(st[pr]))) + opsi[s, pr] for pr in pairs]
        for pr in pairs:
            o_scr[s * CHUNK:(s + 1) * CHUNK, pr * pw:(pr + 1) * pw] = os[pr][:CHUNK]
        st = [os[pr][CHUNK:] for pr in pairs]
    for pr in pairs:
        st_scr[pr] = st[pr]

    y_a = _rwkv_post(o_scr[...], r, k2, v, g, w)
    x2_ref[...] = _merge_out(x1_ref[...], y_a, ga_scr[...], gb_scr[...], w)

    @pl.when(c == pl.num_programs(1) - 1)
    def _():
        shift_ref[...] = f[tc - 1:tc, :]
        for hd in range(HEADS):
            wkv_ref[hd] = st_scr[hd // 2][:, (hd % 2) * HEAD:(hd % 2 + 1) * HEAD].T


def _mixer_prompt(h, x1, yb, wts, tri, mask, *, tc):
    bsz, seq, d = h.shape
    assert seq % tc == 0 and tc % (CHUNK * CHUNKS_PER_GROUP) == 0
    nf = wts['w_f'].shape[1]
    wlist = [wts[n] for n in _W_NAMES]
    tok = lambda width: pl.BlockSpec((None, tc, width), lambda b, c: (b, c, 0))
    return pl.pallas_call(
        functools.partial(_mixer_prompt_body, tc=tc),
        grid=(bsz, seq // tc),
        in_specs=[tok(d), tok(d), tok(RW)] + [_const_spec(a.shape) for a in wlist]
                 + [_const_spec(tri.shape), _const_spec(mask.shape)],
        out_specs=[tok(d),
                   pl.BlockSpec((None, 1, nf), lambda b, c: (b, 0, 0)),
                   pl.BlockSpec((None, HEADS, HEAD, HEAD), lambda b, c: (b, 0, 0, 0))],
        out_shape=[jax.ShapeDtypeStruct((bsz, seq, d), F32),
                   jax.ShapeDtypeStruct((bsz, 1, nf), F32),
                   jax.ShapeDtypeStruct((bsz, HEADS, HEAD, HEAD), F32)],
        scratch_shapes=[pltpu.VMEM((tc + 8, nf), F32),
                        pltpu.VMEM((PAIRS, HEAD, 2 * HEAD), F32),
                        pltpu.VMEM((tc, RW), F32),
                        pltpu.VMEM((tc, d), F32), pltpu.VMEM((tc, d), F32)],
        compiler_params=_cparams(2),
        name="mixer_prompt",
    )(h, x1, yb, *wlist, tri, mask)


def _mixer_sample_body(*refs):
    h_ref, x1_ref, yb_ref, shift_in_ref, s_ref = refs[0:5]
    nwt = len(_W_NAMES)
    w = dict(zip(_W_NAMES, refs[5:5 + nwt]))
    x2_ref, shift_ref, s_out_ref = refs[5 + nwt:8 + nwt]
    t_scr, row_scr, ot_scr = refs[8 + nwt:]
    hd = pl.program_id(0)

    @pl.when(hd == 0)
    def _():
        f = _dot(h_ref[...], w['w_f'][...])
        fm = f + w['mu'][...] * (shift_in_ref[...] - f)
        shift_ref[...] = f
        r, logw, k2, v, kk, b, g = _rwkv_pre(fm, w)
        for i, a in enumerate((kk, jnp.exp(logw), b, k2, r, v)):
            t_scr[i] = a.T
        for i, a in enumerate((r, k2, v, g)):
            row_scr[i] = a

    off = pl.multiple_of(hd * HEAD, HEAD)
    hs = pl.ds(off, HEAD)
    kk_t, dec_t, b_t, k_t, r_t = (t_scr[i, hs, :] for i in range(5))

    def v_group(vg, carry):
        v0 = pl.multiple_of(vg * 8, 8)
        s8 = s_ref[pl.ds(v0, 8)]
        v8 = t_scr[5, pl.ds(off + v0, 8), :]
        o_rows = []
        for j in range(8):
            sa = -jnp.sum(s8[j] * kk_t, axis=0, keepdims=True)
            sn = s8[j] * dec_t + sa * b_t + v8[j:j + 1, :] * k_t
            s_out_ref[v0 + j] = sn
            o_rows.append(jnp.sum(sn * r_t, axis=0, keepdims=True))
        ot_scr[pl.ds(off + v0, 8), :] = jnp.concatenate(o_rows, axis=0)
        return carry

    lax.fori_loop(0, HEAD // 8, v_group, 0)

    @pl.when(hd == pl.num_programs(0) - 1)
    def _():
        r, k2, v, g = (row_scr[i] for i in range(4))
        y_a = _rwkv_post(ot_scr[...].T, r, k2, v, g, w)
        gate_a, gated_b = _merge_gates(h_ref[...], yb_ref[...], w)
        x2_ref[...] = _merge_out(x1_ref[...], y_a, gate_a, gated_b, w)


def _mixer_sample(h, x1, yb, shift_in, wkv_t, wts):
    n, d = h.shape
    nf = wts['w_f'].shape[1]
    wlist = [wts[nm] for nm in _W_NAMES]
    st_spec = pl.BlockSpec((None, HEAD, HEAD, n), lambda hd: (hd, 0, 0, 0))
    return pl.pallas_call(
        _mixer_sample_body,
        grid=(HEADS,),
        in_specs=[_const_spec(a.shape) for a in (h, x1, yb, shift_in)] + [st_spec]
                 + [_const_spec(a.shape) for a in wlist],
        out_specs=[pl.BlockSpec((n, d), lambda hd: (0, 0)),
                   pl.BlockSpec((n, nf), lambda hd: (0, 0)), st_spec],
        out_shape=[jax.ShapeDtypeStruct((n, d), F32),
                   jax.ShapeDtypeStruct((n, nf), F32),
                   jax.ShapeDtypeStruct(wkv_t.shape, F32)],
        scratch_shapes=[pltpu.VMEM((6, RW, n), F32), pltpu.VMEM((4, n, RW), F32),
                        pltpu.VMEM((RW, n), F32)],
        compiler_params=_cparams(1),
        name="mixer_sample",
    )(h, x1, yb, shift_in, wkv_t, *wlist)


def _chunk_constants():
    t = jnp.arange(CHUNK)
    incl = (t[:, None] >= t[None, :])
    strict = (t[:, None] > t[None, :])
    tri = incl.astype(BF16)
    top = jnp.concatenate([strict, strict], axis=1)
    bot = jnp.concatenate([incl, incl], axis=1)
    mask = jnp.concatenate([top, bot], axis=0).astype(F32)
    return tri, mask


def kernel(x_prompt, x_sample, state_shift, state_wkv, state_s5_re, state_s5_im, g_ffn1, ffn1_gate, ffn1_up, ffn1_down, g_mix, w_in, mu_shift, w0, w_up, a0, a_up, g_up, k_k, k_a, r_k, lnx_w, lnx_b, A_re, A_im, log_dt, B_re, B_im, C_re, C_im, D_skip, w_glu, w_a_up, w_b_up, w_out, g_ffn2, ffn2_gate, ffn2_up, ffn2_down, g_final):
    depth = g_ffn1.shape[0]
    assert depth == 1
    bp, seq, d = x_prompt.shape
    bs = x_sample.shape[0]
    assert x_sample.shape[1] == 1
    ncols_f = mu_shift.shape[1]
    su = D_skip.shape[1]
    groups, pstate = A_re.shape[1:]
    nstate = groups * pstate
    bf = lambda a: a.astype(BF16)
    row = lambda a: a.reshape(1, -1).astype(F32)

    w_in0 = w_in[0]
    eye_h = jnp.eye(HEADS, dtype=F32)
    wts = {
        'w_f': bf(w_in0[:, :ncols_f]),
        'w_g': bf(w_in0[:, ncols_f + su:]),
        'mu': row(mu_shift[0]), 'w0': row(w0[0]), 'w_up': bf(w_up[0]), 'a0': row(a0[0]),
        'a_up': bf(a_up[0]), 'g_up': bf(g_up[0]), 'k_k': row(k_k[0]), 'k_a': row(k_a[0]),
        'r_k': row(r_k[0]), 'lnx_w': row(lnx_w[0]), 'lnx_b': row(lnx_b[0]),
        'e': bf(jnp.kron(eye_h, jnp.ones((HEAD, HEAD), F32))),
        'w_a_up': bf(w_a_up[0]), 'w_b_up': bf(w_b_up[0]), 'w_out': bf(w_out[0]),
    }
    w_u = bf(w_in0[:, ncols_f:ncols_f + su])
    abr, abi, bbr, bbi = _s5_params(A_re[0], A_im[0], log_dt[0],
                                    jnp.swapaxes(B_re[0], 1, 2), jnp.swapaxes(B_im[0], 1, 2))
    bb = bf(jnp.concatenate([_block_diag(bbr, S5_BLOCKS), _block_diag(bbi, S5_BLOCKS)], axis=2))
    cm = bf(jnp.concatenate([_block_diag(jnp.swapaxes(C_re[0], 1, 2), S5_BLOCKS),
                             _block_diag(-jnp.swapaxes(C_im[0], 1, 2), S5_BLOCKS)], axis=1))
    abr = abr.reshape(1, nstate)
    abi = abi.reshape(1, nstate)
    dsk = row(D_skip[0])
    wglu = bf(w_glu[0])
    ffn1 = (row(g_ffn1[0]), bf(ffn1_gate[0]), bf(ffn1_up[0]), bf(ffn1_down[0]), row(g_mix[0]))
    ffn2 = (row(g_ffn2[0]), bf(ffn2_gate[0]), bf(ffn2_up[0]), bf(ffn2_down[0]), row(g_final))
    fc = 4 * MXU_TILE_V7X
    tri, mask = _chunk_constants()

    xp = x_prompt.reshape(bp * seq, d)
    x1p, hnp = _ffn(xp, *ffn1, final=False, tm=1024, fc=fc)
    hnp = hnp.reshape(bp, seq, d)
    zeros_state = jnp.zeros((bp, nstate), F32)
    yb_p, s5r_p, s5i_p = _s5(hnp, w_u, bb, cm, abr, abi, dsk, wglu, zeros_state, zeros_state,
                             nb=bp, steps=64, sub=2)
    x2p, shift_p, wkv_p = _mixer_prompt(hnp, x1p.reshape(bp, seq, d), yb_p, wts, tri, mask, tc=512)
    y_prompt = _ffn(x2p.reshape(bp * seq, d), *ffn2, final=True, tm=1024, fc=fc).reshape(bp, seq, d)

    xs = x_sample.reshape(bs, d)
    x1s, hns = _ffn(xs, *ffn1, final=False, tm=bs, fc=fc)
    yb_s, s5r_s, s5i_s = _s5(hns[None], w_u, bb, cm, abr, abi, dsk, wglu,
                             state_s5_re[0].reshape(bs, nstate), state_s5_im[0].reshape(bs, nstate),
                             nb=bs, steps=1)
    wkv_t = jnp.transpose(state_wkv[0], (1, 2, 3, 0))
    x2s, shift_s, wkv_ts = _mixer_sample(hns, x1s, yb_s[0], state_shift[0], wkv_t, wts)
    wkv_s = jnp.transpose(wkv_ts, (3, 0, 1, 2))
    y_sample = _ffn(x2s, *ffn2, final=True, tm=bs, fc=fc).reshape(bs, 1, d)

    st5 = lambda a, n: a.reshape(1, n, groups, pstate)
    return (y_prompt, y_sample,
            shift_p.reshape(1, bp, ncols_f), wkv_p[None], st5(s5r_p, bp), st5(s5i_p, bp),
            shift_s[None], wkv_s[None], st5(s5r_s, bs), st5(s5i_s, bs))
```

```python
import functools

import jax
import jax.numpy as jnp
from jax import lax
from jax.experimental import pallas as pl
from jax.experimental.pallas import tpu as pltpu

F32 = jnp.float32
BF16 = jnp.bfloat16

NORM_EPS = 1e-6
LNX_EPS = 64e-5
HEAD = 64
HEADS = 8
PAIRS = HEADS // 2
RW = HEAD * HEADS
S5_BLOCKS = 4
CHUNK = 64
VMEM_LIMIT = 56 * 1024 * 1024
MXU_TILE_V7X = 256


def _dot(a, b):
    return jnp.dot(a, b, preferred_element_type=F32)


def _dot_nt(a, b):
    return lax.dot_general(a, b, (((1,), (1,)), ((), ())), preferred_element_type=F32)


def _rms(x, g):
    ms = jnp.mean(x * x, axis=-1, keepdims=True)
    return x * lax.rsqrt(ms + NORM_EPS) * g


def _const_spec(shape):
    nd = len(shape)
    return pl.BlockSpec(shape, lambda *_: (0,) * nd, pipeline_mode=pl.Buffered(1))


def _cparams(ngrid):
    return pltpu.CompilerParams(dimension_semantics=("arbitrary",) * ngrid,
                                vmem_limit_bytes=VMEM_LIMIT)


def _ffn_body(x_ref, xs_ref, g_ref, wg_ref, wu_ref, wd_ref, gn_ref, *outs, final, fc):
    nout = len(outs) // 2

    def run(src_ref, dst):
        x = src_ref[...]
        hn = _rms(x, g_ref[...]).astype(BF16)
        acc = None
        f = wg_ref.shape[1]
        for c0 in range(0, f, fc):
            c1 = min(c0 + fc, f)
            gate = _dot(hn, wg_ref[:, c0:c1])
            up = _dot(hn, wu_ref[:, c0:c1])
            act = (jax.nn.silu(gate) * up).astype(BF16)
            part = _dot(act, wd_ref[c0:c1, :])
            acc = part if acc is None else acc + part
        x1 = x + 0.5 * acc
        if final:
            dst[0][...] = _rms(x1, gn_ref[...])
        else:
            dst[0][...] = x1
            dst[1][...] = _rms(x1, gn_ref[...]).astype(BF16)

    last = pl.num_programs(0) - 1
    pl.when(pl.program_id(0) < last)(functools.partial(run, x_ref, outs[:nout]))
    pl.when(pl.program_id(0) == last)(functools.partial(run, xs_ref, outs[nout:]))


def _ffn(x, xs, g, wg, wu, wd, gn, *, final, tm, fc):
    m, d = x.shape
    assert m % tm == 0 and fc % MXU_TILE_V7X == 0
    nt = m // tm
    tile = pl.BlockSpec((tm, d), lambda i: (jnp.minimum(i, nt - 1), 0))
    whole = pl.BlockSpec(xs.shape, lambda i: (0, 0))
    dts = (F32,) if final else (F32, BF16)
    return pl.pallas_call(
        functools.partial(_ffn_body, final=final, fc=fc),
        grid=(nt + 1,),
        in_specs=[tile, _const_spec(xs.shape), _const_spec(g.shape), _const_spec(wg.shape),
                  _const_spec(wu.shape), _const_spec(wd.shape), _const_spec(gn.shape)],
        out_specs=[tile] * len(dts) + [whole] * len(dts),
        out_shape=[jax.ShapeDtypeStruct(x.shape, dt) for dt in dts]
                  + [jax.ShapeDtypeStruct(xs.shape, dt) for dt in dts],
        compiler_params=_cparams(1),
        name="ffn_final" if final else "ffn_in",
    )(x, xs, g, wg, wu, wd, gn)


def _s5_param_body(are_ref, aim_ref, ldt_ref, bre_ref, bim_ref, abr_ref, abi_ref, bbr_ref, bbi_ref):
    lam_r = are_ref[...]
    lam_i = aim_ref[...]
    dt = jnp.exp(ldt_ref[...])
    mag = jnp.exp(lam_r * dt)
    ab_r = mag * jnp.cos(lam_i * dt)
    ab_i = mag * jnp.sin(lam_i * dt)
    den = lam_r * lam_r + lam_i * lam_i
    q_r = ((ab_r - 1.0) * lam_r + ab_i * lam_i) / den
    q_i = (ab_i * lam_r - (ab_r - 1.0) * lam_i) / den
    abr_ref[...] = ab_r
    abi_ref[...] = ab_i
    bre = bre_ref[...]
    bim = bim_ref[...]
    bbr_ref[...] = q_r[:, None, :] * bre - q_i[:, None, :] * bim
    bbi_ref[...] = q_r[:, None, :] * bim + q_i[:, None, :] * bre


def _s5_params(a_re, a_im, log_dt, b_re_t, b_im_t):
    g, p = a_re.shape
    hg = b_re_t.shape[1]
    return pl.pallas_call(
        _s5_param_body,
        out_shape=[jax.ShapeDtypeStruct((g, p), F32), jax.ShapeDtypeStruct((g, p), F32),
                   jax.ShapeDtypeStruct((g, hg, p), F32), jax.ShapeDtypeStruct((g, hg, p), F32)],
        name="s5_params",
    )(a_re, a_im, log_dt.reshape(g, 1), b_re_t, b_im_t)


def _block_diag(w, nblk):
    g, a, b = w.shape
    gb = g // nblk
    w4 = w.reshape(nblk, gb, a, b)
    eye = jnp.eye(gb, dtype=w.dtype)
    return jnp.einsum('jgab,gk->jgakb', w4, eye).reshape(nblk, gb * a, gb * b)


def _s5_body(h_ref, perm_ref, wu_ref, bb_ref, cm_ref, abr_ref, abi_ref, dsk_ref, wglu_ref, x0r_ref,
             x0i_ref, yb_ref, xr_out, xi_out, bu_scr, xr_scr, xi_scr, *, nb, steps, sub):
    c = pl.program_id(0)
    rows = nb * steps
    blk = h_ref.shape[1] // sub

    @pl.when(c == 0)
    def _():
        xr_scr[...] = x0r_ref[...]
        xi_scr[...] = x0i_ref[...]

    nin = wu_ref.shape[1] // S5_BLOCKS
    ns = abr_ref.shape[1] // S5_BLOCKS
    us = []
    for i in range(sub):
        h = h_ref[:, i * blk:(i + 1) * blk, :].reshape(rows, h_ref.shape[-1])
        if steps > 1:
            h = _dot(perm_ref[0], h).astype(BF16)
        u = _dot(h, wu_ref[...])
        us.append(u)
        ub = u.astype(BF16)
        for j in range(S5_BLOCKS):
            bu_scr[i, j] = _dot(ub[:, j * nin:(j + 1) * nin], bb_ref[j])
    xr = [xr_scr[:, j * ns:(j + 1) * ns] for j in range(S5_BLOCKS)]
    xi = [xi_scr[:, j * ns:(j + 1) * ns] for j in range(S5_BLOCKS)]
    for i in range(sub):
        ys = []
        for j in range(S5_BLOCKS):
            ar = jnp.broadcast_to(abr_ref[:, j * ns:(j + 1) * ns], (nb, ns))
            ai = jnp.broadcast_to(abi_ref[:, j * ns:(j + 1) * ns], (nb, ns))
            for t in range(steps):
                tr = slice(t * nb, (t + 1) * nb)
                xr[j], xi[j] = (ar * xr[j] - ai * xi[j] + bu_scr[i, j, tr, 0:ns],
                                ar * xi[j] + ai * xr[j] + bu_scr[i, j, tr, ns:2 * ns])
                bu_scr[i, j, tr, 0:ns] = xr[j]
                bu_scr[i, j, tr, ns:2 * ns] = xi[j]
            ys.append(_dot(bu_scr[i, j].astype(BF16), cm_ref[j]))
        y = jnp.concatenate(ys, axis=1) + dsk_ref[...] * us[i]
        z = jax.nn.gelu(y)
        yb = (z * jax.nn.sigmoid(_dot(z.astype(BF16), wglu_ref[...]))).astype(BF16)
        if steps > 1:
            yb = _dot(perm_ref[1], yb).astype(BF16)
        yb_ref[:, i * blk:(i + 1) * blk, :] = yb.reshape(yb_ref.shape[0], blk, yb_ref.shape[2])
    for j in range(S5_BLOCKS):
        xr_scr[:, j * ns:(j + 1) * ns] = xr[j]
        xi_scr[:, j * ns:(j + 1) * ns] = xi[j]
    xr_out[...] = xr_scr[...]
    xi_out[...] = xi_scr[...]


def _s5(h, w_u, bb, cm, abr, abi, dsk, wglu, x0r, x0i, *, nb, steps, sub=1):
    g, seq, d = h.shape
    rows = nb * steps
    blk = sub * rows // g
    assert seq % blk == 0 and (g == nb or steps == 1)
    su = w_u.shape[1]
    nst = abr.shape[1]
    r = jnp.arange(rows)
    to_tm = (r[:, None] % nb) * steps + r[:, None] // nb == r[None, :]
    perm = jnp.stack([to_tm, to_tm.T]).astype(BF16)
    return pl.pallas_call(
        functools.partial(_s5_body, nb=nb, steps=steps, sub=sub),
        grid=(seq // blk,),
        in_specs=[
            pl.BlockSpec((g, blk, d), lambda c: (0, c, 0)),
            _const_spec(perm.shape),
            _const_spec(w_u.shape), _const_spec(bb.shape), _const_spec(cm.shape),
            _const_spec(abr.shape), _const_spec(abi.shape), _const_spec(dsk.shape),
            _const_spec(wglu.shape), _const_spec(x0r.shape), _const_spec(x0i.shape),
        ],
        out_specs=[pl.BlockSpec((g, blk, su), lambda c: (0, c, 0)),
                   pl.BlockSpec((nb, nst), lambda c: (0, 0)),
                   pl.BlockSpec((nb, nst), lambda c: (0, 0))],
        out_shape=[jax.ShapeDtypeStruct((g, seq, su), BF16),
                   jax.ShapeDtypeStruct((nb, nst), F32),
                   jax.ShapeDtypeStruct((nb, nst), F32)],
        scratch_shapes=[pltpu.VMEM((sub, S5_BLOCKS, rows, 2 * nst // S5_BLOCKS), F32),
                        pltpu.VMEM((nb, nst), F32), pltpu.VMEM((nb, nst), F32)],
        compiler_params=_cparams(1),
        name="s5_branch",
    )(h, perm, w_u, bb, cm, abr, abi, dsk, wglu, x0r, x0i)


def _headsum(x, e):
    xb = x.astype(BF16)
    t = MXU_TILE_V7X
    return jnp.concatenate([_dot(xb[:, c:c + t], e[c:c + t, c:c + t]) for c in range(0, RW, t)], axis=1)


def _split3(x):
    hi = x.astype(BF16)
    r1 = x - hi.astype(F32)
    mid = r1.astype(BF16)
    lo = (r1 - mid.astype(F32)).astype(BF16)
    return hi, mid, lo


def _rwkv_pre(fm, w):
    r = fm[:, 0:RW]
    k = fm[:, RW:2 * RW]
    v = fm[:, 2 * RW:3 * RW]
    o0 = 3 * RW
    nw = w['w_up'].shape[0]
    na = w['a_up'].shape[0]
    wd = fm[:, o0:o0 + nw]
    ad = fm[:, o0 + nw:o0 + nw + na]
    gd = fm[:, o0 + nw + na:]
    w_log = -jax.nn.softplus(-(w['w0'][...] + _dot(jnp.tanh(wd).astype(BF16), w['w_up'][...]))) - 0.5
    logw = -jnp.exp(w_log)
    a = jax.nn.sigmoid(w['a0'][...] + _dot(ad.astype(BF16), w['a_up'][...]))
    g = _dot(jax.nn.sigmoid(gd).astype(BF16), w['g_up'][...])
    kk = k * w['k_k'][...]
    kk = kk * lax.rsqrt(jnp.maximum(_headsum(kk * kk, w['e']), 1e-24))
    k2 = k * (1.0 + (a - 1.0) * w['k_a'][...])
    return r, logw, k2, v, kk, kk * a, g


def _rwkv_post(o, r, k2, v, g, w):
    inv_n = 1.0 / HEAD
    mu = _headsum(o, w['e']) * inv_n
    d = o - mu
    var = _headsum(d * d, w['e']) * inv_n
    on = d * lax.rsqrt(var + LNX_EPS) * w['lnx_w'][...] + w['lnx_b'][...]
    bonus = _headsum(r * k2 * w['r_k'][...], w['e']) * v
    return (on + bonus) * g


def _sigmoid(x):
    return 0.5 * jnp.tanh(0.5 * x) + 0.5


def _merge_gates(h, y_b, w, c0=0, c1=None):
    dm = w['w_g'].shape[1] // 2
    c1 = dm if c1 is None else c1
    gate_a = _sigmoid(_dot(h, w['w_g'][:, c0:c1]))
    gate_b = _sigmoid(_dot(h, w['w_g'][:, dm + c0:dm + c1]))
    return gate_a, gate_b * _dot(y_b.astype(BF16), w['w_b_up'][:, c0:c1])


def _merge_out(x1, y_a, gate_a, gated_b, w):
    merged = gate_a * _dot(y_a.astype(BF16), w['w_a_up'][...]) + gated_b
    return x1 + _dot(merged.astype(BF16), w['w_out'][...])


_W_NAMES = ('w_f', 'w_g', 'mu', 'w0', 'w_up', 'a0', 'a_up', 'g_up', 'k_k', 'k_a', 'r_k',
            'lnx_w', 'lnx_b', 'e', 'w_a_up', 'w_b_up', 'w_out')


def _mixer_prompt_body(*refs, tc):
    h_ref, x1_ref, yb_ref = refs[0:3]
    nwt = len(_W_NAMES)
    w = dict(zip(_W_NAMES, refs[3:3 + nwt]))
    tri_ref, mask_ref = refs[3 + nwt:5 + nwt]
    x2_ref, shift_ref, wkv_ref = refs[5 + nwt:8 + nwt]
    f_scr, st_scr, o_scr, ga_scr, gb_scr = refs[8 + nwt:]
    c = pl.program_id(1)

    @pl.when(c == 0)
    def _():
        f_scr[0:8, :] = jnp.zeros((8, f_scr.shape[1]), F32)
        st_scr[...] = jnp.zeros_like(st_scr)

    @pl.when(c > 0)
    def _():
        f_scr[7:8, :] = f_scr[tc + 7:tc + 8, :]

    h = h_ref[...]
    f = _dot(h, w['w_f'][...])
    f_scr[8:8 + tc, :] = f
    shifted = f_scr[7:7 + tc, :]
    fm = f + w['mu'][...] * (shifted - f)
    r, logw, k2, v, kk, b, g = _rwkv_pre(fm, w)

    ns = tc // CHUNK
    dm = x1_ref.shape[1]
    yb = yb_ref[...]
    tri = tri_ref[...]
    mask = mask_ref[...] > 0.5
    pw = 2 * HEAD
    lane = lambda rows: lax.broadcasted_iota(jnp.int32, (rows, pw), 1)
    low = lambda a: lane(a.shape[0]) < HEAD
    eye2 = (lax.broadcasted_iota(jnp.int32, (CHUNK, pw), 0) == lane(CHUNK) % HEAD).astype(F32)
    n_double = CHUNK.bit_length() - 2
    pairs = range(PAIRS)
    first = lambda a: jnp.where(low(a), a, jnp.zeros_like(a))
    second = lambda a: jnp.where(low(a), jnp.zeros_like(a), a)
    bd = lambda y: jnp.concatenate([first(y), second(y)], axis=0)
    ad = lambda y: jnp.concatenate([second(y), first(y)], axis=0)
    pick = lambda x, y: jnp.where(low(x), x, y)
    stack = lambda top, bot: jnp.concatenate([top.astype(BF16), bot.astype(BF16)], axis=0)
    bf = lambda a: a.astype(BF16)
    zc = jnp.zeros((CHUNK, pw), BF16)

    def chunk_setup(s):
        sl = slice(s * CHUNK, (s + 1) * CHUNK)
        lw = logw[sl]
        hi, mid, lo = _split3(lw)
        lpre = _dot(tri, hi) + _dot(tri, mid) + _dot(tri, lo)
        lsuf = lpre[CHUNK - 1:CHUNK, :] - lpre
        e_pre = jnp.exp(lpre)
        e_neg = jnp.exp(-lpre)
        e_suf = jnp.exp(lsuf)
        lhs = jnp.concatenate([kk[sl] * jnp.exp(lpre - lw), r[sl] * e_pre], axis=0).astype(BF16)
        rhs = jnp.concatenate([b[sl] * e_neg, k2[sl] * e_neg], axis=0).astype(BF16)
        sfx_t = jnp.concatenate([b[sl] * e_suf, k2[sl] * e_suf], axis=0).T.astype(BF16)
        every = ns * MXU_TILE_V7X // dm
        if s % every == 0:
            c0 = s // every * MXU_TILE_V7X
            ga_scr[:, c0:c0 + MXU_TILE_V7X], gb_scr[:, c0:c0 + MXU_TILE_V7X] = _merge_gates(
                h, yb, w, c0, c0 + MXU_TILE_V7X)
        return lhs, rhs, sfx_t, v[sl].astype(BF16), e_pre[CHUNK - 1:CHUNK, :]

    def chunk_units(setups):
        units = [(s, pr) for s in setups for pr in pairs]
        cut = lambda a, pr: a[:, pr * pw:(pr + 1) * pw]
        lhs = {u: cut(setups[u[0]][0], u[1]) for u in units}
        rhs = {u: cut(setups[u[0]][1], u[1]) for u in units}
        a0 = {u: jnp.where(mask, _dot_nt(first(lhs[u]), rhs[u]), 0.0) for u in units}
        a1 = {u: jnp.where(mask, _dot_nt(second(lhs[u]),
                                         jnp.concatenate([rhs[u][CHUNK:], rhs[u][:CHUNK]], axis=0)), 0.0)
              for u in units}
        m_b = {u: pick(a0[u][:CHUNK], a1[u][:CHUNK]) for u in units}
        m_ks = {u: pick(a1[u][:CHUNK], a0[u][:CHUNK]) for u in units}
        n_b = {u: pick(a0[u][CHUNK:], a1[u][CHUNK:]) for u in units}
        n_ks = {u: pick(a1[u][CHUNK:], a0[u][CHUNK:]) for u in units}
        p = {u: -m_b[u] for u in units}
        winv = {u: eye2 + p[u] for u in units}
        p = {u: _dot(bf(p[u]), bd(bf(p[u]))) for u in units}
        for _ in range(n_double - 1):
            wp = {u: _dot(stack(winv[u], p[u]), bd(bf(p[u]))) for u in units}
            winv = {u: winv[u] + wp[u][:CHUNK] for u in units}
            p = {u: wp[u][CHUNK:] for u in units}
        wb = {u: bf(winv[u] + _dot(bf(winv[u]), bd(bf(p[u])))) for u in units}
        adv = {u: ad(cut(setups[u[0]][3], u[1])) for u in units}
        mkv = {u: bf(_dot(bf(m_ks[u]), adv[u])) for u in units}
        gu = {u: _dot(wb[u], jnp.concatenate([bd(lhs[u][:CHUNK]), bd(mkv[u])], axis=1)) for u in units}
        bdg = {u: bd(bf(gu[u][:, :pw])) for u in units}
        bdu = {u: bd(bf(-gu[u][:, pw:])) for u in units}
        q = {u: lhs[u][CHUNK:].astype(F32) - _dot(bf(n_b[u]), bdg[u]) for u in units}
        o0 = {u: _dot(jnp.concatenate([bf(n_b[u]), bf(n_ks[u])], axis=1),
                      jnp.concatenate([bdu[u], adv[u]], axis=0)) for u in units}
        sfx_t = {u: jnp.concatenate([setups[u[0]][2][(2 * u[1] + i) * HEAD:(2 * u[1] + i + 1) * HEAD, :]
                                     for i in range(2)], axis=1)
                 for u in units}
        gv = {u: jnp.concatenate(
            [jnp.concatenate([bdg[u][:CHUNK], bdu[u][:CHUNK]], axis=1),
             jnp.concatenate([zc, adv[u][CHUNK:]], axis=1),
             jnp.concatenate([bdg[u][CHUNK:], bdu[u][CHUNK:]], axis=1),
             jnp.concatenate([zc, adv[u][:CHUNK]], axis=1)], axis=0) for u in units}
        sg = {u: _dot(sfx_t[u], gv[u]) for u in units}
        qphi = {u: stack(q[u], eye2 * cut(setups[u[0]][4], u[1]) - sg[u][:, :pw]) for u in units}
        opsi = {u: jnp.concatenate([o0[u], sg[u][:, pw:]], axis=0) for u in units}
        return qphi, opsi

    qphi, opsi = chunk_units({s: chunk_setup(s) for s in range(ns)})
    st =[st_scr[pr] for pr in pairs]
    for s in range(ns):
        os = [_dot(qphi[s, pr], bd(bf(st[pr]))) + opsi[s, pr] for pr in pairs]
        for pr in pairs:
            o_scr[s * CHUNK:(s + 1) * CHUNK, pr * pw:(pr + 1) * pw] = os[pr][:CHUNK]
        st = [os[pr][CHUNK:] for pr in pairs]
    for pr in pairs:
        st_scr[pr] = st[pr]

    y_a = _rwkv_post(o_scr[...], r, k2, v, g, w)
    x2_ref[...] = _merge_out(x1_ref[...], y_a, ga_scr[...], gb_scr[...], w)

    @pl.when(c == pl.num_programs(1) - 1)
    def _():
        shift_ref[...] = f[tc - 1:tc, :]
        for hd in range(HEADS):
            wkv_ref[hd] = st_scr[hd // 2][:, (hd % 2) * HEAD:(hd % 2 + 1) * HEAD].T


def _mixer_prompt(h, x1, yb, wts, tri, mask, *, tc):
    bsz, seq, d = h.shape
    assert seq % tc == 0 and tc % CHUNK == 0
    nf = wts['w_f'].shape[1]
    wlist = [wts[n] for n in _W_NAMES]
    tok = lambda width: pl.BlockSpec((None, tc, width), lambda b, c: (b, c, 0))
    return pl.pallas_call(
        functools.partial(_mixer_prompt_body, tc=tc),
        grid=(bsz, seq // tc),
        in_specs=[tok(d), tok(d), tok(RW)] + [_const_spec(a.shape) for a in wlist]
                 + [_const_spec(tri.shape), _const_spec(mask.shape)],
        out_specs=[tok(d),
                   pl.BlockSpec((None, 1, nf), lambda b, c: (b, 0, 0)),
                   pl.BlockSpec((None, HEADS, HEAD, HEAD), lambda b, c: (b, 0, 0, 0))],
        out_shape=[jax.ShapeDtypeStruct((bsz, seq, d), F32),
                   jax.ShapeDtypeStruct((bsz, 1, nf), F32),
                   jax.ShapeDtypeStruct((bsz, HEADS, HEAD, HEAD), F32)],
        scratch_shapes=[pltpu.VMEM((tc + 8, nf), F32),
                        pltpu.VMEM((PAIRS, HEAD, 2 * HEAD), F32),
                        pltpu.VMEM((tc, RW), F32),
                        pltpu.VMEM((tc, d), F32), pltpu.VMEM((tc, d), F32)],
        compiler_params=_cparams(2),
        name="mixer_prompt",
    )(h, x1, yb, *wlist, tri, mask)


def _mixer_sample_body(*refs):
    h_ref, x1_ref, yb_ref, shift_in_ref, s_ref = refs[0:5]
    nwt = len(_W_NAMES)
    w = dict(zip(_W_NAMES, refs[5:5 + nwt]))
    x2_ref, shift_ref, s_out_ref = refs[5 + nwt:8 + nwt]
    t_scr, row_scr, ot_scr = refs[8 + nwt:]
    hd = pl.program_id(0)

    @pl.when(hd == 0)
    def _():
        f = _dot(h_ref[...], w['w_f'][...])
        fm = f + w['mu'][...] * (shift_in_ref[...] - f)
        shift_ref[...] = f
        r, logw, k2, v, kk, b, g = _rwkv_pre(fm, w)
        for i, a in enumerate((kk, jnp.exp(logw), b, k2, r, v)):
            t_scr[i] = a.T
        for i, a in enumerate((r, k2, v, g)):
            row_scr[i] = a

    off = pl.multiple_of(hd * HEAD, HEAD)
    hs = pl.ds(off, HEAD)
    kk_t, dec_t, b_t, k_t, r_t = (t_scr[i, hs, :] for i in range(5))

    def v_group(vg, carry):
        v0 = pl.multiple_of(vg * 8, 8)
        s8 = s_ref[pl.ds(v0, 8)]
        v8 = t_scr[5, pl.ds(off + v0, 8), :]
        o_rows = []
        for j in range(8):
            sa = -jnp.sum(s8[j] * kk_t, axis=0, keepdims=True)
            sn = s8[j] * dec_t + sa * b_t + v8[j:j + 1, :] * k_t
            s_out_ref[v0 + j] = sn
            o_rows.append(jnp.sum(sn * r_t, axis=0, keepdims=True))
        ot_scr[pl.ds(off + v0, 8), :] = jnp.concatenate(o_rows, axis=0)
        return carry

    lax.fori_loop(0, HEAD // 8, v_group, 0)

    @pl.when(hd == pl.num_programs(0) - 1)
    def _():
        r, k2, v, g = (row_scr[i] for i in range(4))
        y_a = _rwkv_post(ot_scr[...].T, r, k2, v, g, w)
        gate_a, gated_b = _merge_gates(h_ref[...], yb_ref[...], w)
        x2_ref[...] = _merge_out(x1_ref[...], y_a, gate_a, gated_b, w)


def _mixer_sample(h, x1, yb, shift_in, wkv_t, wts):
    n, d = h.shape
    nf = wts['w_f'].shape[1]
    wlist = [wts[nm] for nm in _W_NAMES]
    st_spec = pl.BlockSpec((None, HEAD, HEAD, n), lambda hd: (hd, 0, 0, 0))
    return pl.pallas_call(
        _mixer_sample_body,
        grid=(HEADS,),
        in_specs=[_const_spec(a.shape) for a in (h, x1, yb, shift_in)] + [st_spec]
                 + [_const_spec(a.shape) for a in wlist],
        out_specs=[pl.BlockSpec((n, d), lambda hd: (0, 0)),
                   pl.BlockSpec((n, nf), lambda hd: (0, 0)), st_spec],
        out_shape=[jax.ShapeDtypeStruct((n, d), F32),
                   jax.ShapeDtypeStruct((n, nf), F32),
                   jax.ShapeDtypeStruct(wkv_t.shape, F32)],
        scratch_shapes=[pltpu.VMEM((6, RW, n), F32), pltpu.VMEM((4, n, RW), F32),
                        pltpu.VMEM((RW, n), F32)],
        compiler_params=_cparams(1),
        name="mixer_sample",
    )(h, x1, yb, shift_in, wkv_t, *wlist)


def _chunk_constants():
    t = jnp.arange(CHUNK)
    incl = (t[:, None] >= t[None, :])
    strict = (t[:, None] > t[None, :])
    tri = incl.astype(BF16)
    top = jnp.concatenate([strict, strict], axis=1)
    bot = jnp.concatenate([incl, incl], axis=1)
    mask = jnp.concatenate([top, bot], axis=0).astype(F32)
    return tri, mask


def kernel(x_prompt, x_sample, state_shift, state_wkv, state_s5_re, state_s5_im, g_ffn1, ffn1_gate, ffn1_up, ffn1_down, g_mix, w_in, mu_shift, w0, w_up, a0, a_up, g_up, k_k, k_a, r_k, lnx_w, lnx_b, A_re, A_im, log_dt, B_re, B_im, C_re, C_im, D_skip, w_glu, w_a_up, w_b_up, w_out, g_ffn2, ffn2_gate, ffn2_up, ffn2_down, g_final):
    depth = g_ffn1.shape[0]
    assert depth == 1
    bp, seq, d = x_prompt.shape
    bs = x_sample.shape[0]
    assert x_sample.shape[1] == 1
    ncols_f = mu_shift.shape[1]
    su = D_skip.shape[1]
    groups, pstate = A_re.shape[1:]
    nstate = groups * pstate
    bf = lambda a: a.astype(BF16)
    row = lambda a: a.reshape(1, -1).astype(F32)

    w_in0 = w_in[0]
    eye_h = jnp.eye(HEADS, dtype=F32)
    wts = {
        'w_f': bf(w_in0[:, :ncols_f]),
        'w_g': bf(w_in0[:, ncols_f + su:]),
        'mu': row(mu_shift[0]), 'w0': row(w0[0]), 'w_up': bf(w_up[0]), 'a0': row(a0[0]),
        'a_up': bf(a_up[0]), 'g_up': bf(g_up[0]), 'k_k': row(k_k[0]), 'k_a': row(k_a[0]),
        'r_k': row(r_k[0]), 'lnx_w': row(lnx_w[0]), 'lnx_b': row(lnx_b[0]),
        'e': bf(jnp.kron(eye_h, jnp.ones((HEAD, HEAD), F32))),
        'w_a_up': bf(w_a_up[0]), 'w_b_up': bf(w_b_up[0]), 'w_out': bf(w_out[0]),
    }
    w_u = bf(w_in0[:, ncols_f:ncols_f + su])
    abr, abi, bbr, bbi = _s5_params(A_re[0], A_im[0], log_dt[0],
                                    jnp.swapaxes(B_re[0], 1, 2), jnp.swapaxes(B_im[0], 1, 2))
    bb = bf(jnp.concatenate([_block_diag(bbr, S5_BLOCKS), _block_diag(bbi, S5_BLOCKS)], axis=2))
    cm = bf(jnp.concatenate([_block_diag(jnp.swapaxes(C_re[0], 1, 2), S5_BLOCKS),
                             _block_diag(-jnp.swapaxes(C_im[0], 1, 2), S5_BLOCKS)], axis=1))
    abr = abr.reshape(1, nstate)
    abi = abi.reshape(1, nstate)
    dsk = row(D_skip[0])
    wglu = bf(w_glu[0])
    ffn1 = (row(g_ffn1[0]), bf(ffn1_gate[0]), bf(ffn1_up[0]), bf(ffn1_down[0]), row(g_mix[0]))
    ffn2 = (row(g_ffn2[0]), bf(ffn2_gate[0]), bf(ffn2_up[0]), bf(ffn2_down[0]), row(g_final))
    fc = 4 * MXU_TILE_V7X
    tri, mask = _chunk_constants()

    x1p, hnp, x1s, hns = _ffn(x_prompt.reshape(bp * seq, d), x_sample.reshape(bs, d), *ffn1,
                              final=False, tm=1024, fc=fc)

    hnp = hnp.reshape(bp, seq, d)
    zeros_state = jnp.zeros((bp, nstate), F32)
    yb_p, s5r_p, s5i_p = _s5(hnp, w_u, bb, cm, abr, abi, dsk, wglu, zeros_state, zeros_state,
                             nb=bp, steps=32, sub=4)
    x2p, shift_p, wkv_p = _mixer_prompt(hnp, x1p.reshape(bp, seq, d), yb_p, wts, tri, mask, tc=512)

    yb_s, s5r_s, s5i_s = _s5(hns[None], w_u, bb, cm, abr, abi, dsk, wglu,
                             state_s5_re[0].reshape(bs, nstate), state_s5_im[0].reshape(bs, nstate),
                             nb=bs, steps=1)
    wkv_t = jnp.transpose(state_wkv[0], (1, 2, 3, 0))
    x2s, shift_s, wkv_ts = _mixer_sample(hns, x1s, yb_s[0], state_shift[0], wkv_t, wts)
    wkv_s = jnp.transpose(wkv_ts, (3, 0, 1, 2))

    y_prompt, y_sample = _ffn(x2p.reshape(bp * seq, d), x2s, *ffn2, final=True, tm=1024, fc=fc)
    y_prompt = y_prompt.reshape(bp, seq, d)
    y_sample = y_sample.reshape(bs, 1, d)

    st5 = lambda a, n: a.reshape(1, n, groups, pstate)
    return (y_prompt, y_sample,
            shift_p.reshape(1, bp, ncols_f), wkv_p[None], st5(s5r_p, bp), st5(s5i_p, bp),
            shift_s[None], wkv_s[None], st5(s5r_s, bs), st5(s5i_s, bs))
```

```python
import functools

import jax
import jax.numpy as jnp
from jax import lax
from jax.experimental import pallas as pl
from jax.experimental.pallas import tpu as pltpu

F32 = jnp.float32
BF16 = jnp.bfloat16

NORM_EPS = 1e-6
LNX_EPS = 64e-5
HEAD = 64
HEADS = 8
PAIRS = HEADS // 2
RW = HEAD * HEADS
S5_BLOCKS = 4
CHUNK = 64
VMEM_LIMIT = 56 * 1024 * 1024
MXU_TILE_V7X = 256


def _dot(a, b):
    return jnp.dot(a, b, preferred_element_type=F32)


def _dot_nt(a, b):
    return lax.dot_general(a, b, (((1,), (1,)), ((), ())), preferred_element_type=F32)


def _rms(x, g):
    ms = jnp.mean(x * x, axis=-1, keepdims=True)
    return x * lax.rsqrt(ms + NORM_EPS) * g


def _const_spec(shape):
    nd = len(shape)
    return pl.BlockSpec(shape, lambda *_: (0,) * nd, pipeline_mode=pl.Buffered(1))


def _cparams(ngrid):
    return pltpu.CompilerParams(dimension_semantics=("arbitrary",) * ngrid,
                                vmem_limit_bytes=VMEM_LIMIT)


def _ffn_body(x_ref, xs_ref, g_ref, wg_ref, wu_ref, wd_ref, gn_ref, *outs, final, fc):
    nout = len(outs) // 2

    def run(src_ref, dst):
        x = src_ref[...]
        hn = _rms(x, g_ref[...]).astype(BF16)
        acc = None
        f = wg_ref.shape[1]
        for c0 in range(0, f, fc):
            c1 = min(c0 + fc, f)
            gate = _dot(hn, wg_ref[:, c0:c1].astype(BF16))
            up = _dot(hn, wu_ref[:, c0:c1].astype(BF16))
            act = (jax.nn.silu(gate) * up).astype(BF16)
            part = _dot(act, wd_ref[c0:c1, :].astype(BF16))
            acc = part if acc is None else acc + part
        x1 = x + 0.5 * acc
        if final:
            dst[0][...] = _rms(x1, gn_ref[...])
        else:
            dst[0][...] = x1
            dst[1][...] = _rms(x1, gn_ref[...]).astype(BF16)

    last = pl.num_programs(0) - 1
    pl.when(pl.program_id(0) < last)(functools.partial(run, x_ref, outs[:nout]))
    pl.when(pl.program_id(0) == last)(functools.partial(run, xs_ref, outs[nout:]))


def _ffn(x, xs, g, wg, wu, wd, gn, *, final, tm, fc):
    m, d = x.shape
    assert m % tm == 0 and fc % MXU_TILE_V7X == 0
    nt = m // tm
    tile = pl.BlockSpec((tm, d), lambda i: (jnp.minimum(i, nt - 1), 0))
    whole = pl.BlockSpec(xs.shape, lambda i: (0, 0))
    dts = (F32,) if final else (F32, BF16)
    return pl.pallas_call(
        functools.partial(_ffn_body, final=final, fc=fc),
        grid=(nt + 1,),
        in_specs=[tile, _const_spec(xs.shape), _const_spec(g.shape), _const_spec(wg.shape),
                  _const_spec(wu.shape), _const_spec(wd.shape), _const_spec(gn.shape)],
        out_specs=[tile] * len(dts) + [whole] * len(dts),
        out_shape=[jax.ShapeDtypeStruct(x.shape, dt) for dt in dts]
                  + [jax.ShapeDtypeStruct(xs.shape, dt) for dt in dts],
        compiler_params=_cparams(1),
        name="ffn_final" if final else "ffn_in",
    )(x, xs, g, wg, wu, wd, gn)


def _s5_param_body(are_ref, aim_ref, ldt_ref, bre_ref, bim_ref, abr_ref, abi_ref, bbr_ref, bbi_ref):
    lam_r = are_ref[...]
    lam_i = aim_ref[...]
    dt = jnp.exp(ldt_ref[...])
    mag = jnp.exp(lam_r * dt)
    ab_r = mag * jnp.cos(lam_i * dt)
    ab_i = mag * jnp.sin(lam_i * dt)
    den = lam_r * lam_r + lam_i * lam_i
    q_r = ((ab_r - 1.0) * lam_r + ab_i * lam_i) / den
    q_i = (ab_i * lam_r - (ab_r - 1.0) * lam_i) / den
    abr_ref[...] = ab_r
    abi_ref[...] = ab_i
    bre = bre_ref[...]
    bim = bim_ref[...]
    bbr_ref[...] = q_r[:, None, :] * bre - q_i[:, None, :] * bim
    bbi_ref[...] = q_r[:, None, :] * bim + q_i[:, None, :] * bre


def _s5_params(a_re, a_im, log_dt, b_re_t, b_im_t):
    g, p = a_re.shape
    hg = b_re_t.shape[1]
    return pl.pallas_call(
        _s5_param_body,
        out_shape=[jax.ShapeDtypeStruct((g, p), F32), jax.ShapeDtypeStruct((g, p), F32),
                   jax.ShapeDtypeStruct((g, hg, p), F32), jax.ShapeDtypeStruct((g, hg, p), F32)],
        name="s5_params",
    )(a_re, a_im, log_dt.reshape(g, 1), b_re_t, b_im_t)


def _block_diag(w, nblk):
    g, a, b = w.shape
    gb = g // nblk
    w4 = w.reshape(nblk, gb, a, b)
    eye = jnp.eye(gb, dtype=w.dtype)
    return jnp.einsum('jgab,gk->jgakb', w4, eye).reshape(nblk, gb * a, gb * b)


def _s5_body(h_ref, perm_ref, wu_ref, bb_ref, cm_ref, abr_ref, abi_ref, dsk_ref, wglu_ref, x0r_ref,
             x0i_ref, yb_ref, xr_out, xi_out, bu_scr, xr_scr, xi_scr, *, nb, steps, sub):
    c = pl.program_id(0)
    rows = nb * steps
    blk = h_ref.shape[1] // sub

    @pl.when(c == 0)
    def _():
        xr_scr[...] = x0r_ref[...]
        xi_scr[...] = x0i_ref[...]

    nin = wu_ref.shape[1] // S5_BLOCKS
    ns = abr_ref.shape[1] // S5_BLOCKS
    us = []
    for i in range(sub):
        h = h_ref[:, i * blk:(i + 1) * blk, :].reshape(rows, h_ref.shape[-1])
        if steps > 1:
            h = _dot(perm_ref[0], h).astype(BF16)
        u = _dot(h, wu_ref[...])
        us.append(u)
        ub = u.astype(BF16)
        for j in range(S5_BLOCKS):
            bu_scr[i, j] = _dot(ub[:, j * nin:(j + 1) * nin], bb_ref[j])
    xr = [xr_scr[:, j * ns:(j + 1) * ns] for j in range(S5_BLOCKS)]
    xi = [xi_scr[:, j * ns:(j + 1) * ns] for j in range(S5_BLOCKS)]
    for i in range(sub):
        ys = []
        for j in range(S5_BLOCKS):
            ar = jnp.broadcast_to(abr_ref[:, j * ns:(j + 1) * ns], (nb, ns))
            ai = jnp.broadcast_to(abi_ref[:, j * ns:(j + 1) * ns], (nb, ns))
            for t in range(steps):
                tr = slice(t * nb, (t + 1) * nb)
                xr[j], xi[j] = (ar * xr[j] - ai * xi[j] + bu_scr[i, j, tr, 0:ns],
                                ar * xi[j] + ai * xr[j] + bu_scr[i, j, tr, ns:2 * ns])
                bu_scr[i, j, tr, 0:ns] = xr[j]
                bu_scr[i, j, tr, ns:2 * ns] = xi[j]
            ys.append(_dot(bu_scr[i, j].astype(BF16), cm_ref[j]))
        y = jnp.concatenate(ys, axis=1) + dsk_ref[...] * us[i]
        z = jax.nn.gelu(y)
        yb = (z * jax.nn.sigmoid(_dot(z.astype(BF16), wglu_ref[...]))).astype(BF16)
        if steps > 1:
            yb = _dot(perm_ref[1], yb).astype(BF16)
        yb_ref[:, i * blk:(i + 1) * blk, :] = yb.reshape(yb_ref.shape[0], blk, yb_ref.shape[2])
    for j in range(S5_BLOCKS):
        xr_scr[:, j * ns:(j + 1) * ns] = xr[j]
        xi_scr[:, j * ns:(j + 1) * ns] = xi[j]
    xr_out[...] = xr_scr[...]
    xi_out[...] = xi_scr[...]


def _s5(h, w_u, bb, cm, abr, abi, dsk, wglu, x0r, x0i, *, nb, steps, sub=1):
    g, seq, d = h.shape
    rows = nb * steps
    blk = sub * rows // g
    assert seq % blk == 0 and (g == nb or steps == 1)
    su = w_u.shape[1]
    nst = abr.shape[1]
    r = jnp.arange(rows)
    to_tm = (r[:, None] % nb) * steps + r[:, None] // nb == r[None, :]
    perm = jnp.stack([to_tm, to_tm.T]).astype(BF16)
    return pl.pallas_call(
        functools.partial(_s5_body, nb=nb, steps=steps, sub=sub),
        grid=(seq // blk,),
        in_specs=[
            pl.BlockSpec((g, blk, d), lambda c: (0, c, 0)),
            _const_spec(perm.shape),
            _const_spec(w_u.shape), _const_spec(bb.shape), _const_spec(cm.shape),
            _const_spec(abr.shape), _const_spec(abi.shape), _const_spec(dsk.shape),
            _const_spec(wglu.shape), _const_spec(x0r.shape), _const_spec(x0i.shape),
        ],
        out_specs=[pl.BlockSpec((g, blk, su), lambda c: (0, c, 0)),
                   pl.BlockSpec((nb, nst), lambda c: (0, 0)),
                   pl.BlockSpec((nb, nst), lambda c: (0, 0))],
        out_shape=[jax.ShapeDtypeStruct((g, seq, su), BF16),
                   jax.ShapeDtypeStruct((nb, nst), F32),
                   jax.ShapeDtypeStruct((nb, nst), F32)],
        scratch_shapes=[pltpu.VMEM((sub, S5_BLOCKS, rows, 2 * nst // S5_BLOCKS), F32),
                        pltpu.VMEM((nb, nst), F32), pltpu.VMEM((nb, nst), F32)],
        compiler_params=_cparams(1),
        name="s5_branch",
    )(h, perm, w_u, bb, cm, abr, abi, dsk, wglu, x0r, x0i)


def _headsum(x, e):
    xb = x.astype(BF16)
    t = MXU_TILE_V7X
    return jnp.concatenate([_dot(xb[:, c:c + t], e[c:c + t, c:c + t]) for c in range(0, RW, t)], axis=1)


def _split3(x):
    hi = x.astype(BF16)
    r1 = x - hi.astype(F32)
    mid = r1.astype(BF16)
    lo = (r1 - mid.astype(F32)).astype(BF16)
    return hi, mid, lo


def _rwkv_pre(fm, w):
    r = fm[:, 0:RW]
    k = fm[:, RW:2 * RW]
    v = fm[:, 2 * RW:3 * RW]
    o0 = 3 * RW
    nw = w['w_up'].shape[0]
    na = w['a_up'].shape[0]
    wd = fm[:, o0:o0 + nw]
    ad = fm[:, o0 + nw:o0 + nw + na]
    gd = fm[:, o0 + nw + na:]
    w_log = -jax.nn.softplus(-(w['w0'][...] + _dot(jnp.tanh(wd).astype(BF16), w['w_up'][...]))) - 0.5
    logw = -jnp.exp(w_log)
    a = jax.nn.sigmoid(w['a0'][...] + _dot(ad.astype(BF16), w['a_up'][...]))
    g = _dot(jax.nn.sigmoid(gd).astype(BF16), w['g_up'][...])
    kk = k * w['k_k'][...]
    kk = kk * lax.rsqrt(jnp.maximum(_headsum(kk * kk, w['e']), 1e-24))
    k2 = k * (1.0 + (a - 1.0) * w['k_a'][...])
    return r, logw, k2, v, kk, kk * a, g


def _rwkv_post(o, r, k2, v, g, w):
    inv_n = 1.0 / HEAD
    mu = _headsum(o, w['e']) * inv_n
    d = o - mu
    var = _headsum(d * d, w['e']) * inv_n
    on = d * lax.rsqrt(var + LNX_EPS) * w['lnx_w'][...] + w['lnx_b'][...]
    bonus = _headsum(r * k2 * w['r_k'][...], w['e']) * v
    return (on + bonus) * g


def _sigmoid(x):
    return 0.5 * jnp.tanh(0.5 * x) + 0.5


def _merge_gates(h, y_b, w, c0=0, c1=None):
    dm = w['w_g'].shape[1] // 2
    c1 = dm if c1 is None else c1
    gate_a = _sigmoid(_dot(h, w['w_g'][:, c0:c1]))
    gate_b = _sigmoid(_dot(h, w['w_g'][:, dm + c0:dm + c1]))
    return gate_a, gate_b * _dot(y_b.astype(BF16), w['w_b_up'][:, c0:c1])


def _merge_out(x1, y_a, gate_a, gated_b, w):
    merged = gate_a * _dot(y_a.astype(BF16), w['w_a_up'][...]) + gated_b
    return x1 + _dot(merged.astype(BF16), w['w_out'][...])


_W_NAMES = ('w_f', 'w_g', 'mu', 'w0', 'w_up', 'a0', 'a_up', 'g_up', 'k_k', 'k_a', 'r_k',
            'lnx_w', 'lnx_b', 'e', 'w_a_up', 'w_b_up', 'w_out')


def _mixer_prompt_body(*refs, tc):
    h_ref, x1_ref, yb_ref = refs[0:3]
    nwt = len(_W_NAMES)
    w = dict(zip(_W_NAMES, refs[3:3 + nwt]))
    tri_ref, mask_ref = refs[3 + nwt:5 + nwt]
    x2_ref, shift_ref, wkv_ref = refs[5 + nwt:8 + nwt]
    f_scr, st_scr, o_scr, ga_scr, gb_scr = refs[8 + nwt:]
    c = pl.program_id(1)

    @pl.when(c == 0)
    def _():
        f_scr[0:8, :] = jnp.zeros((8, f_scr.shape[1]), F32)
        st_scr[...] = jnp.zeros_like(st_scr)

    @pl.when(c > 0)
    def _():
        f_scr[7:8, :] = f_scr[tc + 7:tc + 8, :]

    h = h_ref[...]
    f = _dot(h, w['w_f'][...])
    f_scr[8:8 + tc, :] = f
    shifted = f_scr[7:7 + tc, :]
    fm = f + w['mu'][...] * (shifted - f)
    r, logw, k2, v, kk, b, g = _rwkv_pre(fm, w)

    ns = tc // CHUNK
    dm = x1_ref.shape[1]
    yb = yb_ref[...]
    tri = tri_ref[...]
    mask = mask_ref[...] > 0.5
    pw = 2 * HEAD
    lane = lambda rows: lax.broadcasted_iota(jnp.int32, (rows, pw), 1)
    low = lambda a: lane(a.shape[0]) < HEAD
    eye2 = (lax.broadcasted_iota(jnp.int32, (CHUNK, pw), 0) == lane(CHUNK) % HEAD).astype(F32)
    n_double = CHUNK.bit_length() - 2
    pairs = range(PAIRS)
    first = lambda a: jnp.where(low(a), a, jnp.zeros_like(a))
    second = lambda a: jnp.where(low(a), jnp.zeros_like(a), a)
    bd = lambda y: jnp.concatenate([first(y), second(y)], axis=0)
    ad = lambda y: jnp.concatenate([second(y), first(y)], axis=0)
    pick = lambda x, y: jnp.where(low(x), x, y)
    stack = lambda top, bot: jnp.concatenate([top.astype(BF16), bot.astype(BF16)], axis=0)
    bf = lambda a: a.astype(BF16)
    zc = jnp.zeros((CHUNK, pw), BF16)

    def chunk_setup(s):
        sl = slice(s * CHUNK, (s + 1) * CHUNK)
        lw = logw[sl]
        hi, mid, lo = _split3(lw)
        lpre = _dot(tri, hi) + _dot(tri, mid) + _dot(tri, lo)
        lsuf = lpre[CHUNK - 1:CHUNK, :] - lpre
        e_pre = jnp.exp(lpre)
        e_neg = jnp.exp(-lpre)
        e_suf = jnp.exp(lsuf)
        lhs = jnp.concatenate([kk[sl] * jnp.exp(lpre - lw), r[sl] * e_pre], axis=0).astype(BF16)
        rhs = jnp.concatenate([b[sl] * e_neg, k2[sl] * e_neg], axis=0).astype(BF16)
        sfx_t = jnp.concatenate([b[sl] * e_suf, k2[sl] * e_suf], axis=0).T.astype(BF16)
        every = ns * MXU_TILE_V7X // dm
        if s % every == 0:
            c0 = s // every * MXU_TILE_V7X
            ga_scr[:, c0:c0 + MXU_TILE_V7X], gb_scr[:, c0:c0 + MXU_TILE_V7X] = _merge_gates(
                h, yb, w, c0, c0 + MXU_TILE_V7X)
        return lhs, rhs, sfx_t, v[sl].astype(BF16), e_pre[CHUNK - 1:CHUNK, :]

    def chunk_units(setups):
        units = [(s, pr) for s in setups for pr in pairs]
        cut = lambda a, pr: a[:, pr * pw:(pr + 1) * pw]
        lhs = {u: cut(setups[u[0]][0], u[1]) for u in units}
        rhs = {u: cut(setups[u[0]][1], u[1]) for u in units}
        a0 = {u: jnp.where(mask, _dot_nt(first(lhs[u]), rhs[u]), 0.0) for u in units}
        a1 = {u: jnp.where(mask, _dot_nt(second(lhs[u]),
                                         jnp.concatenate([rhs[u][CHUNK:], rhs[u][:CHUNK]], axis=0)), 0.0)
              for u in units}
        m_b = {u: pick(a0[u][:CHUNK], a1[u][:CHUNK]) for u in units}
        m_ks = {u: pick(a1[u][:CHUNK], a0[u][:CHUNK]) for u in units}
        n_b = {u: pick(a0[u][CHUNK:], a1[u][CHUNK:]) for u in units}
        n_ks = {u: pick(a1[u][CHUNK:], a0[u][CHUNK:]) for u in units}
        p = {u: -m_b[u] for u in units}
        winv = {u: eye2 + p[u] for u in units}
        p = {u: _dot(bf(p[u]), bd(bf(p[u]))) for u in units}
        for _ in range(n_double - 1):
            wp = {u: _dot(stack(winv[u], p[u]), bd(bf(p[u]))) for u in units}
            winv = {u: winv[u] + wp[u][:CHUNK] for u in units}
            p = {u: wp[u][CHUNK:] for u in units}
        wb = {u: bf(winv[u] + _dot(bf(winv[u]), bd(bf(p[u])))) for u in units}
        adv = {u: ad(cut(setups[u[0]][3], u[1])) for u in units}
        mkv = {u: bf(_dot(bf(m_ks[u]), adv[u])) for u in units}
        gu = {u: _dot(wb[u], jnp.concatenate([bd(lhs[u][:CHUNK]), bd(mkv[u])], axis=1)) for u in units}
        bdg = {u: bd(bf(gu[u][:, :pw])) for u in units}
        bdu = {u: bd(bf(-gu[u][:, pw:])) for u in units}
        q = {u: lhs[u][CHUNK:].astype(F32) - _dot(bf(n_b[u]), bdg[u]) for u in units}
        o0 = {u: _dot(jnp.concatenate([bf(n_b[u]), bf(n_ks[u])], axis=1),
                      jnp.concatenate([bdu[u], adv[u]], axis=0)) for u in units}
        sfx_t = {u: jnp.concatenate([setups[u[0]][2][(2 * u[1] + i) * HEAD:(2 * u[1] + i + 1) * HEAD, :]
                                     for i in range(2)], axis=1)
                 for u in units}
        gv = {u: jnp.concatenate(
            [jnp.concatenate([bdg[u][:CHUNK], bdu[u][:CHUNK]], axis=1),
             jnp.concatenate([zc, adv[u][CHUNK:]], axis=1),
             jnp.concatenate([bdg[u][CHUNK:], bdu[u][CHUNK:]], axis=1),
             jnp.concatenate([zc, adv[u][:CHUNK]], axis=1)], axis=0) for u in units}
        sg = {u: _dot(sfx_t[u], gv[u]) for u in units}
        qphi = {u: stack(q[u], eye2 * cut(setups[u[0]][4], u[1]) - sg[u][:, :pw]) for u in units}
        opsi = {u: jnp.concatenate([o0[u], sg[u][:, pw:]], axis=0) for u in units}
        return qphi, opsi

    qphi, opsi = chunk_units({s: chunk_setup(s) for s in range(ns)})
    st =[st_scr[pr] for pr in pairs]
    for s in range(ns):
        os = [_dot(qphi[s, pr], bd(bf(st[pr]))) + opsi[s, pr] for pr in pairs]
        for pr in pairs:
            o_scr[s * CHUNK:(s + 1) * CHUNK, pr * pw:(pr + 1) * pw] = os[pr][:CHUNK]
        st = [os[pr][CHUNK:] for pr in pairs]
    for pr in pairs:
        st_scr[pr] = st[pr]

    y_a = _rwkv_post(o_scr[...], r, k2, v, g, w)
    x2_ref[...] = _merge_out(x1_ref[...], y_a, ga_scr[...], gb_scr[...], w)

    @pl.when(c == pl.num_programs(1) - 1)
    def _():
        shift_ref[...] = f[tc - 1:tc, :]
        for hd in range(HEADS):
            wkv_ref[hd] = st_scr[hd // 2][:, (hd % 2) * HEAD:(hd % 2 + 1) * HEAD].T


def _mixer_prompt(h, x1, yb, wts, tri, mask, *, tc):
    bsz, seq, d = h.shape
    assert seq % tc == 0 and tc % CHUNK == 0
    nf = wts['w_f'].shape[1]
    wlist = [wts[n] for n in _W_NAMES]
    tok = lambda width: pl.BlockSpec((None, tc, width), lambda b, c: (b, c, 0))
    return pl.pallas_call(
        functools.partial(_mixer_prompt_body, tc=tc),
        grid=(bsz, seq // tc),
        in_specs=[tok(d), tok(d), tok(RW)] + [_const_spec(a.shape) for a in wlist]
                 + [_const_spec(tri.shape), _const_spec(mask.shape)],
        out_specs=[tok(d),
                   pl.BlockSpec((None, 1, nf), lambda b, c: (b, 0, 0)),
                   pl.BlockSpec((None, HEADS, HEAD, HEAD), lambda b, c: (b, 0, 0, 0))],
        out_shape=[jax.ShapeDtypeStruct((bsz, seq, d), F32),
                   jax.ShapeDtypeStruct((bsz, 1, nf), F32),
                   jax.ShapeDtypeStruct((bsz, HEADS, HEAD, HEAD), F32)],
        scratch_shapes=[pltpu.VMEM((tc + 8, nf), F32),
                        pltpu.VMEM((PAIRS, HEAD, 2 * HEAD), F32),
                        pltpu.VMEM((tc, RW), F32),
                        pltpu.VMEM((tc, d), F32), pltpu.VMEM((tc, d), F32)],
        compiler_params=_cparams(2),
        name="mixer_prompt",
    )(h, x1, yb, *wlist, tri, mask)


def _mixer_sample_body(*refs):
    h_ref, x1_ref, yb_ref, shift_in_ref, s_ref = refs[0:5]
    nwt = len(_W_NAMES)
    w = dict(zip(_W_NAMES, refs[5:5 + nwt]))
    x2_ref, shift_ref, s_out_ref = refs[5 + nwt:8 + nwt]
    t_scr, row_scr, ot_scr = refs[8 + nwt:]
    hd = pl.program_id(0)

    @pl.when(hd == 0)
    def _():
        f = _dot(h_ref[...], w['w_f'][...])
        fm = f + w['mu'][...] * (shift_in_ref[...] - f)
        shift_ref[...] = f
        r, logw, k2, v, kk, b, g = _rwkv_pre(fm, w)
        for i, a in enumerate((kk, jnp.exp(logw), b, k2, r, v)):
            t_scr[i] = a.T
        for i, a in enumerate((r, k2, v, g)):
            row_scr[i] = a

    off = pl.multiple_of(hd * HEAD, HEAD)
    hs = pl.ds(off, HEAD)
    kk_t, dec_t, b_t, k_t, r_t = (t_scr[i, hs, :] for i in range(5))

    def v_group(vg, carry):
        v0 = pl.multiple_of(vg * 8, 8)
        s8 = s_ref[pl.ds(v0, 8)]
        v8 = t_scr[5, pl.ds(off + v0, 8), :]
        o_rows = []
        for j in range(8):
            sa = -jnp.sum(s8[j] * kk_t, axis=0, keepdims=True)
            sn = s8[j] * dec_t + sa * b_t + v8[j:j + 1, :] * k_t
            s_out_ref[v0 + j] = sn
            o_rows.append(jnp.sum(sn * r_t, axis=0, keepdims=True))
        ot_scr[pl.ds(off + v0, 8), :] = jnp.concatenate(o_rows, axis=0)
        return carry

    lax.fori_loop(0, HEAD // 8, v_group, 0)

    @pl.when(hd == pl.num_programs(0) - 1)
    def _():
        r, k2, v, g = (row_scr[i] for i in range(4))
        y_a = _rwkv_post(ot_scr[...].T, r, k2, v, g, w)
        gate_a, gated_b = _merge_gates(h_ref[...], yb_ref[...], w)
        x2_ref[...] = _merge_out(x1_ref[...], y_a, gate_a, gated_b, w)


def _mixer_sample(h, x1, yb, shift_in, wkv_t, wts):
    n, d = h.shape
    nf = wts['w_f'].shape[1]
    wlist = [wts[nm] for nm in _W_NAMES]
    st_spec = pl.BlockSpec((None, HEAD, HEAD, n), lambda hd: (hd, 0, 0, 0))
    return pl.pallas_call(
        _mixer_sample_body,
        grid=(HEADS,),
        in_specs=[_const_spec(a.shape) for a in (h, x1, yb, shift_in)] + [st_spec]
                 + [_const_spec(a.shape) for a in wlist],
        out_specs=[pl.BlockSpec((n, d), lambda hd: (0, 0)),
                   pl.BlockSpec((n, nf), lambda hd: (0, 0)), st_spec],
        out_shape=[jax.ShapeDtypeStruct((n, d), F32),
                   jax.ShapeDtypeStruct((n, nf), F32),
                   jax.ShapeDtypeStruct(wkv_t.shape, F32)],
        scratch_shapes=[pltpu.VMEM((6, RW, n), F32), pltpu.VMEM((4, n, RW), F32),
                        pltpu.VMEM((RW, n), F32)],
        compiler_params=_cparams(1),
        name="mixer_sample",
    )(h, x1, yb, shift_in, wkv_t, *wlist)


def _chunk_constants():
    t = jnp.arange(CHUNK)
    incl = (t[:, None] >= t[None, :])
    strict = (t[:, None] > t[None, :])
    tri = incl.astype(BF16)
    top = jnp.concatenate([strict, strict], axis=1)
    bot = jnp.concatenate([incl, incl], axis=1)
    mask = jnp.concatenate([top, bot], axis=0).astype(F32)
    return tri, mask


def kernel(x_prompt, x_sample, state_shift, state_wkv, state_s5_re, state_s5_im, g_ffn1, ffn1_gate, ffn1_up, ffn1_down, g_mix, w_in, mu_shift, w0, w_up, a0, a_up, g_up, k_k, k_a, r_k, lnx_w, lnx_b, A_re, A_im, log_dt, B_re, B_im, C_re, C_im, D_skip, w_glu, w_a_up, w_b_up, w_out, g_ffn2, ffn2_gate, ffn2_up, ffn2_down, g_final):
    depth = g_ffn1.shape[0]
    assert depth == 1
    bp, seq, d = x_prompt.shape
    bs = x_sample.shape[0]
    assert x_sample.shape[1] == 1
    ncols_f = mu_shift.shape[1]
    su = D_skip.shape[1]
    groups, pstate = A_re.shape[1:]
    nstate = groups * pstate
    bf = lambda a: a.astype(BF16)
    row = lambda a: a.reshape(1, -1).astype(F32)

    w_in0 = w_in[0]
    eye_h = jnp.eye(HEADS, dtype=F32)
    wts = {
        'w_f': bf(w_in0[:, :ncols_f]),
        'w_g': bf(w_in0[:, ncols_f + su:]),
        'mu': row(mu_shift[0]), 'w0': row(w0[0]), 'w_up': bf(w_up[0]), 'a0': row(a0[0]),
        'a_up': bf(a_up[0]), 'g_up': bf(g_up[0]), 'k_k': row(k_k[0]), 'k_a': row(k_a[0]),
        'r_k': row(r_k[0]), 'lnx_w': row(lnx_w[0]), 'lnx_b': row(lnx_b[0]),
        'e': bf(jnp.kron(eye_h, jnp.ones((HEAD, HEAD), F32))),
        'w_a_up': bf(w_a_up[0]), 'w_b_up': bf(w_b_up[0]), 'w_out': bf(w_out[0]),
    }
    w_u = bf(w_in0[:, ncols_f:ncols_f + su])
    abr, abi, bbr, bbi = _s5_params(A_re[0], A_im[0], log_dt[0],
                                    jnp.swapaxes(B_re[0], 1, 2), jnp.swapaxes(B_im[0], 1, 2))
    bb = bf(jnp.concatenate([_block_diag(bbr, S5_BLOCKS), _block_diag(bbi, S5_BLOCKS)], axis=2))
    cm = bf(jnp.concatenate([_block_diag(jnp.swapaxes(C_re[0], 1, 2), S5_BLOCKS),
                             _block_diag(-jnp.swapaxes(C_im[0], 1, 2), S5_BLOCKS)], axis=1))
    abr = abr.reshape(1, nstate)
    abi = abi.reshape(1, nstate)
    dsk = row(D_skip[0])
    wglu = bf(w_glu[0])
    ffn1 = (row(g_ffn1[0]), ffn1_gate[0], ffn1_up[0], ffn1_down[0], row(g_mix[0]))
    ffn2 = (row(g_ffn2[0]), ffn2_gate[0], ffn2_up[0], ffn2_down[0], row(g_final))
    fc = 2 * MXU_TILE_V7X
    tri, mask = _chunk_constants()

    x1p, hnp, x1s, hns = _ffn(x_prompt.reshape(bp * seq, d), x_sample.reshape(bs, d), *ffn1,
                              final=False, tm=512, fc=fc)

    hnp = hnp.reshape(bp, seq, d)
    zeros_state = jnp.zeros((bp, nstate), F32)
    yb_p, s5r_p, s5i_p = _s5(hnp, w_u, bb, cm, abr, abi, dsk, wglu, zeros_state, zeros_state,
                             nb=bp, steps=32, sub=4)
    x2p, shift_p, wkv_p = _mixer_prompt(hnp, x1p.reshape(bp, seq, d), yb_p, wts, tri, mask, tc=512)

    yb_s, s5r_s, s5i_s = _s5(hns[None], w_u, bb, cm, abr, abi, dsk, wglu,
                             state_s5_re[0].reshape(bs, nstate), state_s5_im[0].reshape(bs, nstate),
                             nb=bs, steps=1)
    wkv_t = jnp.transpose(state_wkv[0], (1, 2, 3, 0))
    x2s, shift_s, wkv_ts = _mixer_sample(hns, x1s, yb_s[0], state_shift[0], wkv_t, wts)
    wkv_s = jnp.transpose(wkv_ts, (3, 0, 1, 2))

    y_prompt, y_sample = _ffn(x2p.reshape(bp * seq, d), x2s, *ffn2, final=True, tm=512, fc=fc)
    y_prompt = y_prompt.reshape(bp, seq, d)
    y_sample = y_sample.reshape(bs, 1, d)

    st5 = lambda a, n: a.reshape(1, n, groups, pstate)
    return (y_prompt, y_sample,
            shift_p.reshape(1, bp, ncols_f), wkv_p[None], st5(s5r_p, bp), st5(s5i_p, bp),
            shift_s[None], wkv_s[None], st5(s5r_s, bs), st5(s5i_s, bs))
```

```python
import functools

import jax
import jax.numpy as jnp
from jax import lax
from jax.experimental import pallas as pl
from jax.experimental.pallas import tpu as pltpu

F32 = jnp.float32
BF16 = jnp.bfloat16

NORM_EPS = 1e-6
LNX_EPS = 64e-5
HEAD = 64
HEADS = 8
PAIRS = HEADS // 2
RW = HEAD * HEADS
S5_BLOCKS = 4
CHUNK = 64
VMEM_LIMIT = 56 * 1024 * 1024
MXU_TILE_V7X = 256


def _dot(a, b):
    return jnp.dot(a, b, preferred_element_type=F32)


def _dot_nt(a, b):
    return lax.dot_general(a, b, (((1,), (1,)), ((), ())), preferred_element_type=F32)


def _rms(x, g):
    ms = jnp.mean(x * x, axis=-1, keepdims=True)
    return x * lax.rsqrt(ms + NORM_EPS) * g


def _const_spec(shape):
    nd = len(shape)
    return pl.BlockSpec(shape, lambda *_: (0,) * nd, pipeline_mode=pl.Buffered(1))


def _cparams(ngrid):
    return pltpu.CompilerParams(dimension_semantics=("arbitrary",) * ngrid,
                                vmem_limit_bytes=VMEM_LIMIT)


def _ffn_body(x_ref, xs_ref, g_ref, wg_ref, wu_ref, wd_ref, gn_ref, *outs, final, fc):
    nout = len(outs) // 2

    def run(src_ref, dst):
        x = src_ref[...]
        hn = _rms(x, g_ref[...]).astype(BF16)
        acc = None
        f = wg_ref.shape[1]
        for c0 in range(0, f, fc):
            c1 = min(c0 + fc, f)
            gate = _dot(hn, wg_ref[:, c0:c1].astype(BF16))
            up = _dot(hn, wu_ref[:, c0:c1].astype(BF16))
            act = (jax.nn.silu(gate) * up).astype(BF16)
            part = _dot(act, wd_ref[c0:c1, :].astype(BF16))
            acc = part if acc is None else acc + part
        x1 = x + 0.5 * acc
        if final:
            dst[0][...] = _rms(x1, gn_ref[...])
        else:
            dst[0][...] = x1
            dst[1][...] = _rms(x1, gn_ref[...]).astype(BF16)

    last = pl.num_programs(0) - 1
    pl.when(pl.program_id(0) < last)(functools.partial(run, x_ref, outs[:nout]))
    pl.when(pl.program_id(0) == last)(functools.partial(run, xs_ref, outs[nout:]))


def _ffn(x, xs, g, wg, wu, wd, gn, *, final, tm, fc):
    m, d = x.shape
    assert m % tm == 0 and fc % MXU_TILE_V7X == 0
    nt = m // tm
    tile = pl.BlockSpec((tm, d), lambda i: (jnp.minimum(i, nt - 1), 0))
    whole = pl.BlockSpec(xs.shape, lambda i: (0, 0))
    dts = (F32,) if final else (F32, BF16)
    return pl.pallas_call(
        functools.partial(_ffn_body, final=final, fc=fc),
        grid=(nt + 1,),
        in_specs=[tile, _const_spec(xs.shape), _const_spec(g.shape), _const_spec(wg.shape),
                  _const_spec(wu.shape), _const_spec(wd.shape), _const_spec(gn.shape)],
        out_specs=[tile] * len(dts) + [whole] * len(dts),
        out_shape=[jax.ShapeDtypeStruct(x.shape, dt) for dt in dts]
                  + [jax.ShapeDtypeStruct(xs.shape, dt) for dt in dts],
        compiler_params=_cparams(1),
        name="ffn_final" if final else "ffn_in",
    )(x, xs, g, wg, wu, wd, gn)


def _s5_param_body(are_ref, aim_ref, ldt_ref, bre_ref, bim_ref, abr_ref, abi_ref, bbr_ref, bbi_ref):
    lam_r = are_ref[...]
    lam_i = aim_ref[...]
    dt = jnp.exp(ldt_ref[...])
    mag = jnp.exp(lam_r * dt)
    ab_r = mag * jnp.cos(lam_i * dt)
    ab_i = mag * jnp.sin(lam_i * dt)
    den = lam_r * lam_r + lam_i * lam_i
    q_r = ((ab_r - 1.0) * lam_r + ab_i * lam_i) / den
    q_i = (ab_i * lam_r - (ab_r - 1.0) * lam_i) / den
    abr_ref[...] = ab_r
    abi_ref[...] = ab_i
    bre = bre_ref[...]
    bim = bim_ref[...]
    bbr_ref[...] = q_r[:, None, :] * bre - q_i[:, None, :] * bim
    bbi_ref[...] = q_r[:, None, :] * bim + q_i[:, None, :] * bre


def _s5_params(a_re, a_im, log_dt, b_re_t, b_im_t):
    g, p = a_re.shape
    hg = b_re_t.shape[1]
    return pl.pallas_call(
        _s5_param_body,
        out_shape=[jax.ShapeDtypeStruct((g, p), F32), jax.ShapeDtypeStruct((g, p), F32),
                   jax.ShapeDtypeStruct((g, hg, p), F32), jax.ShapeDtypeStruct((g, hg, p), F32)],
        name="s5_params",
    )(a_re, a_im, log_dt.reshape(g, 1), b_re_t, b_im_t)


def _block_diag(w, nblk):
    g, a, b = w.shape
    gb = g // nblk
    w4 = w.reshape(nblk, gb, a, b)
    eye = jnp.eye(gb, dtype=w.dtype)
    return jnp.einsum('jgab,gk->jgakb', w4, eye).reshape(nblk, gb * a, gb * b)


def _s5_body(h_ref, perm_ref, wu_ref, bb_ref, cm_ref, abr_ref, abi_ref, dsk_ref, wglu_ref, x0r_ref,
             x0i_ref, yb_ref, xr_out, xi_out, bu_scr, xr_scr, xi_scr, *, nb, steps, sub):
    c = pl.program_id(0)
    rows = nb * steps
    blk = h_ref.shape[1] // sub

    @pl.when(c == 0)
    def _():
        xr_scr[...] = x0r_ref[...]
        xi_scr[...] = x0i_ref[...]

    nin = wu_ref.shape[1] // S5_BLOCKS
    ns = abr_ref.shape[1] // S5_BLOCKS
    us = []
    for i in range(sub):
        h = h_ref[:, i * blk:(i + 1) * blk, :].reshape(rows, h_ref.shape[-1])
        if steps > 1:
            h = _dot(perm_ref[0], h).astype(BF16)
        u = _dot(h, wu_ref[...])
        us.append(u)
        ub = u.astype(BF16)
        for j in range(S5_BLOCKS):
            bu_scr[i, j] = _dot(ub[:, j * nin:(j + 1) * nin], bb_ref[j])
    xr = [xr_scr[:, j * ns:(j + 1) * ns] for j in range(S5_BLOCKS)]
    xi = [xi_scr[:, j * ns:(j + 1) * ns] for j in range(S5_BLOCKS)]
    for i in range(sub):
        ys = []
        for j in range(S5_BLOCKS):
            ar = jnp.broadcast_to(abr_ref[:, j * ns:(j + 1) * ns], (nb, ns))
            ai = jnp.broadcast_to(abi_ref[:, j * ns:(j + 1) * ns], (nb, ns))
            for t in range(steps):
                tr = slice(t * nb, (t + 1) * nb)
                xr[j], xi[j] = (ar * xr[j] - ai * xi[j] + bu_scr[i, j, tr, 0:ns],
                                ar * xi[j] + ai * xr[j] + bu_scr[i, j, tr, ns:2 * ns])
                bu_scr[i, j, tr, 0:ns] = xr[j]
                bu_scr[i, j, tr, ns:2 * ns] = xi[j]
            ys.append(_dot(bu_scr[i, j].astype(BF16), cm_ref[j]))
        y = jnp.concatenate(ys, axis=1) + dsk_ref[...] * us[i]
        z = jax.nn.gelu(y)
        yb = (z * jax.nn.sigmoid(_dot(z.astype(BF16), wglu_ref[...]))).astype(BF16)
        if steps > 1:
            yb = _dot(perm_ref[1], yb).astype(BF16)
        yb_ref[:, i * blk:(i + 1) * blk, :] = yb.reshape(yb_ref.shape[0], blk, yb_ref.shape[2])
    for j in range(S5_BLOCKS):
        xr_scr[:, j * ns:(j + 1) * ns] = xr[j]
        xi_scr[:, j * ns:(j + 1) * ns] = xi[j]
    xr_out[...] = xr_scr[...]
    xi_out[...] = xi_scr[...]


def _s5(h, w_u, bb, cm, abr, abi, dsk, wglu, x0r, x0i, *, nb, steps, sub=1):
    g, seq, d = h.shape
    rows = nb * steps
    blk = sub * rows // g
    assert seq % blk == 0 and (g == nb or steps == 1)
    su = w_u.shape[1]
    nst = abr.shape[1]
    r = jnp.arange(rows)
    to_tm = (r[:, None] % nb) * steps + r[:, None] // nb == r[None, :]
    perm = jnp.stack([to_tm, to_tm.T]).astype(BF16)
    return pl.pallas_call(
        functools.partial(_s5_body, nb=nb, steps=steps, sub=sub),
        grid=(seq // blk,),
        in_specs=[
            pl.BlockSpec((g, blk, d), lambda c: (0, c, 0)),
            _const_spec(perm.shape),
            _const_spec(w_u.shape), _const_spec(bb.shape), _const_spec(cm.shape),
            _const_spec(abr.shape), _const_spec(abi.shape), _const_spec(dsk.shape),
            _const_spec(wglu.shape), _const_spec(x0r.shape), _const_spec(x0i.shape),
        ],
        out_specs=[pl.BlockSpec((g, blk, su), lambda c: (0, c, 0)),
                   pl.BlockSpec((nb, nst), lambda c: (0, 0)),
                   pl.BlockSpec((nb, nst), lambda c: (0, 0))],
        out_shape=[jax.ShapeDtypeStruct((g, seq, su), BF16),
                   jax.ShapeDtypeStruct((nb, nst), F32),
                   jax.ShapeDtypeStruct((nb, nst), F32)],
        scratch_shapes=[pltpu.VMEM((sub, S5_BLOCKS, rows, 2 * nst // S5_BLOCKS), F32),
                        pltpu.VMEM((nb, nst), F32), pltpu.VMEM((nb, nst), F32)],
        compiler_params=_cparams(1),
        name="s5_branch",
    )(h, perm, w_u, bb, cm, abr, abi, dsk, wglu, x0r, x0i)


def _headsum(x, e):
    xb = x.astype(BF16)
    t = MXU_TILE_V7X
    return jnp.concatenate([_dot(xb[:, c:c + t], e[c:c + t, c:c + t]) for c in range(0, RW, t)], axis=1)


def _split3(x):
    hi = x.astype(BF16)
    r1 = x - hi.astype(F32)
    mid = r1.astype(BF16)
    lo = (r1 - mid.astype(F32)).astype(BF16)
    return hi, mid, lo


def _rwkv_pre(fm, w):
    r = fm[:, 0:RW]
    k = fm[:, RW:2 * RW]
    v = fm[:, 2 * RW:3 * RW]
    o0 = 3 * RW
    nw = w['w_up'].shape[0]
    na = w['a_up'].shape[0]
    wd = fm[:, o0:o0 + nw]
    ad = fm[:, o0 + nw:o0 + nw + na]
    gd = fm[:, o0 + nw + na:]
    w_log = -jax.nn.softplus(-(w['w0'][...] + _dot(jnp.tanh(wd).astype(BF16), w['w_up'][...]))) - 0.5
    logw = -jnp.exp(w_log)
    a = jax.nn.sigmoid(w['a0'][...] + _dot(ad.astype(BF16), w['a_up'][...]))
    g = _dot(jax.nn.sigmoid(gd).astype(BF16), w['g_up'][...])
    kk = k * w['k_k'][...]
    kk = kk * lax.rsqrt(jnp.maximum(_headsum(kk * kk, w['e']), 1e-24))
    k2 = k * (1.0 + (a - 1.0) * w['k_a'][...])
    return r, logw, k2, v, kk, kk * a, g


def _rwkv_post(o, r, k2, v, g, w):
    inv_n = 1.0 / HEAD
    mu = _headsum(o, w['e']) * inv_n
    d = o - mu
    var = _headsum(d * d, w['e']) * inv_n
    on = d * lax.rsqrt(var + LNX_EPS) * w['lnx_w'][...] + w['lnx_b'][...]
    bonus = _headsum(r * k2 * w['r_k'][...], w['e']) * v
    return (on + bonus) * g


def _sigmoid(x):
    return 0.5 * jnp.tanh(0.5 * x) + 0.5


def _merge_gates(h, y_b, w, c0=0, c1=None):
    dm = w['w_g'].shape[1] // 2
    c1 = dm if c1 is None else c1
    gate_a = _sigmoid(_dot(h, w['w_g'][:, c0:c1]))
    gate_b = _sigmoid(_dot(h, w['w_g'][:, dm + c0:dm + c1]))
    return gate_a, gate_b * _dot(y_b.astype(BF16), w['w_b_up'][:, c0:c1])


def _merge_out(x1, y_a, gate_a, gated_b, w):
    merged = gate_a * _dot(y_a.astype(BF16), w['w_a_up'][...]) + gated_b
    return x1 + _dot(merged.astype(BF16), w['w_out'][...])


_W_NAMES = ('w_f', 'w_g', 'mu', 'w0', 'w_up', 'a0', 'a_up', 'g_up', 'k_k', 'k_a', 'r_k',
            'lnx_w', 'lnx_b', 'e', 'w_a_up', 'w_b_up', 'w_out')


def _mixer_prompt_body(*refs, tc):
    h_ref, x1_ref, yb_ref = refs[0:3]
    nwt = len(_W_NAMES)
    w = dict(zip(_W_NAMES, refs[3:3 + nwt]))
    tri_ref, mask_ref = refs[3 + nwt:5 + nwt]
    x2_ref, shift_ref, wkv_ref = refs[5 + nwt:8 + nwt]
    f_scr, st_scr, o_scr, ga_scr, gb_scr = refs[8 + nwt:]
    c = pl.program_id(1)

    @pl.when(c == 0)
    def _():
        f_scr[...] = jnp.zeros_like(f_scr)
        st_scr[...] = jnp.zeros_like(st_scr)

    h = h_ref[...]
    f = _dot(h, w['w_f'][...])
    row0 = lax.broadcasted_iota(jnp.int32, f.shape, 0) == 0
    shifted = jnp.where(row0, f_scr[...], pltpu.roll(f, 1, 0))
    f_scr[...] = f[tc - 1:tc, :]
    fm = f + w['mu'][...] * (shifted - f)
    r, logw, k2, v, kk, b, g = _rwkv_pre(fm, w)

    ns = tc // CHUNK
    dm = x1_ref.shape[1]
    yb = yb_ref[...]
    tri = tri_ref[...]
    mask = mask_ref[...] > 0.5
    pw = 2 * HEAD
    lane = lambda rows: lax.broadcasted_iota(jnp.int32, (rows, pw), 1)
    low = lambda a: lane(a.shape[0]) < HEAD
    eye2 = (lax.broadcasted_iota(jnp.int32, (CHUNK, pw), 0) == lane(CHUNK) % HEAD).astype(F32)
    n_double = CHUNK.bit_length() - 2
    pairs = range(PAIRS)
    first = lambda a: jnp.where(low(a), a, jnp.zeros_like(a))
    second = lambda a: jnp.where(low(a), jnp.zeros_like(a), a)
    bd = lambda y: jnp.concatenate([first(y), second(y)], axis=0)
    ad = lambda y: jnp.concatenate([second(y), first(y)], axis=0)
    pick = lambda x, y: jnp.where(low(x), x, y)
    stack = lambda top, bot: jnp.concatenate([top.astype(BF16), bot.astype(BF16)], axis=0)
    bf = lambda a: a.astype(BF16)
    zc = jnp.zeros((CHUNK, pw), BF16)

    def chunk_setup(s):
        sl = slice(s * CHUNK, (s + 1) * CHUNK)
        lw = logw[sl]
        hi, mid, lo = _split3(lw)
        lpre = _dot(tri, hi) + _dot(tri, mid) + _dot(tri, lo)
        lsuf = lpre[CHUNK - 1:CHUNK, :] - lpre
        e_pre = jnp.exp(lpre)
        e_neg = jnp.exp(-lpre)
        e_suf = jnp.exp(lsuf)
        lhs = jnp.concatenate([kk[sl] * jnp.exp(lpre - lw), r[sl] * e_pre], axis=0).astype(BF16)
        rhs = jnp.concatenate([b[sl] * e_neg, k2[sl] * e_neg], axis=0).astype(BF16)
        sfx_t = jnp.concatenate([b[sl] * e_suf, k2[sl] * e_suf], axis=0).T.astype(BF16)
        every = ns * MXU_TILE_V7X // dm
        if s % every == 0:
            c0 = s // every * MXU_TILE_V7X
            ga_scr[:, c0:c0 + MXU_TILE_V7X], gb_scr[:, c0:c0 + MXU_TILE_V7X] = _merge_gates(
                h, yb, w, c0, c0 + MXU_TILE_V7X)
        return lhs, rhs, sfx_t, v[sl].astype(BF16), e_pre[CHUNK - 1:CHUNK, :]

    def chunk_units(setups):
        units = [(s, pr) for s in setups for pr in pairs]
        cut = lambda a, pr: a[:, pr * pw:(pr + 1) * pw]
        lhs = {u: cut(setups[u[0]][0], u[1]) for u in units}
        rhs = {u: cut(setups[u[0]][1], u[1]) for u in units}
        sc = {u: _dot_nt(lhs[u], jnp.concatenate(
            [first(rhs[u]), second(rhs[u][CHUNK:]), second(rhs[u][:CHUNK])], axis=0)) for u in units}
        a0 = {u: jnp.where(mask, sc[u][:, :pw], 0.0) for u in units}
        a1 = {u: jnp.where(mask, sc[u][:, pw:], 0.0) for u in units}
        m_b = {u: pick(a0[u][:CHUNK], a1[u][:CHUNK]) for u in units}
        m_ks = {u: pick(a1[u][:CHUNK], a0[u][:CHUNK]) for u in units}
        n_b = {u: pick(a0[u][CHUNK:], a1[u][CHUNK:]) for u in units}
        n_ks = {u: pick(a1[u][CHUNK:], a0[u][CHUNK:]) for u in units}
        p = {u: -m_b[u] for u in units}
        winv = {u: eye2 + p[u] for u in units}
        p = {u: _dot(bf(p[u]), bd(bf(p[u]))) for u in units}
        for _ in range(n_double - 1):
            wp = {u: _dot(stack(winv[u], p[u]), bd(bf(p[u]))) for u in units}
            winv = {u: winv[u] + wp[u][:CHUNK] for u in units}
            p = {u: wp[u][CHUNK:] for u in units}
        wb = {u: bf(winv[u] + _dot(bf(winv[u]), bd(bf(p[u])))) for u in units}
        adv = {u: ad(cut(setups[u[0]][3], u[1])) for u in units}
        mkv = {u: bf(_dot(bf(m_ks[u]), adv[u])) for u in units}
        gu = {u: _dot(wb[u], jnp.concatenate([bd(lhs[u][:CHUNK]), bd(mkv[u])], axis=1)) for u in units}
        bdg = {u: bd(bf(gu[u][:, :pw])) for u in units}
        bdu = {u: bd(bf(-gu[u][:, pw:])) for u in units}
        q = {u: lhs[u][CHUNK:].astype(F32) - _dot(bf(n_b[u]), bdg[u]) for u in units}
        o0 = {u: _dot(jnp.concatenate([bf(n_b[u]), bf(n_ks[u])], axis=1),
                      jnp.concatenate([bdu[u], adv[u]], axis=0)) for u in units}
        sfx_t = {u: jnp.concatenate([setups[u[0]][2][(2 * u[1] + i) * HEAD:(2 * u[1] + i + 1) * HEAD, :]
                                     for i in range(2)], axis=1)
                 for u in units}
        gv = {u: jnp.concatenate(
            [jnp.concatenate([bdg[u][:CHUNK], bdu[u][:CHUNK]], axis=1),
             jnp.concatenate([zc, adv[u][CHUNK:]], axis=1),
             jnp.concatenate([bdg[u][CHUNK:], bdu[u][CHUNK:]], axis=1),
             jnp.concatenate([zc, adv[u][:CHUNK]], axis=1)], axis=0) for u in units}
        sg = {u: _dot(sfx_t[u], gv[u]) for u in units}
        qphi = {u: stack(q[u], eye2 * cut(setups[u[0]][4], u[1]) - sg[u][:, :pw]) for u in units}
        opsi = {u: jnp.concatenate([o0[u], sg[u][:, pw:]], axis=0) for u in units}
        return qphi, opsi

    qphi, opsi = chunk_units({s: chunk_setup(s) for s in range(ns)})
    st =[st_scr[pr] for pr in pairs]
    for s in range(ns):
        os = [_dot(qphi[s, pr], bd(bf(st[pr]))) + opsi[s, pr] for pr in pairs]
        for pr in pairs:
            o_scr[s * CHUNK:(s + 1) * CHUNK, pr * pw:(pr + 1) * pw] = os[pr][:CHUNK]
        st = [os[pr][CHUNK:] for pr in pairs]
    for pr in pairs:
        st_scr[pr] = st[pr]

    y_a = _rwkv_post(o_scr[...], r, k2, v, g, w)
    x2_ref[...] = _merge_out(x1_ref[...], y_a, ga_scr[...], gb_scr[...], w)

    @pl.when(c == pl.num_programs(1) - 1)
    def _():
        shift_ref[...] = f[tc - 1:tc, :]
        for hd in range(HEADS):
            wkv_ref[hd] = st_scr[hd // 2][:, (hd % 2) * HEAD:(hd % 2 + 1) * HEAD].T


def _mixer_prompt(h, x1, yb, wts, tri, mask, *, tc):
    bsz, seq, d = h.shape
    assert seq % tc == 0 and tc % CHUNK == 0
    nf = wts['w_f'].shape[1]
    wlist = [wts[n] for n in _W_NAMES]
    tok = lambda width: pl.BlockSpec((None, tc, width), lambda b, c: (b, c, 0))
    return pl.pallas_call(
        functools.partial(_mixer_prompt_body, tc=tc),
        grid=(bsz, seq // tc),
        in_specs=[tok(d), tok(d), tok(RW)] + [_const_spec(a.shape) for a in wlist]
                 + [_const_spec(tri.shape), _const_spec(mask.shape)],
        out_specs=[tok(d),
                   pl.BlockSpec((None, 1, nf), lambda b, c: (b, 0, 0)),
                   pl.BlockSpec((None, HEADS, HEAD, HEAD), lambda b, c: (b, 0, 0, 0))],
        out_shape=[jax.ShapeDtypeStruct((bsz, seq, d), F32),
                   jax.ShapeDtypeStruct((bsz, 1, nf), F32),
                   jax.ShapeDtypeStruct((bsz, HEADS, HEAD, HEAD), F32)],
        scratch_shapes=[pltpu.VMEM((1, nf), F32),
                        pltpu.VMEM((PAIRS, HEAD, 2 * HEAD), F32),
                        pltpu.VMEM((tc, RW), F32),
                        pltpu.VMEM((tc, d), F32), pltpu.VMEM((tc, d), F32)],
        compiler_params=_cparams(2),
        name="mixer_prompt",
    )(h, x1, yb, *wlist, tri, mask)


def _mixer_sample_body(*refs):
    h_ref, x1_ref, yb_ref, shift_in_ref, s_ref = refs[0:5]
    nwt = len(_W_NAMES)
    w = dict(zip(_W_NAMES, refs[5:5 + nwt]))
    x2_ref, shift_ref, s_out_ref = refs[5 + nwt:8 + nwt]
    t_scr, row_scr, ot_scr = refs[8 + nwt:]
    hd = pl.program_id(0)

    @pl.when(hd == 0)
    def _():
        f = _dot(h_ref[...], w['w_f'][...])
        fm = f + w['mu'][...] * (shift_in_ref[...] - f)
        shift_ref[...] = f
        r, logw, k2, v, kk, b, g = _rwkv_pre(fm, w)
        for i, a in enumerate((kk, jnp.exp(logw), b, k2, r, v)):
            t_scr[i] = a.T
        for i, a in enumerate((r, k2, v, g)):
            row_scr[i] = a

    off = pl.multiple_of(hd * HEAD, HEAD)
    hs = pl.ds(off, HEAD)
    kk_t, dec_t, b_t, k_t, r_t = (t_scr[i, hs, :] for i in range(5))

    def v_group(vg, carry):
        v0 = pl.multiple_of(vg * 8, 8)
        s8 = s_ref[pl.ds(v0, 8)]
        v8 = t_scr[5, pl.ds(off + v0, 8), :]
        o_rows = []
        for j in range(8):
            sa = -jnp.sum(s8[j] * kk_t, axis=0, keepdims=True)
            sn = s8[j] * dec_t + sa * b_t + v8[j:j + 1, :] * k_t
            s_out_ref[v0 + j] = sn
            o_rows.append(jnp.sum(sn * r_t, axis=0, keepdims=True))
        ot_scr[pl.ds(off + v0, 8), :] = jnp.concatenate(o_rows, axis=0)
        return carry

    lax.fori_loop(0, HEAD // 8, v_group, 0)

    @pl.when(hd == pl.num_programs(0) - 1)
    def _():
        r, k2, v, g = (row_scr[i] for i in range(4))
        y_a = _rwkv_post(ot_scr[...].T, r, k2, v, g, w)
        gate_a, gated_b = _merge_gates(h_ref[...], yb_ref[...], w)
        x2_ref[...] = _merge_out(x1_ref[...], y_a, gate_a, gated_b, w)


def _mixer_sample(h, x1, yb, shift_in, wkv_t, wts):
    n, d = h.shape
    nf = wts['w_f'].shape[1]
    wlist = [wts[nm] for nm in _W_NAMES]
    st_spec = pl.BlockSpec((None, HEAD, HEAD, n), lambda hd: (hd, 0, 0, 0))
    return pl.pallas_call(
        _mixer_sample_body,
        grid=(HEADS,),
        in_specs=[_const_spec(a.shape) for a in (h, x1, yb, shift_in)] + [st_spec]
                 + [_const_spec(a.shape) for a in wlist],
        out_specs=[pl.BlockSpec((n, d), lambda hd: (0, 0)),
                   pl.BlockSpec((n, nf), lambda hd: (0, 0)), st_spec],
        out_shape=[jax.ShapeDtypeStruct((n, d), F32),
                   jax.ShapeDtypeStruct((n, nf), F32),
                   jax.ShapeDtypeStruct(wkv_t.shape, F32)],
        scratch_shapes=[pltpu.VMEM((6, RW, n), F32), pltpu.VMEM((4, n, RW), F32),
                        pltpu.VMEM((RW, n), F32)],
        compiler_params=_cparams(1),
        name="mixer_sample",
    )(h, x1, yb, shift_in, wkv_t, *wlist)


def _chunk_constants():
    t = jnp.arange(CHUNK)
    incl = (t[:, None] >= t[None, :])
    strict = (t[:, None] > t[None, :])
    tri = incl.astype(BF16)
    top = jnp.concatenate([strict, strict], axis=1)
    bot = jnp.concatenate([incl, incl], axis=1)
    mask = jnp.concatenate([top, bot], axis=0).astype(F32)
    return tri, mask


def kernel(x_prompt, x_sample, state_shift, state_wkv, state_s5_re, state_s5_im, g_ffn1, ffn1_gate, ffn1_up, ffn1_down, g_mix, w_in, mu_shift, w0, w_up, a0, a_up, g_up, k_k, k_a, r_k, lnx_w, lnx_b, A_re, A_im, log_dt, B_re, B_im, C_re, C_im, D_skip, w_glu, w_a_up, w_b_up, w_out, g_ffn2, ffn2_gate, ffn2_up, ffn2_down, g_final):
    depth = g_ffn1.shape[0]
    assert depth == 1
    bp, seq, d = x_prompt.shape
    bs = x_sample.shape[0]
    assert x_sample.shape[1] == 1
    ncols_f = mu_shift.shape[1]
    su = D_skip.shape[1]
    groups, pstate = A_re.shape[1:]
    nstate = groups * pstate
    bf = lambda a: a.astype(BF16)
    row = lambda a: a.reshape(1, -1).astype(F32)

    w_in0 = w_in[0]
    eye_h = jnp.eye(HEADS, dtype=F32)
    wts = {
        'w_f': bf(w_in0[:, :ncols_f]),
        'w_g': bf(w_in0[:, ncols_f + su:]),
        'mu': row(mu_shift[0]), 'w0': row(w0[0]), 'w_up': bf(w_up[0]), 'a0': row(a0[0]),
        'a_up': bf(a_up[0]), 'g_up': bf(g_up[0]), 'k_k': row(k_k[0]), 'k_a': row(k_a[0]),
        'r_k': row(r_k[0]), 'lnx_w': row(lnx_w[0]), 'lnx_b': row(lnx_b[0]),
        'e': bf(jnp.kron(eye_h, jnp.ones((HEAD, HEAD), F32))),
        'w_a_up': bf(w_a_up[0]), 'w_b_up': bf(w_b_up[0]), 'w_out': bf(w_out[0]),
    }
    w_u = bf(w_in0[:, ncols_f:ncols_f + su])
    abr, abi, bbr, bbi = _s5_params(A_re[0], A_im[0], log_dt[0],
                                    jnp.swapaxes(B_re[0], 1, 2), jnp.swapaxes(B_im[0], 1, 2))
    bb = bf(jnp.concatenate([_block_diag(bbr, S5_BLOCKS), _block_diag(bbi, S5_BLOCKS)], axis=2))
    cm = bf(jnp.concatenate([_block_diag(jnp.swapaxes(C_re[0], 1, 2), S5_BLOCKS),
                             _block_diag(-jnp.swapaxes(C_im[0], 1, 2), S5_BLOCKS)], axis=1))
    abr = abr.reshape(1, nstate)
    abi = abi.reshape(1, nstate)
    dsk = row(D_skip[0])
    wglu = bf(w_glu[0])
    ffn1 = (row(g_ffn1[0]), ffn1_gate[0], ffn1_up[0], ffn1_down[0], row(g_mix[0]))
    ffn2 = (row(g_ffn2[0]), ffn2_gate[0], ffn2_up[0], ffn2_down[0], row(g_final))
    fc = 2 * MXU_TILE_V7X
    tri, mask = _chunk_constants()

    x1p, hnp, x1s, hns = _ffn(x_prompt.reshape(bp * seq, d), x_sample.reshape(bs, d), *ffn1,
                              final=False, tm=512, fc=fc)

    hnp = hnp.reshape(bp, seq, d)
    zeros_state = jnp.zeros((bp, nstate), F32)
    yb_p, s5r_p, s5i_p = _s5(hnp, w_u, bb, cm, abr, abi, dsk, wglu, zeros_state, zeros_state,
                             nb=bp, steps=32, sub=4)
    x2p, shift_p, wkv_p = _mixer_prompt(hnp, x1p.reshape(bp, seq, d), yb_p, wts, tri, mask, tc=512)

    yb_s, s5r_s, s5i_s = _s5(hns[None], w_u, bb, cm, abr, abi, dsk, wglu,
                             state_s5_re[0].reshape(bs, nstate), state_s5_im[0].reshape(bs, nstate),
                             nb=bs, steps=1)
    wkv_t = jnp.transpose(state_wkv[0], (1, 2, 3, 0))
    x2s, shift_s, wkv_ts = _mixer_sample(hns, x1s, yb_s[0], state_shift[0], wkv_t, wts)
    wkv_s = jnp.transpose(wkv_ts, (3, 0, 1, 2))

    y_prompt, y_sample = _ffn(x2p.reshape(bp * seq, d), x2s, *ffn2, final=True, tm=512, fc=fc)
    y_prompt = y_prompt.reshape(bp, seq, d)
    y_sample = y_sample.reshape(bs, 1, d)

    st5 = lambda a, n: a.reshape(1, n, groups, pstate)
    return (y_prompt, y_sample,
            shift_p.reshape(1, bp, ncols_f), wkv_p[None], st5(s5r_p, bp), st5(s5i_p, bp),
            shift_s[None], wkv_s[None], st5(s5r_s, bs), st5(s5i_s, bs))
```

```python
import functools
import math

import jax
import jax.numpy as jnp
from jax import lax
from jax.experimental import pallas as pl
from jax.experimental.pallas import tpu as pltpu

F32 = jnp.float32
BF16 = jnp.bfloat16

NORM_EPS = 1e-6
LNX_EPS = 64e-5
HEAD = 64
HEADS = 8
PAIRS = HEADS // 2
RW = HEAD * HEADS
S5_BLOCKS = 4
CHUNK = 64
VMEM_LIMIT = 56 * 1024 * 1024
MXU_TILE_V7X = 256


def _dot(a, b):
    return jnp.dot(a, b, preferred_element_type=F32)


def _dot_nt(a, b):
    return lax.dot_general(a, b, (((1,), (1,)), ((), ())), preferred_element_type=F32)


def _rms(x, g):
    ms = jnp.mean(x * x, axis=-1, keepdims=True)
    return x * lax.rsqrt(ms + NORM_EPS) * g


def _const_spec(shape):
    nd = len(shape)
    return pl.BlockSpec(shape, lambda *_: (0,) * nd, pipeline_mode=pl.Buffered(1))


def _cparams(ngrid):
    return pltpu.CompilerParams(dimension_semantics=("arbitrary",) * ngrid,
                                vmem_limit_bytes=VMEM_LIMIT)


def _ffn_body(x_ref, xs_ref, g_ref, wg_ref, wu_ref, wd_ref, gn_ref, *outs, final, fc):
    nout = len(outs) // 2

    def run(src_ref, dst):
        x = src_ref[...]
        hn = _rms(x, g_ref[...]).astype(BF16)
        acc = None
        f = wg_ref.shape[1]
        for c0 in range(0, f, fc):
            c1 = min(c0 + fc, f)
            gate = _dot(hn, wg_ref[:, c0:c1].astype(BF16))
            up = _dot(hn, wu_ref[:, c0:c1].astype(BF16))
            act = (jax.nn.silu(gate) * up).astype(BF16)
            part = _dot(act, wd_ref[c0:c1, :].astype(BF16))
            acc = part if acc is None else acc + part
        x1 = x + 0.5 * acc
        if final:
            dst[0][...] = _rms(x1, gn_ref[...])
        else:
            dst[0][...] = x1
            dst[1][...] = _rms(x1, gn_ref[...]).astype(BF16)

    last = pl.num_programs(0) - 1
    pl.when(pl.program_id(0) < last)(functools.partial(run, x_ref, outs[:nout]))
    pl.when(pl.program_id(0) == last)(functools.partial(run, xs_ref, outs[nout:]))


def _ffn(x, xs, g, wg, wu, wd, gn, *, final, tm, fc):
    m, d = x.shape
    assert m % tm == 0 and fc % MXU_TILE_V7X == 0
    nt = m // tm
    tile = pl.BlockSpec((tm, d), lambda i: (jnp.minimum(i, nt - 1), 0))
    whole = pl.BlockSpec(xs.shape, lambda i: (0, 0))
    dts = (F32,) if final else (F32, BF16)
    return pl.pallas_call(
        functools.partial(_ffn_body, final=final, fc=fc),
        grid=(nt + 1,),
        in_specs=[tile, _const_spec(xs.shape), _const_spec(g.shape), _const_spec(wg.shape),
                  _const_spec(wu.shape), _const_spec(wd.shape), _const_spec(gn.shape)],
        out_specs=[tile] * len(dts) + [whole] * len(dts),
        out_shape=[jax.ShapeDtypeStruct(x.shape, dt) for dt in dts]
                  + [jax.ShapeDtypeStruct(xs.shape, dt) for dt in dts],
        compiler_params=_cparams(1),
        name="ffn_final" if final else "ffn_in",
    )(x, xs, g, wg, wu, wd, gn)


def _s5_param_body(are_ref, aim_ref, ldt_ref, bre_ref, bim_ref, abr_ref, abi_ref, bbr_ref, bbi_ref):
    lam_r = are_ref[...]
    lam_i = aim_ref[...]
    dt = jnp.exp(ldt_ref[...])
    mag = jnp.exp(lam_r * dt)
    ab_r = mag * jnp.cos(lam_i * dt)
    ab_i = mag * jnp.sin(lam_i * dt)
    den = lam_r * lam_r + lam_i * lam_i
    q_r = ((ab_r - 1.0) * lam_r + ab_i * lam_i) / den
    q_i = (ab_i * lam_r - (ab_r - 1.0) * lam_i) / den
    abr_ref[...] = ab_r
    abi_ref[...] = ab_i
    bre = bre_ref[...]
    bim = bim_ref[...]
    bbr_ref[...] = q_r[:, None, :] * bre - q_i[:, None, :] * bim
    bbi_ref[...] = q_r[:, None, :] * bim + q_i[:, None, :] * bre


def _s5_params(a_re, a_im, log_dt, b_re_t, b_im_t):
    g, p = a_re.shape
    hg = b_re_t.shape[1]
    return pl.pallas_call(
        _s5_param_body,
        out_shape=[jax.ShapeDtypeStruct((g, p), F32), jax.ShapeDtypeStruct((g, p), F32),
                   jax.ShapeDtypeStruct((g, hg, p), F32), jax.ShapeDtypeStruct((g, hg, p), F32)],
        name="s5_params",
    )(a_re, a_im, log_dt.reshape(g, 1), b_re_t, b_im_t)


def _block_diag(w, nblk):
    g, a, b = w.shape
    gb = g // nblk
    w4 = w.reshape(nblk, gb, a, b)
    eye = jnp.eye(gb, dtype=w.dtype)
    return jnp.einsum('jgab,gk->jgakb', w4, eye).reshape(nblk, gb * a, gb * b)


def _s5_body(h_ref, perm_ref, wu_ref, bb_ref, cm_ref, abr_ref, abi_ref, dsk_ref, wglu_ref, x0r_ref,
             x0i_ref, yb_ref, xr_out, xi_out, bu_scr, xr_scr, xi_scr, *, nb, steps, sub):
    c = pl.program_id(0)
    rows = nb * steps
    blk = h_ref.shape[1] // sub

    @pl.when(c == 0)
    def _():
        xr_scr[...] = x0r_ref[...]
        xi_scr[...] = x0i_ref[...]

    nin = wu_ref.shape[1] // S5_BLOCKS
    ns = abr_ref.shape[1] // S5_BLOCKS
    us = []
    for i in range(sub):
        h = h_ref[:, i * blk:(i + 1) * blk, :].reshape(rows, h_ref.shape[-1])
        if steps > 1:
            h = _dot(perm_ref[0], h).astype(BF16)
        u = _dot(h, wu_ref[...])
        us.append(u)
        ub = u.astype(BF16)
        for j in range(S5_BLOCKS):
            bu_scr[i, j] = _dot(ub[:, j * nin:(j + 1) * nin], bb_ref[j])
    xr = [xr_scr[:, j * ns:(j + 1) * ns] for j in range(S5_BLOCKS)]
    xi = [xi_scr[:, j * ns:(j + 1) * ns] for j in range(S5_BLOCKS)]
    for i in range(sub):
        ys = []
        for j in range(S5_BLOCKS):
            ar = jnp.broadcast_to(abr_ref[:, j * ns:(j + 1) * ns], (nb, ns))
            ai = jnp.broadcast_to(abi_ref[:, j * ns:(j + 1) * ns], (nb, ns))
            for t in range(steps):
                tr = slice(t * nb, (t + 1) * nb)
                xr[j], xi[j] = (ar * xr[j] - ai * xi[j] + bu_scr[i, j, tr, 0:ns],
                                ar * xi[j] + ai * xr[j] + bu_scr[i, j, tr, ns:2 * ns])
                bu_scr[i, j, tr, 0:ns] = xr[j]
                bu_scr[i, j, tr, ns:2 * ns] = xi[j]
            ys.append(_dot(bu_scr[i, j].astype(BF16), cm_ref[j]))
        y = jnp.concatenate(ys, axis=1) + dsk_ref[...] * us[i]
        z = jax.nn.gelu(y)
        yb = (z * jax.nn.sigmoid(_dot(z.astype(BF16), wglu_ref[...]))).astype(BF16)
        if steps > 1:
            yb = _dot(perm_ref[1], yb).astype(BF16)
        yb_ref[:, i * blk:(i + 1) * blk, :] = yb.reshape(yb_ref.shape[0], blk, yb_ref.shape[2])
    for j in range(S5_BLOCKS):
        xr_scr[:, j * ns:(j + 1) * ns] = xr[j]
        xi_scr[:, j * ns:(j + 1) * ns] = xi[j]
    xr_out[...] = xr_scr[...]
    xi_out[...] = xi_scr[...]


def _s5(h, w_u, bb, cm, abr, abi, dsk, wglu, x0r, x0i, *, nb, steps, sub=1):
    g, seq, d = h.shape
    rows = nb * steps
    blk = sub * rows // g
    assert seq % blk == 0 and (g == nb or steps == 1)
    su = w_u.shape[1]
    nst = abr.shape[1]
    r = jnp.arange(rows)
    to_tm = (r[:, None] % nb) * steps + r[:, None] // nb == r[None, :]
    perm = jnp.stack([to_tm, to_tm.T]).astype(BF16)
    return pl.pallas_call(
        functools.partial(_s5_body, nb=nb, steps=steps, sub=sub),
        grid=(seq // blk,),
        in_specs=[
            pl.BlockSpec((g, blk, d), lambda c: (0, c, 0)),
            _const_spec(perm.shape),
            _const_spec(w_u.shape), _const_spec(bb.shape), _const_spec(cm.shape),
            _const_spec(abr.shape), _const_spec(abi.shape), _const_spec(dsk.shape),
            _const_spec(wglu.shape), _const_spec(x0r.shape), _const_spec(x0i.shape),
        ],
        out_specs=[pl.BlockSpec((g, blk, su), lambda c: (0, c, 0)),
                   pl.BlockSpec((nb, nst), lambda c: (0, 0)),
                   pl.BlockSpec((nb, nst), lambda c: (0, 0))],
        out_shape=[jax.ShapeDtypeStruct((g, seq, su), BF16),
                   jax.ShapeDtypeStruct((nb, nst), F32),
                   jax.ShapeDtypeStruct((nb, nst), F32)],
        scratch_shapes=[pltpu.VMEM((sub, S5_BLOCKS, rows, 2 * nst // S5_BLOCKS), F32),
                        pltpu.VMEM((nb, nst), F32), pltpu.VMEM((nb, nst), F32)],
        compiler_params=_cparams(1),
        name="s5_branch",
    )(h, perm, w_u, bb, cm, abr, abi, dsk, wglu, x0r, x0i)


def _headsum(x, e):
    xb = x.astype(BF16)
    t = MXU_TILE_V7X
    return jnp.concatenate([_dot(xb[:, c:c + t], e[c:c + t, c:c + t]) for c in range(0, RW, t)], axis=1)


def _split3(x):
    hi = x.astype(BF16)
    r1 = x - hi.astype(F32)
    mid = r1.astype(BF16)
    lo = (r1 - mid.astype(F32)).astype(BF16)
    return hi, mid, lo


def _rwkv_pre(fm, w):
    r = fm[:, 0:RW]
    k = fm[:, RW:2 * RW]
    v = fm[:, 2 * RW:3 * RW]
    o0 = 3 * RW
    nw = w['w_up'].shape[0]
    na = w['a_up'].shape[0]
    wd = fm[:, o0:o0 + nw]
    ad = fm[:, o0 + nw:o0 + nw + na]
    gd = fm[:, o0 + nw + na:]
    logw = -math.exp(-0.5) * _sigmoid(w['w0'][...] + _dot(jnp.tanh(wd).astype(BF16), w['w_up'][...]))
    a = _sigmoid(w['a0'][...] + _dot(ad.astype(BF16), w['a_up'][...]))
    g = _dot(_sigmoid(gd).astype(BF16), w['g_up'][...])
    kk = k * w['k_k'][...]
    kk = kk * lax.rsqrt(jnp.maximum(_headsum(kk * kk, w['e']), 1e-24))
    k2 = k * (1.0 + (a - 1.0) * w['k_a'][...])
    return r, logw, k2, v, kk, kk * a, g


def _rwkv_post(o, r, k2, v, g, w):
    inv_n = 1.0 / HEAD
    mu = _headsum(o, w['e']) * inv_n
    d = o - mu
    var = _headsum(d * d, w['e']) * inv_n
    on = d * lax.rsqrt(var + LNX_EPS) * w['lnx_w'][...] + w['lnx_b'][...]
    bonus = _headsum(r * k2 * w['r_k'][...], w['e']) * v
    return (on + bonus) * g


def _sigmoid_of_twice(xh):
    return 0.5 * jnp.tanh(xh) + 0.5


def _sigmoid(x):
    return _sigmoid_of_twice(0.5 * x)


def _merge_gates(h, y_b, w, c0=0, c1=None):
    dm = w['w_gh'].shape[1] // 2
    c1 = dm if c1 is None else c1
    gate_a = _sigmoid_of_twice(_dot(h, w['w_gh'][:, c0:c1]))
    gate_b = _sigmoid_of_twice(_dot(h, w['w_gh'][:, dm + c0:dm + c1]))
    return gate_a, gate_b * _dot(y_b.astype(BF16), w['w_b_up'][:, c0:c1])


def _merge_out(x1, y_a, gate_a, gated_b, w):
    merged = gate_a * _dot(y_a.astype(BF16), w['w_a_up'][...]) + gated_b
    return x1 + _dot(merged.astype(BF16), w['w_out'][...])


_W_NAMES = ('w_f', 'w_gh', 'mu', 'w0', 'w_up', 'a0', 'a_up', 'g_up', 'k_k', 'k_a', 'r_k',
            'lnx_w', 'lnx_b', 'e', 'w_a_up', 'w_b_up', 'w_out')


def _mixer_prompt_body(*refs, tc):
    h_ref, x1_ref, yb_ref = refs[0:3]
    nwt = len(_W_NAMES)
    w = dict(zip(_W_NAMES, refs[3:3 + nwt]))
    tri_ref, mask_ref = refs[3 + nwt:5 + nwt]
    x2_ref, shift_ref, wkv_ref = refs[5 + nwt:8 + nwt]
    f_scr, st_scr, o_scr, ga_scr, gb_scr = refs[8 + nwt:]
    c = pl.program_id(1)

    @pl.when(c == 0)
    def _():
        f_scr[...] = jnp.zeros_like(f_scr)
        st_scr[...] = jnp.zeros_like(st_scr)

    h = h_ref[...]
    f = _dot(h, w['w_f'][...])
    row0 = lax.broadcasted_iota(jnp.int32, f.shape, 0) == 0
    shifted = jnp.where(row0, f_scr[...], pltpu.roll(f, 1, 0))
    f_scr[...] = f[tc - 1:tc, :]
    fm = f + w['mu'][...] * (shifted - f)
    r, logw, k2, v, kk, b, g = _rwkv_pre(fm, w)

    ns = tc // CHUNK
    dm = x1_ref.shape[1]
    yb = yb_ref[...]
    tri = tri_ref[...]
    mask = mask_ref[...] > 0.5
    pw = 2 * HEAD
    lane = lambda rows: lax.broadcasted_iota(jnp.int32, (rows, pw), 1)
    low = lambda a: lane(a.shape[0]) < HEAD
    eye2 = (lax.broadcasted_iota(jnp.int32, (CHUNK, pw), 0) == lane(CHUNK) % HEAD).astype(F32)
    n_double = CHUNK.bit_length() - 2
    pairs = range(PAIRS)
    first = lambda a: jnp.where(low(a), a, jnp.zeros_like(a))
    second = lambda a: jnp.where(low(a), jnp.zeros_like(a), a)
    bd = lambda y: jnp.concatenate([first(y), second(y)], axis=0)
    ad = lambda y: jnp.concatenate([second(y), first(y)], axis=0)
    pick = lambda x, y: jnp.where(low(x), x, y)
    stack = lambda top, bot: jnp.concatenate([top.astype(BF16), bot.astype(BF16)], axis=0)
    bf = lambda a: a.astype(BF16)
    zc = jnp.zeros((CHUNK, pw), BF16)

    def chunk_setup(s):
        sl = slice(s * CHUNK, (s + 1) * CHUNK)
        lw = logw[sl]
        hi, mid, lo = _split3(lw)
        lpre = _dot(tri, hi) + _dot(tri, mid) + _dot(tri, lo)
        lsuf = lpre[CHUNK - 1:CHUNK, :] - lpre
        e_pre = jnp.exp(lpre)
        e_neg = jnp.exp(-lpre)
        e_suf = jnp.exp(lsuf)
        lhs = jnp.concatenate([kk[sl] * jnp.exp(lpre - lw), r[sl] * e_pre], axis=0).astype(BF16)
        rhs = jnp.concatenate([b[sl] * e_neg, k2[sl] * e_neg], axis=0).astype(BF16)
        sfx_t = jnp.concatenate([b[sl] * e_suf, k2[sl] * e_suf], axis=0).T.astype(BF16)
        every = ns * MXU_TILE_V7X // dm
        if s % every == 0:
            c0 = s // every * MXU_TILE_V7X
            ga_scr[:, c0:c0 + MXU_TILE_V7X], gb_scr[:, c0:c0 + MXU_TILE_V7X] = _merge_gates(
                h, yb, w, c0, c0 + MXU_TILE_V7X)
        return lhs, rhs, sfx_t, v[sl].astype(BF16), e_pre[CHUNK - 1:CHUNK, :]

    def chunk_units(setups):
        units = [(s, pr) for s in setups for pr in pairs]
        cut = lambda a, pr: a[:, pr * pw:(pr + 1) * pw]
        lhs = {u: cut(setups[u[0]][0], u[1]) for u in units}
        rhs = {u: cut(setups[u[0]][1], u[1]) for u in units}
        sc = {u: _dot_nt(lhs[u], jnp.concatenate(
            [first(rhs[u]), second(rhs[u][CHUNK:]), second(rhs[u][:CHUNK])], axis=0)) for u in units}
        a0 = {u: jnp.where(mask, sc[u][:, :pw], 0.0) for u in units}
        a1 = {u: jnp.where(mask, sc[u][:, pw:], 0.0) for u in units}
        m_b = {u: pick(a0[u][:CHUNK], a1[u][:CHUNK]) for u in units}
        m_ks = {u: pick(a1[u][:CHUNK], a0[u][:CHUNK]) for u in units}
        n_b = {u: pick(a0[u][CHUNK:], a1[u][CHUNK:]) for u in units}
        n_ks = {u: pick(a1[u][CHUNK:], a0[u][CHUNK:]) for u in units}
        p = {u: -m_b[u] for u in units}
        winv = {u: eye2 + p[u] for u in units}
        p = {u: _dot(bf(p[u]), bd(bf(p[u]))) for u in units}
        for _ in range(n_double - 1):
            wp = {u: _dot(stack(winv[u], p[u]), bd(bf(p[u]))) for u in units}
            winv = {u: winv[u] + wp[u][:CHUNK] for u in units}
            p = {u: wp[u][CHUNK:] for u in units}
        wb = {u: bf(winv[u] + _dot(bf(winv[u]), bd(bf(p[u])))) for u in units}
        adv = {u: ad(cut(setups[u[0]][3], u[1])) for u in units}
        mkv = {u: bf(_dot(bf(m_ks[u]), adv[u])) for u in units}
        gu = {u: _dot(wb[u], jnp.concatenate([bd(lhs[u][:CHUNK]), bd(mkv[u])], axis=1)) for u in units}
        bdg = {u: bd(bf(gu[u][:, :pw])) for u in units}
        bdu = {u: bd(bf(-gu[u][:, pw:])) for u in units}
        q = {u: lhs[u][CHUNK:].astype(F32) - _dot(bf(n_b[u]), bdg[u]) for u in units}
        o0 = {u: _dot(jnp.concatenate([bf(n_b[u]), bf(n_ks[u])], axis=1),
                      jnp.concatenate([bdu[u], adv[u]], axis=0)) for u in units}
        sfx_t = {u: jnp.concatenate([setups[u[0]][2][(2 * u[1] + i) * HEAD:(2 * u[1] + i + 1) * HEAD, :]
                                     for i in range(2)], axis=1)
                 for u in units}
        gv = {u: jnp.concatenate(
            [jnp.concatenate([bdg[u][:CHUNK], bdu[u][:CHUNK]], axis=1),
             jnp.concatenate([zc, adv[u][CHUNK:]], axis=1),
             jnp.concatenate([bdg[u][CHUNK:], bdu[u][CHUNK:]], axis=1),
             jnp.concatenate([zc, adv[u][:CHUNK]], axis=1)], axis=0) for u in units}
        sg = {u: _dot(sfx_t[u], gv[u]) for u in units}
        qphi = {u: stack(q[u], eye2 * cut(setups[u[0]][4], u[1]) - sg[u][:, :pw]) for u in units}
        opsi = {u: jnp.concatenate([o0[u], sg[u][:, pw:]], axis=0) for u in units}
        return qphi, opsi

    qphi, opsi = chunk_units({s: chunk_setup(s) for s in range(ns)})
    st =[st_scr[pr] for pr in pairs]
    for s in range(ns):
        os = [_dot(qphi[s, pr], bd(bf(st[pr]))) + opsi[s, pr] for pr in pairs]
        for pr in pairs:
            o_scr[s * CHUNK:(s + 1) * CHUNK, pr * pw:(pr + 1) * pw] = os[pr][:CHUNK]
        st = [os[pr][CHUNK:] for pr in pairs]
    for pr in pairs:
        st_scr[pr] = st[pr]

    y_a = _rwkv_post(o_scr[...], r, k2, v, g, w)
    x2_ref[...] = _merge_out(x1_ref[...], y_a, ga_scr[...], gb_scr[...], w)

    @pl.when(c == pl.num_programs(1) - 1)
    def _():
        shift_ref[...] = f[tc - 1:tc, :]
        for hd in range(HEADS):
            wkv_ref[hd] = st_scr[hd // 2][:, (hd % 2) * HEAD:(hd % 2 + 1) * HEAD].T


def _mixer_prompt(h, x1, yb, wts, tri, mask, *, tc):
    bsz, seq, d = h.shape
    assert seq % tc == 0 and tc % CHUNK == 0
    nf = wts['w_f'].shape[1]
    wlist = [wts[n] for n in _W_NAMES]
    tok = lambda width: pl.BlockSpec((None, tc, width), lambda b, c: (b, c, 0))
    return pl.pallas_call(
        functools.partial(_mixer_prompt_body, tc=tc),
        grid=(bsz, seq // tc),
        in_specs=[tok(d), tok(d), tok(RW)] + [_const_spec(a.shape) for a in wlist]
                 + [_const_spec(tri.shape), _const_spec(mask.shape)],
        out_specs=[tok(d),
                   pl.BlockSpec((None, 1, nf), lambda b, c: (b, 0, 0)),
                   pl.BlockSpec((None, HEADS, HEAD, HEAD), lambda b, c: (b, 0, 0, 0))],
        out_shape=[jax.ShapeDtypeStruct((bsz, seq, d), F32),
                   jax.ShapeDtypeStruct((bsz, 1, nf), F32),
                   jax.ShapeDtypeStruct((bsz, HEADS, HEAD, HEAD), F32)],
        scratch_shapes=[pltpu.VMEM((1, nf), F32),
                        pltpu.VMEM((PAIRS, HEAD, 2 * HEAD), F32),
                        pltpu.VMEM((tc, RW), F32),
                        pltpu.VMEM((tc, d), F32), pltpu.VMEM((tc, d), F32)],
        compiler_params=_cparams(2),
        name="mixer_prompt",
    )(h, x1, yb, *wlist, tri, mask)


def _mixer_sample_body(*refs):
    h_ref, x1_ref, yb_ref, shift_in_ref, s_ref = refs[0:5]
    nwt = len(_W_NAMES)
    w = dict(zip(_W_NAMES, refs[5:5 + nwt]))
    x2_ref, shift_ref, s_out_ref = refs[5 + nwt:8 + nwt]
    t_scr, row_scr, ot_scr = refs[8 + nwt:]
    hd = pl.program_id(0)

    @pl.when(hd == 0)
    def _():
        f = _dot(h_ref[...], w['w_f'][...])
        fm = f + w['mu'][...] * (shift_in_ref[...] - f)
        shift_ref[...] = f
        r, logw, k2, v, kk, b, g = _rwkv_pre(fm, w)
        for i, a in enumerate((kk, jnp.exp(logw), b, k2, r, v)):
            t_scr[i] = a.T
        for i, a in enumerate((r, k2, v, g)):
            row_scr[i] = a

    off = pl.multiple_of(hd * HEAD, HEAD)
    hs = pl.ds(off, HEAD)
    kk_t, dec_t, b_t, k_t, r_t = (t_scr[i, hs, :] for i in range(5))

    def v_group(vg, carry):
        v0 = pl.multiple_of(vg * 8, 8)
        s8 = s_ref[pl.ds(v0, 8)]
        v8 = t_scr[5, pl.ds(off + v0, 8), :]
        o_rows = []
        for j in range(8):
            sa = -jnp.sum(s8[j] * kk_t, axis=0, keepdims=True)
            sn = s8[j] * dec_t + sa * b_t + v8[j:j + 1, :] * k_t
            s_out_ref[v0 + j] = sn
            o_rows.append(jnp.sum(sn * r_t, axis=0, keepdims=True))
        ot_scr[pl.ds(off + v0, 8), :] = jnp.concatenate(o_rows, axis=0)
        return carry

    lax.fori_loop(0, HEAD // 8, v_group, 0)

    @pl.when(hd == pl.num_programs(0) - 1)
    def _():
        r, k2, v, g = (row_scr[i] for i in range(4))
        y_a = _rwkv_post(ot_scr[...].T, r, k2, v, g, w)
        gate_a, gated_b = _merge_gates(h_ref[...], yb_ref[...], w)
        x2_ref[...] = _merge_out(x1_ref[...], y_a, gate_a, gated_b, w)


def _mixer_sample(h, x1, yb, shift_in, wkv_t, wts):
    n, d = h.shape
    nf = wts['w_f'].shape[1]
    wlist = [wts[nm] for nm in _W_NAMES]
    st_spec = pl.BlockSpec((None, HEAD, HEAD, n), lambda hd: (hd, 0, 0, 0))
    return pl.pallas_call(
        _mixer_sample_body,
        grid=(HEADS,),
        in_specs=[_const_spec(a.shape) for a in (h, x1, yb, shift_in)] + [st_spec]
                 + [_const_spec(a.shape) for a in wlist],
        out_specs=[pl.BlockSpec((n, d), lambda hd: (0, 0)),
                   pl.BlockSpec((n, nf), lambda hd: (0, 0)), st_spec],
        out_shape=[jax.ShapeDtypeStruct((n, d), F32),
                   jax.ShapeDtypeStruct((n, nf), F32),
                   jax.ShapeDtypeStruct(wkv_t.shape, F32)],
        scratch_shapes=[pltpu.VMEM((6, RW, n), F32), pltpu.VMEM((4, n, RW), F32),
                        pltpu.VMEM((RW, n), F32)],
        compiler_params=_cparams(1),
        name="mixer_sample",
    )(h, x1, yb, shift_in, wkv_t, *wlist)


def _chunk_constants():
    t = jnp.arange(CHUNK)
    incl = (t[:, None] >= t[None, :])
    strict = (t[:, None] > t[None, :])
    tri = incl.astype(BF16)
    top = jnp.concatenate([strict, strict], axis=1)
    bot = jnp.concatenate([incl, incl], axis=1)
    mask = jnp.concatenate([top, bot], axis=0).astype(F32)
    return tri, mask


def kernel(x_prompt, x_sample, state_shift, state_wkv, state_s5_re, state_s5_im, g_ffn1, ffn1_gate, ffn1_up, ffn1_down, g_mix, w_in, mu_shift, w0, w_up, a0, a_up, g_up, k_k, k_a, r_k, lnx_w, lnx_b, A_re, A_im, log_dt, B_re, B_im, C_re, C_im, D_skip, w_glu, w_a_up, w_b_up, w_out, g_ffn2, ffn2_gate, ffn2_up, ffn2_down, g_final):
    depth = g_ffn1.shape[0]
    assert depth == 1
    bp, seq, d = x_prompt.shape
    bs = x_sample.shape[0]
    assert x_sample.shape[1] == 1
    ncols_f = mu_shift.shape[1]
    su = D_skip.shape[1]
    groups, pstate = A_re.shape[1:]
    nstate = groups * pstate
    bf = lambda a: a.astype(BF16)
    row = lambda a: a.reshape(1, -1).astype(F32)

    w_in0 = w_in[0]
    eye_h = jnp.eye(HEADS, dtype=F32)
    wts = {
        'w_f': bf(w_in0[:, :ncols_f]),
        'w_gh': bf(0.5 * w_in0[:, ncols_f + su:]),
        'mu': row(mu_shift[0]), 'w0': row(w0[0]), 'w_up': bf(w_up[0]), 'a0': row(a0[0]),
        'a_up': bf(a_up[0]), 'g_up': bf(g_up[0]), 'k_k': row(k_k[0]), 'k_a': row(k_a[0]),
        'r_k': row(r_k[0]), 'lnx_w': row(lnx_w[0]), 'lnx_b': row(lnx_b[0]),
        'e': bf(jnp.kron(eye_h, jnp.ones((HEAD, HEAD), F32))),
        'w_a_up': bf(w_a_up[0]), 'w_b_up': bf(w_b_up[0]), 'w_out': bf(w_out[0]),
    }
    w_u = bf(w_in0[:, ncols_f:ncols_f + su])
    abr, abi, bbr, bbi = _s5_params(A_re[0], A_im[0], log_dt[0],
                                    jnp.swapaxes(B_re[0], 1, 2), jnp.swapaxes(B_im[0], 1, 2))
    bb = bf(jnp.concatenate([_block_diag(bbr, S5_BLOCKS), _block_diag(bbi, S5_BLOCKS)], axis=2))
    cm = bf(jnp.concatenate([_block_diag(jnp.swapaxes(C_re[0], 1, 2), S5_BLOCKS),
                             _block_diag(-jnp.swapaxes(C_im[0], 1, 2), S5_BLOCKS)], axis=1))
    abr = abr.reshape(1, nstate)
    abi = abi.reshape(1, nstate)
    dsk = row(D_skip[0])
    wglu = bf(w_glu[0])
    ffn1 = (row(g_ffn1[0]), ffn1_gate[0], ffn1_up[0], ffn1_down[0], row(g_mix[0]))
    ffn2 = (row(g_ffn2[0]), ffn2_gate[0], ffn2_up[0], ffn2_down[0], row(g_final))
    fc = 2 * MXU_TILE_V7X
    tri, mask = _chunk_constants()

    x1p, hnp, x1s, hns = _ffn(x_prompt.reshape(bp * seq, d), x_sample.reshape(bs, d), *ffn1,
                              final=False, tm=512, fc=fc)

    hnp = hnp.reshape(bp, seq, d)
    zeros_state = jnp.zeros((bp, nstate), F32)
    yb_p, s5r_p, s5i_p = _s5(hnp, w_u, bb, cm, abr, abi, dsk, wglu, zeros_state, zeros_state,
                             nb=bp, steps=32, sub=4)
    x2p, shift_p, wkv_p = _mixer_prompt(hnp, x1p.reshape(bp, seq, d), yb_p, wts, tri, mask, tc=512)

    yb_s, s5r_s, s5i_s = _s5(hns[None], w_u, bb, cm, abr, abi, dsk, wglu,
                             state_s5_re[0].reshape(bs, nstate), state_s5_im[0].reshape(bs, nstate),
                             nb=bs, steps=1)
    wkv_t = jnp.transpose(state_wkv[0], (1, 2, 3, 0))
    x2s, shift_s, wkv_ts = _mixer_sample(hns, x1s, yb_s[0], state_shift[0], wkv_t, wts)
    wkv_s = jnp.transpose(wkv_ts, (3, 0, 1, 2))

    y_prompt, y_sample = _ffn(x2p.reshape(bp * seq, d), x2s, *ffn2, final=True, tm=512, fc=fc)
    y_prompt = y_prompt.reshape(bp, seq, d)
    y_sample = y_sample.reshape(bs, 1, d)

    st5 = lambda a, n: a.reshape(1, n, groups, pstate)
    return (y_prompt, y_sample,
            shift_p.reshape(1, bp, ncols_f), wkv_p[None], st5(s5r_p, bp), st5(s5i_p, bp),
            shift_s[None], wkv_s[None], st5(s5r_s, bs), st5(s5i_s, bs))
```

```python
import functools
import math

import jax
import jax.numpy as jnp
from jax import lax
from jax.experimental import pallas as pl
from jax.experimental.pallas import tpu as pltpu

F32 = jnp.float32
BF16 = jnp.bfloat16

NORM_EPS = 1e-6
LNX_EPS = 64e-5
HEAD = 64
HEADS = 8
PAIRS = HEADS // 2
RW = HEAD * HEADS
S5_BLOCKS = 4
CHUNK = 64
VMEM_LIMIT = 56 * 1024 * 1024
MXU_TILE_V7X = 256


def _dot(a, b):
    return jnp.dot(a, b, preferred_element_type=F32)


def _dot_nt(a, b):
    return lax.dot_general(a, b, (((1,), (1,)), ((), ())), preferred_element_type=F32)


def _rms(x, g):
    ms = jnp.mean(x * x, axis=-1, keepdims=True)
    return x * lax.rsqrt(ms + NORM_EPS) * g


def _const_spec(shape):
    nd = len(shape)
    return pl.BlockSpec(shape, lambda *_: (0,) * nd, pipeline_mode=pl.Buffered(1))


def _cparams(ngrid):
    return pltpu.CompilerParams(dimension_semantics=("arbitrary",) * ngrid,
                                vmem_limit_bytes=VMEM_LIMIT)


def _ffn_body(x_ref, xs_ref, g_ref, wg_ref, wu_ref, wd_ref, gn_ref, *outs, final, fc):
    nout = len(outs) // 2

    def run(src_ref, dst):
        x = src_ref[...]
        hn = _rms(x, g_ref[...]).astype(BF16)
        acc = None
        f = wg_ref.shape[1]
        for c0 in range(0, f, fc):
            c1 = min(c0 + fc, f)
            gate = _dot(hn, wg_ref[:, c0:c1].astype(BF16))
            up = _dot(hn, wu_ref[:, c0:c1].astype(BF16))
            act = (jax.nn.silu(gate) * up).astype(BF16)
            part = _dot(act, wd_ref[c0:c1, :].astype(BF16))
            acc = part if acc is None else acc + part
        x1 = x + 0.5 * acc
        if final:
            dst[0][...] = _rms(x1, gn_ref[...])
        else:
            dst[0][...] = x1
            dst[1][...] = _rms(x1, gn_ref[...]).astype(BF16)

    last = pl.num_programs(0) - 1
    pl.when(pl.program_id(0) < last)(functools.partial(run, x_ref, outs[:nout]))
    pl.when(pl.program_id(0) == last)(functools.partial(run, xs_ref, outs[nout:]))


def _ffn(x, xs, g, wg, wu, wd, gn, *, final, tm, fc):
    m, d = x.shape
    assert m % tm == 0 and fc % MXU_TILE_V7X == 0
    nt = m // tm
    tile = pl.BlockSpec((tm, d), lambda i: (jnp.minimum(i, nt - 1), 0))
    whole = pl.BlockSpec(xs.shape, lambda i: (0, 0))
    dts = (F32,) if final else (F32, BF16)
    return pl.pallas_call(
        functools.partial(_ffn_body, final=final, fc=fc),
        grid=(nt + 1,),
        in_specs=[tile, _const_spec(xs.shape), _const_spec(g.shape), _const_spec(wg.shape),
                  _const_spec(wu.shape), _const_spec(wd.shape), _const_spec(gn.shape)],
        out_specs=[tile] * len(dts) + [whole] * len(dts),
        out_shape=[jax.ShapeDtypeStruct(x.shape, dt) for dt in dts]
                  + [jax.ShapeDtypeStruct(xs.shape, dt) for dt in dts],
        compiler_params=_cparams(1),
        name="ffn_final" if final else "ffn_in",
    )(x, xs, g, wg, wu, wd, gn)


def _s5_param_body(are_ref, aim_ref, ldt_ref, bre_ref, bim_ref, abr_ref, abi_ref, bbr_ref, bbi_ref):
    lam_r = are_ref[...]
    lam_i = aim_ref[...]
    dt = jnp.exp(ldt_ref[...])
    mag = jnp.exp(lam_r * dt)
    ab_r = mag * jnp.cos(lam_i * dt)
    ab_i = mag * jnp.sin(lam_i * dt)
    den = lam_r * lam_r + lam_i * lam_i
    q_r = ((ab_r - 1.0) * lam_r + ab_i * lam_i) / den
    q_i = (ab_i * lam_r - (ab_r - 1.0) * lam_i) / den
    abr_ref[...] = ab_r
    abi_ref[...] = ab_i
    bre = bre_ref[...]
    bim = bim_ref[...]
    bbr_ref[...] = q_r[:, None, :] * bre - q_i[:, None, :] * bim
    bbi_ref[...] = q_r[:, None, :] * bim + q_i[:, None, :] * bre


def _s5_params(a_re, a_im, log_dt, b_re_t, b_im_t):
    g, p = a_re.shape
    hg = b_re_t.shape[1]
    return pl.pallas_call(
        _s5_param_body,
        out_shape=[jax.ShapeDtypeStruct((g, p), F32), jax.ShapeDtypeStruct((g, p), F32),
                   jax.ShapeDtypeStruct((g, hg, p), F32), jax.ShapeDtypeStruct((g, hg, p), F32)],
        name="s5_params",
    )(a_re, a_im, log_dt.reshape(g, 1), b_re_t, b_im_t)


def _block_diag(w, nblk):
    g, a, b = w.shape
    gb = g // nblk
    w4 = w.reshape(nblk, gb, a, b)
    eye = jnp.eye(gb, dtype=w.dtype)
    return jnp.einsum('jgab,gk->jgakb', w4, eye).reshape(nblk, gb * a, gb * b)


def _s5_body(h_ref, perm_ref, wu_ref, bb_ref, cm_ref, abr_ref, abi_ref, dsk_ref, wglu_ref, x0r_ref,
             x0i_ref, yb_ref, xr_out, xi_out, bu_scr, xr_scr, xi_scr, *, nb, steps, sub):
    c = pl.program_id(0)
    rows = nb * steps
    blk = h_ref.shape[1] // sub

    @pl.when(c == 0)
    def _():
        xr_scr[...] = x0r_ref[...]
        xi_scr[...] = x0i_ref[...]

    nin = wu_ref.shape[1] // S5_BLOCKS
    ns = abr_ref.shape[1] // S5_BLOCKS
    us = []
    for i in range(sub):
        h = h_ref[:, i * blk:(i + 1) * blk, :].reshape(rows, h_ref.shape[-1])
        if steps > 1:
            h = _dot(perm_ref[0], h).astype(BF16)
        u = _dot(h, wu_ref[...])
        us.append(u)
        ub = u.astype(BF16)
        for j in range(S5_BLOCKS):
            bu_scr[i, j] = _dot(ub[:, j * nin:(j + 1) * nin], bb_ref[j])
    xr = [xr_scr[:, j * ns:(j + 1) * ns] for j in range(S5_BLOCKS)]
    xi = [xi_scr[:, j * ns:(j + 1) * ns] for j in range(S5_BLOCKS)]
    for i in range(sub):
        ys = []
        for j in range(S5_BLOCKS):
            ar = jnp.broadcast_to(abr_ref[:, j * ns:(j + 1) * ns], (nb, ns))
            ai = jnp.broadcast_to(abi_ref[:, j * ns:(j + 1) * ns], (nb, ns))
            for t in range(steps):
                tr = slice(t * nb, (t + 1) * nb)
                xr[j], xi[j] = (ar * xr[j] - ai * xi[j] + bu_scr[i, j, tr, 0:ns],
                                ar * xi[j] + ai * xr[j] + bu_scr[i, j, tr, ns:2 * ns])
                bu_scr[i, j, tr, 0:ns] = xr[j]
                bu_scr[i, j, tr, ns:2 * ns] = xi[j]
            ys.append(_dot(bu_scr[i, j].astype(BF16), cm_ref[j]))
        y = jnp.concatenate(ys, axis=1) + dsk_ref[...] * us[i]
        z = jax.nn.gelu(y)
        yb = (z * jax.nn.sigmoid(_dot(z.astype(BF16), wglu_ref[...]))).astype(BF16)
        if steps > 1:
            yb = _dot(perm_ref[1], yb).astype(BF16)
        yb_ref[:, i * blk:(i + 1) * blk, :] = yb.reshape(yb_ref.shape[0], blk, yb_ref.shape[2])
    for j in range(S5_BLOCKS):
        xr_scr[:, j * ns:(j + 1) * ns] = xr[j]
        xi_scr[:, j * ns:(j + 1) * ns] = xi[j]
    xr_out[...] = xr_scr[...]
    xi_out[...] = xi_scr[...]


def _s5(h, w_u, bb, cm, abr, abi, dsk, wglu, x0r, x0i, *, nb, steps, sub=1):
    g, seq, d = h.shape
    rows = nb * steps
    blk = sub * rows // g
    assert seq % blk == 0 and (g == nb or steps == 1)
    su = w_u.shape[1]
    nst = abr.shape[1]
    r = jnp.arange(rows)
    to_tm = (r[:, None] % nb) * steps + r[:, None] // nb == r[None, :]
    perm = jnp.stack([to_tm, to_tm.T]).astype(BF16)
    return pl.pallas_call(
        functools.partial(_s5_body, nb=nb, steps=steps, sub=sub),
        grid=(seq // blk,),
        in_specs=[
            pl.BlockSpec((g, blk, d), lambda c: (0, c, 0)),
            _const_spec(perm.shape),
            _const_spec(w_u.shape), _const_spec(bb.shape), _const_spec(cm.shape),
            _const_spec(abr.shape), _const_spec(abi.shape), _const_spec(dsk.shape),
            _const_spec(wglu.shape), _const_spec(x0r.shape), _const_spec(x0i.shape),
        ],
        out_specs=[pl.BlockSpec((g, blk, su), lambda c: (0, c, 0)),
                   pl.BlockSpec((nb, nst), lambda c: (0, 0)),
                   pl.BlockSpec((nb, nst), lambda c: (0, 0))],
        out_shape=[jax.ShapeDtypeStruct((g, seq, su), BF16),
                   jax.ShapeDtypeStruct((nb, nst), F32),
                   jax.ShapeDtypeStruct((nb, nst), F32)],
        scratch_shapes=[pltpu.VMEM((sub, S5_BLOCKS, rows, 2 * nst // S5_BLOCKS), F32),
                        pltpu.VMEM((nb, nst), F32), pltpu.VMEM((nb, nst), F32)],
        compiler_params=_cparams(1),
        name="s5_branch",
    )(h, perm, w_u, bb, cm, abr, abi, dsk, wglu, x0r, x0i)


def _headsum(x, e):
    xb = x.astype(BF16)
    t = MXU_TILE_V7X
    return jnp.concatenate([_dot(xb[:, c:c + t], e[c:c + t, c:c + t]) for c in range(0, RW, t)], axis=1)


def _split3(x):
    hi = x.astype(BF16)
    r1 = x - hi.astype(F32)
    mid = r1.astype(BF16)
    lo = (r1 - mid.astype(F32)).astype(BF16)
    return hi, mid, lo


def _rwkv_pre(fm, w):
    r = fm[:, 0:RW]
    k = fm[:, RW:2 * RW]
    v = fm[:, 2 * RW:3 * RW]
    o0 = 3 * RW
    nw = w['w_up'].shape[0]
    na = w['a_up'].shape[0]
    wd = fm[:, o0:o0 + nw]
    ad = fm[:, o0 + nw:o0 + nw + na]
    gd = fm[:, o0 + nw + na:]
    logw = -math.exp(-0.5) * _sigmoid(w['w0'][...] + _dot(jnp.tanh(wd).astype(BF16), w['w_up'][...]))
    a = _sigmoid(w['a0'][...] + _dot(ad.astype(BF16), w['a_up'][...]))
    g = _dot(_sigmoid(gd).astype(BF16), w['g_up'][...])
    kk = k * w['k_k'][...]
    kk = kk * lax.rsqrt(jnp.maximum(_headsum(kk * kk, w['e']), 1e-24))
    k2 = k * (1.0 + (a - 1.0) * w['k_a'][...])
    return r, logw, k2, v, kk, kk * a, g


def _rwkv_post(o, r, k2, v, g, w):
    inv_n = 1.0 / HEAD
    mu = _headsum(o, w['e']) * inv_n
    d = o - mu
    var = _headsum(d * d, w['e']) * inv_n
    on = d * lax.rsqrt(var + LNX_EPS) * w['lnx_w'][...] + w['lnx_b'][...]
    bonus = _headsum(r * k2 * w['r_k'][...], w['e']) * v
    return (on + bonus) * g


def _sigmoid_of_twice(xh):
    return 0.5 * jnp.tanh(xh) + 0.5


def _sigmoid(x):
    return _sigmoid_of_twice(0.5 * x)


def _merge_gates(h, y_b, w, c0=0, c1=None):
    dm = w['w_gh'].shape[1] // 2
    c1 = dm if c1 is None else c1
    gate_a = _sigmoid_of_twice(_dot(h, w['w_gh'][:, c0:c1]))
    gate_b = _sigmoid_of_twice(_dot(h, w['w_gh'][:, dm + c0:dm + c1]))
    return gate_a, gate_b * _dot(y_b.astype(BF16), w['w_b_up'][:, c0:c1])


def _merge_out(x1, y_a, gate_a, gated_b, w):
    merged = gate_a * _dot(y_a.astype(BF16), w['w_a_up'][...]) + gated_b
    return x1 + _dot(merged.astype(BF16), w['w_out'][...])


_W_NAMES = ('w_f', 'w_gh', 'mu', 'w0', 'w_up', 'a0', 'a_up', 'g_up', 'k_k', 'k_a', 'r_k',
            'lnx_w', 'lnx_b', 'e', 'w_a_up', 'w_b_up', 'w_out')


def _mixer_prompt_body(*refs, tc):
    h_ref, x1_ref, yb_ref = refs[0:3]
    nwt = len(_W_NAMES)
    w = dict(zip(_W_NAMES, refs[3:3 + nwt]))
    tri_ref, mask_ref = refs[3 + nwt:5 + nwt]
    x2_ref, shift_ref, wkv_ref = refs[5 + nwt:8 + nwt]
    f_scr, st_scr, o_scr, ga_scr, gb_scr = refs[8 + nwt:]
    c = pl.program_id(1)

    @pl.when(c == 0)
    def _():
        f_scr[...] = jnp.zeros_like(f_scr)
        st_scr[...] = jnp.zeros_like(st_scr)

    h = h_ref[...]
    f = _dot(h, w['w_f'][...])
    row0 = lax.broadcasted_iota(jnp.int32, f.shape, 0) == 0
    shifted = jnp.where(row0, f_scr[...], pltpu.roll(f, 1, 0))
    f_scr[...] = f[tc - 1:tc, :]
    fm = f + w['mu'][...] * (shifted - f)
    r, logw, k2, v, kk, b, g = _rwkv_pre(fm, w)

    ns = tc // CHUNK
    dm = x1_ref.shape[1]
    yb = yb_ref[...]
    tri = tri_ref[...]
    mask = mask_ref[...] > 0.5
    pw = 2 * HEAD
    lane = lambda rows: lax.broadcasted_iota(jnp.int32, (rows, pw), 1)
    low = lambda a: lane(a.shape[0]) < HEAD
    eye2 = (lax.broadcasted_iota(jnp.int32, (CHUNK, pw), 0) == lane(CHUNK) % HEAD).astype(F32)
    n_double = CHUNK.bit_length() - 2
    pairs = range(PAIRS)
    first = lambda a: jnp.where(low(a), a, jnp.zeros_like(a))
    second = lambda a: jnp.where(low(a), jnp.zeros_like(a), a)
    bd = lambda y: jnp.concatenate([first(y), second(y)], axis=0)
    ad = lambda y: jnp.concatenate([second(y), first(y)], axis=0)
    pick = lambda x, y: jnp.where(low(x), x, y)
    stack = lambda top, bot: jnp.concatenate([top.astype(BF16), bot.astype(BF16)], axis=0)
    bf = lambda a: a.astype(BF16)
    zc = jnp.zeros((CHUNK, pw), BF16)

    def gate_slice(s, phase):
        per_phase = dm // MXU_TILE_V7X // 2
        every = ns // per_phase
        if s % every == 0:
            c0 = (phase * per_phase + s // every) * MXU_TILE_V7X
            ga_scr[:, c0:c0 + MXU_TILE_V7X], gb_scr[:, c0:c0 + MXU_TILE_V7X] = _merge_gates(
                h, yb, w, c0, c0 + MXU_TILE_V7X)

    def chunk_setup(s):
        sl = slice(s * CHUNK, (s + 1) * CHUNK)
        lw = logw[sl]
        hi, mid, lo = _split3(lw)
        lpre = _dot(tri, hi) + _dot(tri, mid) + _dot(tri, lo)
        lsuf = lpre[CHUNK - 1:CHUNK, :] - lpre
        e_pre = jnp.exp(lpre)
        e_neg = jnp.exp(-lpre)
        e_suf = jnp.exp(lsuf)
        lhs = jnp.concatenate([kk[sl] * jnp.exp(lpre - lw), r[sl] * e_pre], axis=0).astype(BF16)
        rhs = jnp.concatenate([b[sl] * e_neg, k2[sl] * e_neg], axis=0).astype(BF16)
        sfx_t = jnp.concatenate([b[sl] * e_suf, k2[sl] * e_suf], axis=0).T.astype(BF16)
        gate_slice(s, 0)
        return lhs, rhs, sfx_t, v[sl].astype(BF16), e_pre[CHUNK - 1:CHUNK, :]

    def chunk_units(setups):
        units = [(s, pr) for s in setups for pr in pairs]
        cut = lambda a, pr: a[:, pr * pw:(pr + 1) * pw]
        lhs = {u: cut(setups[u[0]][0], u[1]) for u in units}
        rhs = {u: cut(setups[u[0]][1], u[1]) for u in units}
        sc = {u: _dot_nt(lhs[u], jnp.concatenate(
            [first(rhs[u]), second(rhs[u][CHUNK:]), second(rhs[u][:CHUNK])], axis=0)) for u in units}
        a0 = {u: jnp.where(mask, sc[u][:, :pw], 0.0) for u in units}
        a1 = {u: jnp.where(mask, sc[u][:, pw:], 0.0) for u in units}
        m_b = {u: pick(a0[u][:CHUNK], a1[u][:CHUNK]) for u in units}
        m_ks = {u: pick(a1[u][:CHUNK], a0[u][:CHUNK]) for u in units}
        n_b = {u: pick(a0[u][CHUNK:], a1[u][CHUNK:]) for u in units}
        n_ks = {u: pick(a1[u][CHUNK:], a0[u][CHUNK:]) for u in units}
        p = {u: -m_b[u] for u in units}
        winv = {u: eye2 + p[u] for u in units}
        p = {u: _dot(bf(p[u]), bd(bf(p[u]))) for u in units}
        for _ in range(n_double - 1):
            wp = {u: _dot(stack(winv[u], p[u]), bd(bf(p[u]))) for u in units}
            winv = {u: winv[u] + wp[u][:CHUNK] for u in units}
            p = {u: wp[u][CHUNK:] for u in units}
        wb = {u: bf(winv[u] + _dot(bf(winv[u]), bd(bf(p[u])))) for u in units}
        adv = {u: ad(cut(setups[u[0]][3], u[1])) for u in units}
        mkv = {u: bf(_dot(bf(m_ks[u]), adv[u])) for u in units}
        gu = {u: _dot(wb[u], jnp.concatenate([bd(lhs[u][:CHUNK]), bd(mkv[u])], axis=1)) for u in units}
        bdg = {u: bd(bf(gu[u][:, :pw])) for u in units}
        bdu = {u: bd(bf(-gu[u][:, pw:])) for u in units}
        q = {u: lhs[u][CHUNK:].astype(F32) - _dot(bf(n_b[u]), bdg[u]) for u in units}
        o0 = {u: _dot(jnp.concatenate([bf(n_b[u]), bf(n_ks[u])], axis=1),
                      jnp.concatenate([bdu[u], adv[u]], axis=0)) for u in units}
        sfx_t = {u: jnp.concatenate([setups[u[0]][2][(2 * u[1] + i) * HEAD:(2 * u[1] + i + 1) * HEAD, :]
                                     for i in range(2)], axis=1)
                 for u in units}
        gv = {u: jnp.concatenate(
            [jnp.concatenate([bdg[u][:CHUNK], bdu[u][:CHUNK]], axis=1),
             jnp.concatenate([zc, adv[u][CHUNK:]], axis=1),
             jnp.concatenate([bdg[u][CHUNK:], bdu[u][CHUNK:]], axis=1),
             jnp.concatenate([zc, adv[u][:CHUNK]], axis=1)], axis=0) for u in units}
        sg = {u: _dot(sfx_t[u], gv[u]) for u in units}
        qphi = {u: stack(q[u], eye2 * cut(setups[u[0]][4], u[1]) - sg[u][:, :pw]) for u in units}
        opsi = {u: jnp.concatenate([o0[u], sg[u][:, pw:]], axis=0) for u in units}
        return qphi, opsi

    qphi, opsi = chunk_units({s: chunk_setup(s) for s in range(ns)})
    st =[st_scr[pr] for pr in pairs]
    for s in range(ns):
        os = [_dot(qphi[s, pr], bd(bf(st[pr]))) + opsi[s, pr] for pr in pairs]
        for pr in pairs:
            o_scr[s * CHUNK:(s + 1) * CHUNK, pr * pw:(pr + 1) * pw] = os[pr][:CHUNK]
        st = [os[pr][CHUNK:] for pr in pairs]
        gate_slice(s, 1)
    for pr in pairs:
        st_scr[pr] = st[pr]

    y_a = _rwkv_post(o_scr[...], r, k2, v, g, w)
    x2_ref[...] = _merge_out(x1_ref[...], y_a, ga_scr[...], gb_scr[...], w)

    @pl.when(c == pl.num_programs(1) - 1)
    def _():
        shift_ref[...] = f[tc - 1:tc, :]
        for hd in range(HEADS):
            wkv_ref[hd] = st_scr[hd // 2][:, (hd % 2) * HEAD:(hd % 2 + 1) * HEAD].T


def _mixer_prompt(h, x1, yb, wts, tri, mask, *, tc):
    bsz, seq, d = h.shape
    assert seq % tc == 0 and tc % CHUNK == 0
    nf = wts['w_f'].shape[1]
    wlist = [wts[n] for n in _W_NAMES]
    tok = lambda width: pl.BlockSpec((None, tc, width), lambda b, c: (b, c, 0))
    return pl.pallas_call(
        functools.partial(_mixer_prompt_body, tc=tc),
        grid=(bsz, seq // tc),
        in_specs=[tok(d), tok(d), tok(RW)] + [_const_spec(a.shape) for a in wlist]
                 + [_const_spec(tri.shape), _const_spec(mask.shape)],
        out_specs=[tok(d),
                   pl.BlockSpec((None, 1, nf), lambda b, c: (b, 0, 0)),
                   pl.BlockSpec((None, HEADS, HEAD, HEAD), lambda b, c: (b, 0, 0, 0))],
        out_shape=[jax.ShapeDtypeStruct((bsz, seq, d), F32),
                   jax.ShapeDtypeStruct((bsz, 1, nf), F32),
                   jax.ShapeDtypeStruct((bsz, HEADS, HEAD, HEAD), F32)],
        scratch_shapes=[pltpu.VMEM((1, nf), F32),
                        pltpu.VMEM((PAIRS, HEAD, 2 * HEAD), F32),
                        pltpu.VMEM((tc, RW), F32),
                        pltpu.VMEM((tc, d), F32), pltpu.VMEM((tc, d), F32)],
        compiler_params=_cparams(2),
        name="mixer_prompt",
    )(h, x1, yb, *wlist, tri, mask)


def _mixer_sample_body(*refs):
    h_ref, x1_ref, yb_ref, shift_in_ref, s_ref = refs[0:5]
    nwt = len(_W_NAMES)
    w = dict(zip(_W_NAMES, refs[5:5 + nwt]))
    x2_ref, shift_ref, s_out_ref = refs[5 + nwt:8 + nwt]
    t_scr, row_scr, ot_scr = refs[8 + nwt:]
    hd = pl.program_id(0)

    @pl.when(hd == 0)
    def _():
        f = _dot(h_ref[...], w['w_f'][...])
        fm = f + w['mu'][...] * (shift_in_ref[...] - f)
        shift_ref[...] = f
        r, logw, k2, v, kk, b, g = _rwkv_pre(fm, w)
        for i, a in enumerate((kk, jnp.exp(logw), b, k2, r, v)):
            t_scr[i] = a.T
        for i, a in enumerate((r, k2, v, g)):
            row_scr[i] = a

    off = pl.multiple_of(hd * HEAD, HEAD)
    hs = pl.ds(off, HEAD)
    kk_t, dec_t, b_t, k_t, r_t = (t_scr[i, hs, :] for i in range(5))

    def v_group(vg, carry):
        v0 = pl.multiple_of(vg * 8, 8)
        s8 = s_ref[pl.ds(v0, 8)]
        v8 = t_scr[5, pl.ds(off + v0, 8), :]
        o_rows = []
        for j in range(8):
            sa = -jnp.sum(s8[j] * kk_t, axis=0, keepdims=True)
            sn = s8[j] * dec_t + sa * b_t + v8[j:j + 1, :] * k_t
            s_out_ref[v0 + j] = sn
            o_rows.append(jnp.sum(sn * r_t, axis=0, keepdims=True))
        ot_scr[pl.ds(off + v0, 8), :] = jnp.concatenate(o_rows, axis=0)
        return carry

    lax.fori_loop(0, HEAD // 8, v_group, 0)

    @pl.when(hd == pl.num_programs(0) - 1)
    def _():
        r, k2, v, g = (row_scr[i] for i in range(4))
        y_a = _rwkv_post(ot_scr[...].T, r, k2, v, g, w)
        gate_a, gated_b = _merge_gates(h_ref[...], yb_ref[...], w)
        x2_ref[...] = _merge_out(x1_ref[...], y_a, gate_a, gated_b, w)


def _mixer_sample(h, x1, yb, shift_in, wkv_t, wts):
    n, d = h.shape
    nf = wts['w_f'].shape[1]
    wlist = [wts[nm] for nm in _W_NAMES]
    st_spec = pl.BlockSpec((None, HEAD, HEAD, n), lambda hd: (hd, 0, 0, 0))
    return pl.pallas_call(
        _mixer_sample_body,
        grid=(HEADS,),
        in_specs=[_const_spec(a.shape) for a in (h, x1, yb, shift_in)] + [st_spec]
                 + [_const_spec(a.shape) for a in wlist],
        out_specs=[pl.BlockSpec((n, d), lambda hd: (0, 0)),
                   pl.BlockSpec((n, nf), lambda hd: (0, 0)), st_spec],
        out_shape=[jax.ShapeDtypeStruct((n, d), F32),
                   jax.ShapeDtypeStruct((n, nf), F32),
                   jax.ShapeDtypeStruct(wkv_t.shape, F32)],
        scratch_shapes=[pltpu.VMEM((6, RW, n), F32), pltpu.VMEM((4, n, RW), F32),
                        pltpu.VMEM((RW, n), F32)],
        compiler_params=_cparams(1),
        name="mixer_sample",
    )(h, x1, yb, shift_in, wkv_t, *wlist)


def _chunk_constants():
    t = jnp.arange(CHUNK)
    incl = (t[:, None] >= t[None, :])
    strict = (t[:, None] > t[None, :])
    tri = incl.astype(BF16)
    top = jnp.concatenate([strict, strict], axis=1)
    bot = jnp.concatenate([incl, incl], axis=1)
    mask = jnp.concatenate([top, bot], axis=0).astype(F32)
    return tri, mask


def kernel(x_prompt, x_sample, state_shift, state_wkv, state_s5_re, state_s5_im, g_ffn1, ffn1_gate, ffn1_up, ffn1_down, g_mix, w_in, mu_shift, w0, w_up, a0, a_up, g_up, k_k, k_a, r_k, lnx_w, lnx_b, A_re, A_im, log_dt, B_re, B_im, C_re, C_im, D_skip, w_glu, w_a_up, w_b_up, w_out, g_ffn2, ffn2_gate, ffn2_up, ffn2_down, g_final):
    depth = g_ffn1.shape[0]
    assert depth == 1
    bp, seq, d = x_prompt.shape
    bs = x_sample.shape[0]
    assert x_sample.shape[1] == 1
    ncols_f = mu_shift.shape[1]
    su = D_skip.shape[1]
    groups, pstate = A_re.shape[1:]
    nstate = groups * pstate
    bf = lambda a: a.astype(BF16)
    row = lambda a: a.reshape(1, -1).astype(F32)

    w_in0 = w_in[0]
    eye_h = jnp.eye(HEADS, dtype=F32)
    wts = {
        'w_f': bf(w_in0[:, :ncols_f]),
        'w_gh': bf(0.5 * w_in0[:, ncols_f + su:]),
        'mu': row(mu_shift[0]), 'w0': row(w0[0]), 'w_up': bf(w_up[0]), 'a0': row(a0[0]),
        'a_up': bf(a_up[0]), 'g_up': bf(g_up[0]), 'k_k': row(k_k[0]), 'k_a': row(k_a[0]),
        'r_k': row(r_k[0]), 'lnx_w': row(lnx_w[0]), 'lnx_b': row(lnx_b[0]),
        'e': bf(jnp.kron(eye_h, jnp.ones((HEAD, HEAD), F32))),
        'w_a_up': bf(w_a_up[0]), 'w_b_up': bf(w_b_up[0]), 'w_out': bf(w_out[0]),
    }
    w_u = bf(w_in0[:, ncols_f:ncols_f + su])
    abr, abi, bbr, bbi = _s5_params(A_re[0], A_im[0], log_dt[0],
                                    jnp.swapaxes(B_re[0], 1, 2), jnp.swapaxes(B_im[0], 1, 2))
    bb = bf(jnp.concatenate([_block_diag(bbr, S5_BLOCKS), _block_diag(bbi, S5_BLOCKS)], axis=2))
    cm = bf(jnp.concatenate([_block_diag(jnp.swapaxes(C_re[0], 1, 2), S5_BLOCKS),
                             _block_diag(-jnp.swapaxes(C_im[0], 1, 2), S5_BLOCKS)], axis=1))
    abr = abr.reshape(1, nstate)
    abi = abi.reshape(1, nstate)
    dsk = row(D_skip[0])
    wglu = bf(w_glu[0])
    ffn1 = (row(g_ffn1[0]), ffn1_gate[0], ffn1_up[0], ffn1_down[0], row(g_mix[0]))
    ffn2 = (row(g_ffn2[0]), ffn2_gate[0], ffn2_up[0], ffn2_down[0], row(g_final))
    fc = 2 * MXU_TILE_V7X
    tri, mask = _chunk_constants()

    x1p, hnp, x1s, hns = _ffn(x_prompt.reshape(bp * seq, d), x_sample.reshape(bs, d), *ffn1,
                              final=False, tm=512, fc=fc)

    hnp = hnp.reshape(bp, seq, d)
    zeros_state = jnp.zeros((bp, nstate), F32)
    yb_p, s5r_p, s5i_p = _s5(hnp, w_u, bb, cm, abr, abi, dsk, wglu, zeros_state, zeros_state,
                             nb=bp, steps=32, sub=4)
    x2p, shift_p, wkv_p = _mixer_prompt(hnp, x1p.reshape(bp, seq, d), yb_p, wts, tri, mask, tc=512)

    yb_s, s5r_s, s5i_s = _s5(hns[None], w_u, bb, cm, abr, abi, dsk, wglu,
                             state_s5_re[0].reshape(bs, nstate), state_s5_im[0].reshape(bs, nstate),
                             nb=bs, steps=1)
    wkv_t = jnp.transpose(state_wkv[0], (1, 2, 3, 0))
    x2s, shift_s, wkv_ts = _mixer_sample(hns, x1s, yb_s[0], state_shift[0], wkv_t, wts)
    wkv_s = jnp.transpose(wkv_ts, (3, 0, 1, 2))

    y_prompt, y_sample = _ffn(x2p.reshape(bp * seq, d), x2s, *ffn2, final=True, tm=512, fc=fc)
    y_prompt = y_prompt.reshape(bp, seq, d)
    y_sample = y_sample.reshape(bs, 1, d)

    st5 = lambda a, n: a.reshape(1, n, groups, pstate)
    return (y_prompt, y_sample,
            shift_p.reshape(1, bp, ncols_f), wkv_p[None], st5(s5r_p, bp), st5(s5i_p, bp),
            shift_s[None], wkv_s[None], st5(s5r_s, bs), st5(s5i_s, bs))
```

```python
import functools
import math

import jax
import jax.numpy as jnp
from jax import lax
from jax.experimental import pallas as pl
from jax.experimental.pallas import tpu as pltpu

F32 = jnp.float32
BF16 = jnp.bfloat16

NORM_EPS = 1e-6
LNX_EPS = 64e-5
HEAD = 64
HEADS = 8
PAIRS = HEADS // 2
RW = HEAD * HEADS
S5_BLOCKS = 4
CHUNK = 64
VMEM_LIMIT = 56 * 1024 * 1024
MXU_TILE_V7X = 256


def _dot(a, b):
    return jnp.dot(a, b, preferred_element_type=F32)


def _dot_nt(a, b):
    return lax.dot_general(a, b, (((1,), (1,)), ((), ())), preferred_element_type=F32)


def _rms(x, g):
    ms = jnp.mean(x * x, axis=-1, keepdims=True)
    return x * lax.rsqrt(ms + NORM_EPS) * g


def _const_spec(shape):
    nd = len(shape)
    return pl.BlockSpec(shape, lambda *_: (0,) * nd, pipeline_mode=pl.Buffered(1))


def _cparams(ngrid):
    return pltpu.CompilerParams(dimension_semantics=("arbitrary",) * ngrid,
                                vmem_limit_bytes=VMEM_LIMIT)


def _ffn_body(x_ref, xs_ref, g_ref, wg_ref, wu_ref, wd_ref, gn_ref, *outs, final, fc):
    nout = len(outs) // 2

    def run(src_ref, dst):
        x = src_ref[...]
        hn = _rms(x, g_ref[...]).astype(BF16)
        acc = None
        f = wg_ref.shape[1]
        for c0 in range(0, f, fc):
            c1 = min(c0 + fc, f)
            gate = _dot(hn, wg_ref[:, c0:c1].astype(BF16))
            up = _dot(hn, wu_ref[:, c0:c1].astype(BF16))
            act = (jax.nn.silu(gate) * up).astype(BF16)
            part = _dot(act, wd_ref[c0:c1, :].astype(BF16))
            acc = part if acc is None else acc + part
        x1 = x + 0.5 * acc
        if final:
            dst[0][...] = _rms(x1, gn_ref[...])
        else:
            dst[0][...] = x1
            dst[1][...] = _rms(x1, gn_ref[...]).astype(BF16)

    last = pl.num_programs(0) - 1
    pl.when(pl.program_id(0) < last)(functools.partial(run, x_ref, outs[:nout]))
    pl.when(pl.program_id(0) == last)(functools.partial(run, xs_ref, outs[nout:]))


def _ffn(x, xs, g, wg, wu, wd, gn, *, final, tm, fc):
    m, d = x.shape
    assert m % tm == 0 and fc % MXU_TILE_V7X == 0
    nt = m // tm
    tile = pl.BlockSpec((tm, d), lambda i: (jnp.minimum(i, nt - 1), 0))
    whole = pl.BlockSpec(xs.shape, lambda i: (0, 0))
    dts = (F32,) if final else (F32, BF16)
    return pl.pallas_call(
        functools.partial(_ffn_body, final=final, fc=fc),
        grid=(nt + 1,),
        in_specs=[tile, _const_spec(xs.shape), _const_spec(g.shape), _const_spec(wg.shape),
                  _const_spec(wu.shape), _const_spec(wd.shape), _const_spec(gn.shape)],
        out_specs=[tile] * len(dts) + [whole] * len(dts),
        out_shape=[jax.ShapeDtypeStruct(x.shape, dt) for dt in dts]
                  + [jax.ShapeDtypeStruct(xs.shape, dt) for dt in dts],
        compiler_params=_cparams(1),
        name="ffn_final" if final else "ffn_in",
    )(x, xs, g, wg, wu, wd, gn)


def _s5_param_body(are_ref, aim_ref, ldt_ref, bre_ref, bim_ref, abr_ref, abi_ref, bbr_ref, bbi_ref):
    lam_r = are_ref[...]
    lam_i = aim_ref[...]
    dt = jnp.exp(ldt_ref[...])
    mag = jnp.exp(lam_r * dt)
    ab_r = mag * jnp.cos(lam_i * dt)
    ab_i = mag * jnp.sin(lam_i * dt)
    den = lam_r * lam_r + lam_i * lam_i
    q_r = ((ab_r - 1.0) * lam_r + ab_i * lam_i) / den
    q_i = (ab_i * lam_r - (ab_r - 1.0) * lam_i) / den
    abr_ref[...] = ab_r
    abi_ref[...] = ab_i
    bre = bre_ref[...]
    bim = bim_ref[...]
    bbr_ref[...] = q_r[:, None, :] * bre - q_i[:, None, :] * bim
    bbi_ref[...] = q_r[:, None, :] * bim + q_i[:, None, :] * bre


def _s5_params(a_re, a_im, log_dt, b_re_t, b_im_t):
    g, p = a_re.shape
    hg = b_re_t.shape[1]
    return pl.pallas_call(
        _s5_param_body,
        out_shape=[jax.ShapeDtypeStruct((g, p), F32), jax.ShapeDtypeStruct((g, p), F32),
                   jax.ShapeDtypeStruct((g, hg, p), F32), jax.ShapeDtypeStruct((g, hg, p), F32)],
        name="s5_params",
    )(a_re, a_im, log_dt.reshape(g, 1), b_re_t, b_im_t)


def _block_diag(w, nblk):
    g, a, b = w.shape
    gb = g // nblk
    w4 = w.reshape(nblk, gb, a, b)
    eye = jnp.eye(gb, dtype=w.dtype)
    return jnp.einsum('jgab,gk->jgakb', w4, eye).reshape(nblk, gb * a, gb * b)


def _s5_body(h_ref, perm_ref, wu_ref, bb_ref, cm_ref, abr_ref, abi_ref, dsk_ref, wglu_ref, x0r_ref,
             x0i_ref, yb_ref, xr_out, xi_out, bu_scr, xr_scr, xi_scr, *, nb, steps, sub):
    c = pl.program_id(0)
    rows = nb * steps
    blk = h_ref.shape[1] // sub

    @pl.when(c == 0)
    def _():
        xr_scr[...] = x0r_ref[...]
        xi_scr[...] = x0i_ref[...]

    nin = wu_ref.shape[1] // S5_BLOCKS
    ns = abr_ref.shape[1] // S5_BLOCKS
    us = []
    for i in range(sub):
        h = h_ref[:, i * blk:(i + 1) * blk, :].reshape(rows, h_ref.shape[-1])
        if steps > 1:
            h = _dot(perm_ref[0], h).astype(BF16)
        u = _dot(h, wu_ref[...])
        us.append(u)
        ub = u.astype(BF16)
        for j in range(S5_BLOCKS):
            bu_scr[i, j] = _dot(ub[:, j * nin:(j + 1) * nin], bb_ref[j])
    xr = [xr_scr[:, j * ns:(j + 1) * ns] for j in range(S5_BLOCKS)]
    xi = [xi_scr[:, j * ns:(j + 1) * ns] for j in range(S5_BLOCKS)]
    for i in range(sub):
        ys = []
        for j in range(S5_BLOCKS):
            ar = jnp.broadcast_to(abr_ref[:, j * ns:(j + 1) * ns], (nb, ns))
            ai = jnp.broadcast_to(abi_ref[:, j * ns:(j + 1) * ns], (nb, ns))
            for t in range(steps):
                tr = slice(t * nb, (t + 1) * nb)
                xr[j], xi[j] = (ar * xr[j] - ai * xi[j] + bu_scr[i, j, tr, 0:ns],
                                ar * xi[j] + ai * xr[j] + bu_scr[i, j, tr, ns:2 * ns])
                bu_scr[i, j, tr, 0:ns] = xr[j]
                bu_scr[i, j, tr, ns:2 * ns] = xi[j]
            ys.append(_dot(bu_scr[i, j].astype(BF16), cm_ref[j]))
        y = jnp.concatenate(ys, axis=1) + dsk_ref[...] * us[i]
        z = jax.nn.gelu(y)
        yb = (z * jax.nn.sigmoid(_dot(z.astype(BF16), wglu_ref[...]))).astype(BF16)
        if steps > 1:
            yb = _dot(perm_ref[1], yb).astype(BF16)
        yb_ref[:, i * blk:(i + 1) * blk, :] = yb.reshape(yb_ref.shape[0], blk, yb_ref.shape[2])
    for j in range(S5_BLOCKS):
        xr_scr[:, j * ns:(j + 1) * ns] = xr[j]
        xi_scr[:, j * ns:(j + 1) * ns] = xi[j]
    xr_out[...] = xr_scr[...]
    xi_out[...] = xi_scr[...]


def _s5(h, w_u, bb, cm, abr, abi, dsk, wglu, x0r, x0i, *, nb, steps, sub=1):
    g, seq, d = h.shape
    rows = nb * steps
    blk = sub * rows // g
    assert seq % blk == 0 and (g == nb or steps == 1)
    su = w_u.shape[1]
    nst = abr.shape[1]
    r = jnp.arange(rows)
    to_tm = (r[:, None] % nb) * steps + r[:, None] // nb == r[None, :]
    perm = jnp.stack([to_tm, to_tm.T]).astype(BF16)
    return pl.pallas_call(
        functools.partial(_s5_body, nb=nb, steps=steps, sub=sub),
        grid=(seq // blk,),
        in_specs=[
            pl.BlockSpec((g, blk, d), lambda c: (0, c, 0)),
            _const_spec(perm.shape),
            _const_spec(w_u.shape), _const_spec(bb.shape), _const_spec(cm.shape),
            _const_spec(abr.shape), _const_spec(abi.shape), _const_spec(dsk.shape),
            _const_spec(wglu.shape), _const_spec(x0r.shape), _const_spec(x0i.shape),
        ],
        out_specs=[pl.BlockSpec((g, blk, su), lambda c: (0, c, 0)),
                   pl.BlockSpec((nb, nst), lambda c: (0, 0)),
                   pl.BlockSpec((nb, nst), lambda c: (0, 0))],
        out_shape=[jax.ShapeDtypeStruct((g, seq, su), BF16),
                   jax.ShapeDtypeStruct((nb, nst), F32),
                   jax.ShapeDtypeStruct((nb, nst), F32)],
        scratch_shapes=[pltpu.VMEM((sub, S5_BLOCKS, rows, 2 * nst // S5_BLOCKS), F32),
                        pltpu.VMEM((nb, nst), F32), pltpu.VMEM((nb, nst), F32)],
        compiler_params=_cparams(1),
        name="s5_branch",
    )(h, perm, w_u, bb, cm, abr, abi, dsk, wglu, x0r, x0i)


def _headsum(x, e):
    xb = x.astype(BF16)
    t = MXU_TILE_V7X
    return jnp.concatenate([_dot(xb[:, c:c + t], e[c:c + t, c:c + t]) for c in range(0, RW, t)], axis=1)


def _split3(x):
    hi = x.astype(BF16)
    r1 = x - hi.astype(F32)
    mid = r1.astype(BF16)
    lo = (r1 - mid.astype(F32)).astype(BF16)
    return hi, mid, lo


def _rwkv_pre(fm, w):
    r = fm[:, 0:RW]
    k = fm[:, RW:2 * RW]
    v = fm[:, 2 * RW:3 * RW]
    o0 = 3 * RW
    nw = w['w_up'].shape[0]
    na = w['a_up'].shape[0]
    wd = fm[:, o0:o0 + nw]
    ad = fm[:, o0 + nw:o0 + nw + na]
    gd = fm[:, o0 + nw + na:]
    logw = -math.exp(-0.5) * _sigmoid(w['w0'][...] + _dot(jnp.tanh(wd).astype(BF16), w['w_up'][...]))
    a = _sigmoid(w['a0'][...] + _dot(ad.astype(BF16), w['a_up'][...]))
    g = _dot(_sigmoid(gd).astype(BF16), w['g_up'][...])
    kk = k * w['k_k'][...]
    kk = kk * lax.rsqrt(jnp.maximum(_headsum(kk * kk, w['e']), 1e-24))
    k2 = k * (1.0 + (a - 1.0) * w['k_a'][...])
    return r, logw, k2, v, kk, kk * a, g


def _rwkv_bonus(r, k2, v, w):
    return _headsum(r * k2 * w['r_k'][...], w['e']) * v


def _rwkv_post(o, bonus, g, w):
    inv_n = 1.0 / HEAD
    mu = _headsum(o, w['e']) * inv_n
    d = o - mu
    var = _headsum(d * d, w['e']) * inv_n
    on = d * lax.rsqrt(var + LNX_EPS) * w['lnx_w'][...] + w['lnx_b'][...]
    return (on + bonus) * g


def _sigmoid_of_twice(xh):
    return 0.5 * jnp.tanh(xh) + 0.5


def _sigmoid(x):
    return _sigmoid_of_twice(0.5 * x)


def _merge_gates(h, y_b, w, c0=0, c1=None):
    dm = w['w_gh'].shape[1] // 2
    c1 = dm if c1 is None else c1
    gate_a = _sigmoid_of_twice(_dot(h, w['w_gh'][:, c0:c1]))
    gate_b = _sigmoid_of_twice(_dot(h, w['w_gh'][:, dm + c0:dm + c1]))
    return gate_a, gate_b * _dot(y_b.astype(BF16), w['w_b_up'][:, c0:c1])


def _merge_out(x1, y_a, gate_a, gated_b, w):
    merged = gate_a * _dot(y_a.astype(BF16), w['w_a_up'][...]) + gated_b
    return x1 + _dot(merged.astype(BF16), w['w_out'][...])


_W_NAMES = ('w_f', 'w_gh', 'mu', 'w0', 'w_up', 'a0', 'a_up', 'g_up', 'k_k', 'k_a', 'r_k',
            'lnx_w', 'lnx_b', 'e', 'w_a_up', 'w_b_up', 'w_out')


def _mixer_prompt_body(*refs, tc):
    h_ref, x1_ref, yb_ref = refs[0:3]
    nwt = len(_W_NAMES)
    w = dict(zip(_W_NAMES, refs[3:3 + nwt]))
    tri_ref, mask_ref = refs[3 + nwt:5 + nwt]
    x2_ref, shift_ref, wkv_ref = refs[5 + nwt:8 + nwt]
    f_scr, st_scr, o_scr, ga_scr, gb_scr = refs[8 + nwt:]
    c = pl.program_id(1)

    @pl.when(c == 0)
    def _():
        f_scr[...] = jnp.zeros_like(f_scr)
        st_scr[...] = jnp.zeros_like(st_scr)

    h = h_ref[...]
    f = _dot(h, w['w_f'][...])
    row0 = lax.broadcasted_iota(jnp.int32, f.shape, 0) == 0
    shifted = jnp.where(row0, f_scr[...], pltpu.roll(f, 1, 0))
    f_scr[...] = f[tc - 1:tc, :]
    fm = f + w['mu'][...] * (shifted - f)
    r, logw, k2, v, kk, b, g = _rwkv_pre(fm, w)

    ns = tc // CHUNK
    dm = x1_ref.shape[1]
    yb = yb_ref[...]
    tri = tri_ref[...]
    mask = mask_ref[...] > 0.5
    pw = 2 * HEAD
    lane = lambda rows: lax.broadcasted_iota(jnp.int32, (rows, pw), 1)
    low = lambda a: lane(a.shape[0]) < HEAD
    eye2 = (lax.broadcasted_iota(jnp.int32, (CHUNK, pw), 0) == lane(CHUNK) % HEAD).astype(F32)
    n_double = CHUNK.bit_length() - 2
    pairs = range(PAIRS)
    first = lambda a: jnp.where(low(a), a, jnp.zeros_like(a))
    second = lambda a: jnp.where(low(a), jnp.zeros_like(a), a)
    bd = lambda y: jnp.concatenate([first(y), second(y)], axis=0)
    ad = lambda y: jnp.concatenate([second(y), first(y)], axis=0)
    pick = lambda x, y: jnp.where(low(x), x, y)
    stack = lambda top, bot: jnp.concatenate([top.astype(BF16), bot.astype(BF16)], axis=0)
    bf = lambda a: a.astype(BF16)
    zc = jnp.zeros((CHUNK, pw), BF16)

    def gate_slice(s, phase):
        per_phase = dm // MXU_TILE_V7X // 2
        every = ns // per_phase
        if s % every == 0:
            c0 = (phase * per_phase + s // every) * MXU_TILE_V7X
            ga_scr[:, c0:c0 + MXU_TILE_V7X], gb_scr[:, c0:c0 + MXU_TILE_V7X] = _merge_gates(
                h, yb, w, c0, c0 + MXU_TILE_V7X)

    def chunk_setup(s):
        sl = slice(s * CHUNK, (s + 1) * CHUNK)
        lw = logw[sl]
        hi, mid, lo = _split3(lw)
        lpre = _dot(tri, hi) + _dot(tri, mid) + _dot(tri, lo)
        lsuf = lpre[CHUNK - 1:CHUNK, :] - lpre
        e_pre = jnp.exp(lpre)
        e_neg = jnp.exp(-lpre)
        e_suf = jnp.exp(lsuf)
        lhs = jnp.concatenate([kk[sl] * jnp.exp(lpre - lw), r[sl] * e_pre], axis=0).astype(BF16)
        rhs = jnp.concatenate([b[sl] * e_neg, k2[sl] * e_neg], axis=0).astype(BF16)
        sfx_t = jnp.concatenate([b[sl] * e_suf, k2[sl] * e_suf], axis=0).T.astype(BF16)
        gate_slice(s, 0)
        return lhs, rhs, sfx_t, v[sl].astype(BF16), e_pre[CHUNK - 1:CHUNK, :]

    def chunk_units(setups):
        units = [(s, pr) for s in setups for pr in pairs]
        cut = lambda a, pr: a[:, pr * pw:(pr + 1) * pw]
        lhs = {u: cut(setups[u[0]][0], u[1]) for u in units}
        rhs = {u: cut(setups[u[0]][1], u[1]) for u in units}
        sc = {u: _dot_nt(lhs[u], jnp.concatenate(
            [first(rhs[u]), second(rhs[u][CHUNK:]), second(rhs[u][:CHUNK])], axis=0)) for u in units}
        a0 = {u: jnp.where(mask, sc[u][:, :pw], 0.0) for u in units}
        a1 = {u: jnp.where(mask, sc[u][:, pw:], 0.0) for u in units}
        m_b = {u: pick(a0[u][:CHUNK], a1[u][:CHUNK]) for u in units}
        m_ks = {u: pick(a1[u][:CHUNK], a0[u][:CHUNK]) for u in units}
        n_b = {u: pick(a0[u][CHUNK:], a1[u][CHUNK:]) for u in units}
        n_ks = {u: pick(a1[u][CHUNK:], a0[u][CHUNK:]) for u in units}
        p = {u: -m_b[u] for u in units}
        winv = {u: eye2 + p[u] for u in units}
        p = {u: _dot(bf(p[u]), bd(bf(p[u]))) for u in units}
        for _ in range(n_double - 1):
            wp = {u: _dot(stack(winv[u], p[u]), bd(bf(p[u]))) for u in units}
            winv = {u: winv[u] + wp[u][:CHUNK] for u in units}
            p = {u: wp[u][CHUNK:] for u in units}
        wb = {u: bf(winv[u] + _dot(bf(winv[u]), bd(bf(p[u])))) for u in units}
        adv = {u: ad(cut(setups[u[0]][3], u[1])) for u in units}
        mkv = {u: bf(_dot(bf(m_ks[u]), adv[u])) for u in units}
        gu = {u: _dot(wb[u], jnp.concatenate([bd(lhs[u][:CHUNK]), bd(mkv[u])], axis=1)) for u in units}
        bdg = {u: bd(bf(gu[u][:, :pw])) for u in units}
        bdu = {u: bd(bf(-gu[u][:, pw:])) for u in units}
        q = {u: lhs[u][CHUNK:].astype(F32) - _dot(bf(n_b[u]), bdg[u]) for u in units}
        o0 = {u: _dot(jnp.concatenate([bf(n_b[u]), bf(n_ks[u])], axis=1),
                      jnp.concatenate([bdu[u], adv[u]], axis=0)) for u in units}
        sfx_t = {u: jnp.concatenate([setups[u[0]][2][(2 * u[1] + i) * HEAD:(2 * u[1] + i + 1) * HEAD, :]
                                     for i in range(2)], axis=1)
                 for u in units}
        gv = {u: jnp.concatenate(
            [jnp.concatenate([bdg[u][:CHUNK], bdu[u][:CHUNK]], axis=1),
             jnp.concatenate([zc, adv[u][CHUNK:]], axis=1),
             jnp.concatenate([bdg[u][CHUNK:], bdu[u][CHUNK:]], axis=1),
             jnp.concatenate([zc, adv[u][:CHUNK]], axis=1)], axis=0) for u in units}
        sg = {u: _dot(sfx_t[u], gv[u]) for u in units}
        qphi = {u: stack(q[u], eye2 * cut(setups[u[0]][4], u[1]) - sg[u][:, :pw]) for u in units}
        opsi = {u: jnp.concatenate([o0[u], sg[u][:, pw:]], axis=0) for u in units}
        return qphi, opsi

    qphi, opsi = chunk_units({s: chunk_setup(s) for s in range(ns)})
    st = [st_scr[pr] for pr in pairs]
    for s in range(ns):
        os = [_dot(qphi[s, pr], bd(bf(st[pr]))) + opsi[s, pr] for pr in pairs]
        for pr in pairs:
            o_scr[s * CHUNK:(s + 1) * CHUNK, pr * pw:(pr + 1) * pw] = os[pr][:CHUNK]
        st = [os[pr][CHUNK:] for pr in pairs]
        gate_slice(s, 1)
    for pr in pairs:
        st_scr[pr] = st[pr]

    y_a = _rwkv_post(o_scr[...], _rwkv_bonus(r, k2, v, w), g, w)
    x2_ref[...] = _merge_out(x1_ref[...], y_a, ga_scr[...], gb_scr[...], w)

    @pl.when(c == pl.num_programs(1) - 1)
    def _():
        shift_ref[...] = f[tc - 1:tc, :]
        for hd in range(HEADS):
            wkv_ref[hd] = st_scr[hd // 2][:, (hd % 2) * HEAD:(hd % 2 + 1) * HEAD].T


def _mixer_prompt(h, x1, yb, wts, tri, mask, *, tc):
    bsz, seq, d = h.shape
    assert seq % tc == 0 and tc % CHUNK == 0
    nf = wts['w_f'].shape[1]
    wlist = [wts[n] for n in _W_NAMES]
    tok = lambda width: pl.BlockSpec((None, tc, width), lambda b, c: (b, c, 0))
    return pl.pallas_call(
        functools.partial(_mixer_prompt_body, tc=tc),
        grid=(bsz, seq // tc),
        in_specs=[tok(d), tok(d), tok(RW)] + [_const_spec(a.shape) for a in wlist]
                 + [_const_spec(tri.shape), _const_spec(mask.shape)],
        out_specs=[tok(d),
                   pl.BlockSpec((None, 1, nf), lambda b, c: (b, 0, 0)),
                   pl.BlockSpec((None, HEADS, HEAD, HEAD), lambda b, c: (b, 0, 0, 0))],
        out_shape=[jax.ShapeDtypeStruct((bsz, seq, d), F32),
                   jax.ShapeDtypeStruct((bsz, 1, nf), F32),
                   jax.ShapeDtypeStruct((bsz, HEADS, HEAD, HEAD), F32)],
        scratch_shapes=[pltpu.VMEM((1, nf), F32),
                        pltpu.VMEM((PAIRS, HEAD, 2 * HEAD), F32),
                        pltpu.VMEM((tc, RW), F32),
                        pltpu.VMEM((tc, d), F32), pltpu.VMEM((tc, d), F32)],
        compiler_params=_cparams(2),
        name="mixer_prompt",
    )(h, x1, yb, *wlist, tri, mask)


def _mixer_sample_body(*refs):
    h_ref, x1_ref, yb_ref, shift_in_ref, s_ref = refs[0:5]
    nwt = len(_W_NAMES)
    w = dict(zip(_W_NAMES, refs[5:5 + nwt]))
    x2_ref, shift_ref, s_out_ref = refs[5 + nwt:8 + nwt]
    t_scr, row_scr, ot_scr = refs[8 + nwt:]
    hd = pl.program_id(0)

    @pl.when(hd == 0)
    def _():
        f = _dot(h_ref[...], w['w_f'][...])
        fm = f + w['mu'][...] * (shift_in_ref[...] - f)
        shift_ref[...] = f
        r, logw, k2, v, kk, b, g = _rwkv_pre(fm, w)
        for i, a in enumerate((kk, jnp.exp(logw), b, k2, r, v)):
            t_scr[i] = a.T
        for i, a in enumerate((r, k2, v, g)):
            row_scr[i] = a

    off = pl.multiple_of(hd * HEAD, HEAD)
    hs = pl.ds(off, HEAD)
    kk_t, dec_t, b_t, k_t, r_t = (t_scr[i, hs, :] for i in range(5))

    def v_group(vg, carry):
        v0 = pl.multiple_of(vg * 8, 8)
        s8 = s_ref[pl.ds(v0, 8)]
        v8 = t_scr[5, pl.ds(off + v0, 8), :]
        o_rows = []
        for j in range(8):
            sa = -jnp.sum(s8[j] * kk_t, axis=0, keepdims=True)
            sn = s8[j] * dec_t + sa * b_t + v8[j:j + 1, :] * k_t
            s_out_ref[v0 + j] = sn
            o_rows.append(jnp.sum(sn * r_t, axis=0, keepdims=True))
        ot_scr[pl.ds(off + v0, 8), :] = jnp.concatenate(o_rows, axis=0)
        return carry

    lax.fori_loop(0, HEAD // 8, v_group, 0)

    @pl.when(hd == pl.num_programs(0) - 1)
    def _():
        r, k2, v, g = (row_scr[i] for i in range(4))
        y_a = _rwkv_post(ot_scr[...].T, _rwkv_bonus(r, k2, v, w), g, w)
        gate_a, gated_b = _merge_gates(h_ref[...], yb_ref[...], w)
        x2_ref[...] = _merge_out(x1_ref[...], y_a, gate_a, gated_b, w)


def _mixer_sample(h, x1, yb, shift_in, wkv_t, wts):
    n, d = h.shape
    nf = wts['w_f'].shape[1]
    wlist = [wts[nm] for nm in _W_NAMES]
    st_spec = pl.BlockSpec((None, HEAD, HEAD, n), lambda hd: (hd, 0, 0, 0))
    return pl.pallas_call(
        _mixer_sample_body,
        grid=(HEADS,),
        in_specs=[_const_spec(a.shape) for a in (h, x1, yb, shift_in)] + [st_spec]
                 + [_const_spec(a.shape) for a in wlist],
        out_specs=[pl.BlockSpec((n, d), lambda hd: (0, 0)),
                   pl.BlockSpec((n, nf), lambda hd: (0, 0)), st_spec],
        out_shape=[jax.ShapeDtypeStruct((n, d), F32),
                   jax.ShapeDtypeStruct((n, nf), F32),
                   jax.ShapeDtypeStruct(wkv_t.shape, F32)],
        scratch_shapes=[pltpu.VMEM((6, RW, n), F32), pltpu.VMEM((4, n, RW), F32),
                        pltpu.VMEM((RW, n), F32)],
        compiler_params=_cparams(1),
        name="mixer_sample",
    )(h, x1, yb, shift_in, wkv_t, *wlist)


def _chunk_constants():
    t = jnp.arange(CHUNK)
    incl = (t[:, None] >= t[None, :])
    strict = (t[:, None] > t[None, :])
    tri = incl.astype(BF16)
    top = jnp.concatenate([strict, strict], axis=1)
    bot = jnp.concatenate([incl, incl], axis=1)
    mask = jnp.concatenate([top, bot], axis=0).astype(F32)
    return tri, mask


def kernel(x_prompt, x_sample, state_shift, state_wkv, state_s5_re, state_s5_im, g_ffn1, ffn1_gate, ffn1_up, ffn1_down, g_mix, w_in, mu_shift, w0, w_up, a0, a_up, g_up, k_k, k_a, r_k, lnx_w, lnx_b, A_re, A_im, log_dt, B_re, B_im, C_re, C_im, D_skip, w_glu, w_a_up, w_b_up, w_out, g_ffn2, ffn2_gate, ffn2_up, ffn2_down, g_final):
    depth = g_ffn1.shape[0]
    assert depth == 1
    bp, seq, d = x_prompt.shape
    bs = x_sample.shape[0]
    assert x_sample.shape[1] == 1
    ncols_f = mu_shift.shape[1]
    su = D_skip.shape[1]
    groups, pstate = A_re.shape[1:]
    nstate = groups * pstate
    bf = lambda a: a.astype(BF16)
    row = lambda a: a.reshape(1, -1).astype(F32)

    w_in0 = w_in[0]
    eye_h = jnp.eye(HEADS, dtype=F32)
    wts = {
        'w_f': bf(w_in0[:, :ncols_f]),
        'w_gh': bf(0.5 * w_in0[:, ncols_f + su:]),
        'mu': row(mu_shift[0]), 'w0': row(w0[0]), 'w_up': bf(w_up[0]), 'a0': row(a0[0]),
        'a_up': bf(a_up[0]), 'g_up': bf(g_up[0]), 'k_k': row(k_k[0]), 'k_a': row(k_a[0]),
        'r_k': row(r_k[0]), 'lnx_w': row(lnx_w[0]), 'lnx_b': row(lnx_b[0]),
        'e': bf(jnp.kron(eye_h, jnp.ones((HEAD, HEAD), F32))),
        'w_a_up': bf(w_a_up[0]), 'w_b_up': bf(w_b_up[0]), 'w_out': bf(w_out[0]),
    }
    w_u = bf(w_in0[:, ncols_f:ncols_f + su])
    abr, abi, bbr, bbi = _s5_params(A_re[0], A_im[0], log_dt[0],
                                    jnp.swapaxes(B_re[0], 1, 2), jnp.swapaxes(B_im[0], 1, 2))
    bb = bf(jnp.concatenate([_block_diag(bbr, S5_BLOCKS), _block_diag(bbi, S5_BLOCKS)], axis=2))
    cm = bf(jnp.concatenate([_block_diag(jnp.swapaxes(C_re[0], 1, 2), S5_BLOCKS),
                             _block_diag(-jnp.swapaxes(C_im[0], 1, 2), S5_BLOCKS)], axis=1))
    abr = abr.reshape(1, nstate)
    abi = abi.reshape(1, nstate)
    dsk = row(D_skip[0])
    wglu = bf(w_glu[0])
    ffn1 = (row(g_ffn1[0]), ffn1_gate[0], ffn1_up[0], ffn1_down[0], row(g_mix[0]))
    ffn2 = (row(g_ffn2[0]), ffn2_gate[0], ffn2_up[0], ffn2_down[0], row(g_final))
    fc = 2 * MXU_TILE_V7X
    tri, mask = _chunk_constants()

    x1p, hnp, x1s, hns = _ffn(x_prompt.reshape(bp * seq, d), x_sample.reshape(bs, d), *ffn1,
                              final=False, tm=512, fc=fc)

    hnp = hnp.reshape(bp, seq, d)
    zeros_state = jnp.zeros((bp, nstate), F32)
    yb_p, s5r_p, s5i_p = _s5(hnp, w_u, bb, cm, abr, abi, dsk, wglu, zeros_state, zeros_state,
                             nb=bp, steps=32, sub=4)
    x2p, shift_p, wkv_p = _mixer_prompt(hnp, x1p.reshape(bp, seq, d), yb_p, wts, tri, mask, tc=512)

    yb_s, s5r_s, s5i_s = _s5(hns[None], w_u, bb, cm, abr, abi, dsk, wglu,
                             state_s5_re[0].reshape(bs, nstate), state_s5_im[0].reshape(bs, nstate),
                             nb=bs, steps=1)
    wkv_t = jnp.transpose(state_wkv[0], (1, 2, 3, 0))
    x2s, shift_s, wkv_ts = _mixer_sample(hns, x1s, yb_s[0], state_shift[0], wkv_t, wts)
    wkv_s = jnp.transpose(wkv_ts, (3, 0, 1, 2))

    y_prompt, y_sample = _ffn(x2p.reshape(bp * seq, d), x2s, *ffn2, final=True, tm=512, fc=fc)
    y_prompt = y_prompt.reshape(bp, seq, d)
    y_sample = y_sample.reshape(bs, 1, d)

    st5 = lambda a, n: a.reshape(1, n, groups, pstate)
    return (y_prompt, y_sample,
            shift_p.reshape(1, bp, ncols_f), wkv_p[None], st5(s5r_p, bp), st5(s5i_p, bp),
            shift_s[None], wkv_s[None], st5(s5r_s, bs), st5(s5i_s, bs))
```

```python
import functools
import math

import jax
import jax.numpy as jnp
from jax import lax
from jax.experimental import pallas as pl
from jax.experimental.pallas import tpu as pltpu

F32 = jnp.float32
BF16 = jnp.bfloat16

NORM_EPS = 1e-6
LNX_EPS = 64e-5
HEAD = 64
HEADS = 8
PAIRS = HEADS // 2
RW = HEAD * HEADS
S5_BLOCKS = 4
CHUNK = 64
VMEM_LIMIT = 56 * 1024 * 1024
MXU_TILE_V7X = 256


def _dot(a, b):
    return jnp.dot(a, b, preferred_element_type=F32)


def _dot_nt(a, b):
    return lax.dot_general(a, b, (((1,), (1,)), ((), ())), preferred_element_type=F32)


def _rms(x, g):
    ms = jnp.mean(x * x, axis=-1, keepdims=True)
    return x * lax.rsqrt(ms + NORM_EPS) * g


def _const_spec(shape):
    nd = len(shape)
    return pl.BlockSpec(shape, lambda *_: (0,) * nd, pipeline_mode=pl.Buffered(1))


def _cparams(ngrid):
    return pltpu.CompilerParams(dimension_semantics=("arbitrary",) * ngrid,
                                vmem_limit_bytes=VMEM_LIMIT)


def _ffn_body(x_ref, xs_ref, g_ref, wg_ref, wu_ref, wd_ref, gn_ref, *outs, final, fc):
    nout = len(outs) // 2

    def run(src_ref, dst):
        x = src_ref[...]
        hn = _rms(x, g_ref[...]).astype(BF16)
        acc = None
        f = wg_ref.shape[1]
        for c0 in range(0, f, fc):
            c1 = min(c0 + fc, f)
            gate = _dot(hn, wg_ref[:, c0:c1].astype(BF16))
            up = _dot(hn, wu_ref[:, c0:c1].astype(BF16))
            act = (jax.nn.silu(gate) * up).astype(BF16)
            part = _dot(act, wd_ref[c0:c1, :].astype(BF16))
            acc = part if acc is None else acc + part
        x1 = x + 0.5 * acc
        if final:
            dst[0][...] = _rms(x1, gn_ref[...])
        else:
            dst[0][...] = x1
            dst[1][...] = _rms(x1, gn_ref[...]).astype(BF16)

    last = pl.num_programs(0) - 1
    pl.when(pl.program_id(0) < last)(functools.partial(run, x_ref, outs[:nout]))
    pl.when(pl.program_id(0) == last)(functools.partial(run, xs_ref, outs[nout:]))


def _ffn(x, xs, g, wg, wu, wd, gn, *, final, tm, fc):
    m, d = x.shape
    assert m % tm == 0 and fc % MXU_TILE_V7X == 0
    nt = m // tm
    tile = pl.BlockSpec((tm, d), lambda i: (jnp.minimum(i, nt - 1), 0))
    whole = pl.BlockSpec(xs.shape, lambda i: (0, 0))
    dts = (F32,) if final else (F32, BF16)
    return pl.pallas_call(
        functools.partial(_ffn_body, final=final, fc=fc),
        grid=(nt + 1,),
        in_specs=[tile, _const_spec(xs.shape), _const_spec(g.shape), _const_spec(wg.shape),
                  _const_spec(wu.shape), _const_spec(wd.shape), _const_spec(gn.shape)],
        out_specs=[tile] * len(dts) + [whole] * len(dts),
        out_shape=[jax.ShapeDtypeStruct(x.shape, dt) for dt in dts]
                  + [jax.ShapeDtypeStruct(xs.shape, dt) for dt in dts],
        compiler_params=_cparams(1),
        name="ffn_final" if final else "ffn_in",
    )(x, xs, g, wg, wu, wd, gn)


def _s5_param_body(are_ref, aim_ref, ldt_ref, bre_ref, bim_ref, abr_ref, abi_ref, bbr_ref, bbi_ref):
    lam_r = are_ref[...]
    lam_i = aim_ref[...]
    dt = jnp.exp(ldt_ref[...])
    mag = jnp.exp(lam_r * dt)
    ab_r = mag * jnp.cos(lam_i * dt)
    ab_i = mag * jnp.sin(lam_i * dt)
    den = lam_r * lam_r + lam_i * lam_i
    q_r = ((ab_r - 1.0) * lam_r + ab_i * lam_i) / den
    q_i = (ab_i * lam_r - (ab_r - 1.0) * lam_i) / den
    abr_ref[...] = ab_r
    abi_ref[...] = ab_i
    bre = bre_ref[...]
    bim = bim_ref[...]
    bbr_ref[...] = q_r[:, None, :] * bre - q_i[:, None, :] * bim
    bbi_ref[...] = q_r[:, None, :] * bim + q_i[:, None, :] * bre


def _s5_params(a_re, a_im, log_dt, b_re_t, b_im_t):
    g, p = a_re.shape
    hg = b_re_t.shape[1]
    return pl.pallas_call(
        _s5_param_body,
        out_shape=[jax.ShapeDtypeStruct((g, p), F32), jax.ShapeDtypeStruct((g, p), F32),
                   jax.ShapeDtypeStruct((g, hg, p), F32), jax.ShapeDtypeStruct((g, hg, p), F32)],
        name="s5_params",
    )(a_re, a_im, log_dt.reshape(g, 1), b_re_t, b_im_t)


def _block_diag(w, nblk):
    g, a, b = w.shape
    gb = g // nblk
    w4 = w.reshape(nblk, gb, a, b)
    eye = jnp.eye(gb, dtype=w.dtype)
    return jnp.einsum('jgab,gk->jgakb', w4, eye).reshape(nblk, gb * a, gb * b)


def _s5_body(h_ref, perm_ref, wu_ref, bb_ref, cm_ref, abr_ref, abi_ref, dsk_ref, wglu_ref, x0r_ref,
             x0i_ref, yb_ref, xr_out, xi_out, bu_scr, xr_scr, xi_scr, *, nb, steps, sub):
    c = pl.program_id(0)
    rows = nb * steps
    blk = h_ref.shape[1] // sub

    @pl.when(c == 0)
    def _():
        xr_scr[...] = x0r_ref[...]
        xi_scr[...] = x0i_ref[...]

    nin = wu_ref.shape[1] // S5_BLOCKS
    ns = abr_ref.shape[1] // S5_BLOCKS
    us = []
    for i in range(sub):
        h = h_ref[:, i * blk:(i + 1) * blk, :].reshape(rows, h_ref.shape[-1])
        if steps > 1:
            h = _dot(perm_ref[0], h).astype(BF16)
        u = _dot(h, wu_ref[...])
        us.append(u)
        ub = u.astype(BF16)
        for j in range(S5_BLOCKS):
            bu_scr[i, j] = _dot(ub[:, j * nin:(j + 1) * nin], bb_ref[j])
    xr = [xr_scr[:, j * ns:(j + 1) * ns] for j in range(S5_BLOCKS)]
    xi = [xi_scr[:, j * ns:(j + 1) * ns] for j in range(S5_BLOCKS)]
    for i in range(sub):
        ys = []
        for j in range(S5_BLOCKS):
            ar = jnp.broadcast_to(abr_ref[:, j * ns:(j + 1) * ns], (nb, ns))
            ai = jnp.broadcast_to(abi_ref[:, j * ns:(j + 1) * ns], (nb, ns))
            for t in range(steps):
                tr = slice(t * nb, (t + 1) * nb)
                xr[j], xi[j] = (ar * xr[j] - ai * xi[j] + bu_scr[i, j, tr, 0:ns],
                                ar * xi[j] + ai * xr[j] + bu_scr[i, j, tr, ns:2 * ns])
                bu_scr[i, j, tr, 0:ns] = xr[j]
                bu_scr[i, j, tr, ns:2 * ns] = xi[j]
            ys.append(_dot(bu_scr[i, j].astype(BF16), cm_ref[j]))
        y = jnp.concatenate(ys, axis=1) + dsk_ref[...] * us[i]
        z = jax.nn.gelu(y)
        yb = (z * jax.nn.sigmoid(_dot(z.astype(BF16), wglu_ref[...]))).astype(BF16)
        if steps > 1:
            yb = _dot(perm_ref[1], yb).astype(BF16)
        yb_ref[:, i * blk:(i + 1) * blk, :] = yb.reshape(yb_ref.shape[0], blk, yb_ref.shape[2])
    for j in range(S5_BLOCKS):
        xr_scr[:, j * ns:(j + 1) * ns] = xr[j]
        xi_scr[:, j * ns:(j + 1) * ns] = xi[j]
    xr_out[...] = xr_scr[...]
    xi_out[...] = xi_scr[...]


def _s5(h, w_u, bb, cm, abr, abi, dsk, wglu, x0r, x0i, *, nb, steps, sub=1):
    g, seq, d = h.shape
    rows = nb * steps
    blk = sub * rows // g
    assert seq % blk == 0 and (g == nb or steps == 1)
    su = w_u.shape[1]
    nst = abr.shape[1]
    r = jnp.arange(rows)
    to_tm = (r[:, None] % nb) * steps + r[:, None] // nb == r[None, :]
    perm = jnp.stack([to_tm, to_tm.T]).astype(BF16)
    return pl.pallas_call(
        functools.partial(_s5_body, nb=nb, steps=steps, sub=sub),
        grid=(seq // blk,),
        in_specs=[
            pl.BlockSpec((g, blk, d), lambda c: (0, c, 0)),
            _const_spec(perm.shape),
            _const_spec(w_u.shape), _const_spec(bb.shape), _const_spec(cm.shape),
            _const_spec(abr.shape), _const_spec(abi.shape), _const_spec(dsk.shape),
            _const_spec(wglu.shape), _const_spec(x0r.shape), _const_spec(x0i.shape),
        ],
        out_specs=[pl.BlockSpec((g, blk, su), lambda c: (0, c, 0)),
                   pl.BlockSpec((nb, nst), lambda c: (0, 0)),
                   pl.BlockSpec((nb, nst), lambda c: (0, 0))],
        out_shape=[jax.ShapeDtypeStruct((g, seq, su), BF16),
                   jax.ShapeDtypeStruct((nb, nst), F32),
                   jax.ShapeDtypeStruct((nb, nst), F32)],
        scratch_shapes=[pltpu.VMEM((sub, S5_BLOCKS, rows, 2 * nst // S5_BLOCKS), F32),
                        pltpu.VMEM((nb, nst), F32), pltpu.VMEM((nb, nst), F32)],
        compiler_params=_cparams(1),
        name="s5_branch",
    )(h, perm, w_u, bb, cm, abr, abi, dsk, wglu, x0r, x0i)


def _headsum(x, e):
    xb = x.astype(BF16)
    t = MXU_TILE_V7X
    return jnp.concatenate([_dot(xb[:, c:c + t], e[c:c + t, c:c + t]) for c in range(0, RW, t)], axis=1)


def _split3(x):
    hi = x.astype(BF16)
    r1 = x - hi.astype(F32)
    mid = r1.astype(BF16)
    lo = (r1 - mid.astype(F32)).astype(BF16)
    return hi, mid, lo


def _rwkv_pre(fm, w):
    r = fm[:, 0:RW]
    k = fm[:, RW:2 * RW]
    v = fm[:, 2 * RW:3 * RW]
    o0 = 3 * RW
    nw = w['w_up'].shape[0]
    na = w['a_up'].shape[0]
    wd = fm[:, o0:o0 + nw]
    ad = fm[:, o0 + nw:o0 + nw + na]
    gd = fm[:, o0 + nw + na:]
    logw = -math.exp(-0.5) * _sigmoid(w['w0'][...] + _dot(jnp.tanh(wd).astype(BF16), w['w_up'][...]))
    a = _sigmoid(w['a0'][...] + _dot(ad.astype(BF16), w['a_up'][...]))
    g = _dot(_sigmoid(gd).astype(BF16), w['g_up'][...])
    kk = k * w['k_k'][...]
    kk = kk * lax.rsqrt(jnp.maximum(_headsum(kk * kk, w['e']), 1e-24))
    k2 = k * (1.0 + (a - 1.0) * w['k_a'][...])
    return r, logw, k2, v, kk, kk * a, g


def _rwkv_bonus(r, k2, v, w):
    return _headsum(r * k2 * w['r_k'][...], w['e']) * v


def _rwkv_post(o, bonus, g, w):
    inv_n = 1.0 / HEAD
    mu = _headsum(o, w['e']) * inv_n
    d = o - mu
    var = _headsum(d * d, w['e']) * inv_n
    on = d * lax.rsqrt(var + LNX_EPS) * w['lnx_w'][...] + w['lnx_b'][...]
    return (on + bonus) * g


def _sigmoid_of_twice(xh):
    return 0.5 * jnp.tanh(xh) + 0.5


def _sigmoid(x):
    return _sigmoid_of_twice(0.5 * x)


def _merge_gates(h, y_b, w, c0=0, c1=None):
    dm = w['w_gh'].shape[1] // 2
    c1 = dm if c1 is None else c1
    gate_a = _sigmoid_of_twice(_dot(h, w['w_gh'][:, c0:c1]))
    gate_b = _sigmoid_of_twice(_dot(h, w['w_gh'][:, dm + c0:dm + c1]))
    return gate_a, gate_b * _dot(y_b.astype(BF16), w['w_b_up'][:, c0:c1])


def _merge_out(x1, y_a, gate_a, gated_b, w):
    merged = gate_a * _dot(y_a.astype(BF16), w['w_a_up'][...]) + gated_b
    return x1 + _dot(merged.astype(BF16), w['w_out'][...])


_W_NAMES = ('w_f', 'w_gh', 'mu', 'w0', 'w_up', 'a0', 'a_up', 'g_up', 'k_k', 'k_a', 'r_k',
            'lnx_w', 'lnx_b', 'e', 'w_a_up', 'w_b_up', 'w_out')


def _mixer_prompt_body(*refs, tc):
    h_ref, x1_ref, yb_ref = refs[0:3]
    nwt = len(_W_NAMES)
    w = dict(zip(_W_NAMES, refs[3:3 + nwt]))
    tri_ref, mask_ref = refs[3 + nwt:5 + nwt]
    x2_ref, shift_ref, wkv_ref = refs[5 + nwt:8 + nwt]
    f_scr, st_scr, o_scr, ga_scr, gb_scr = refs[8 + nwt:]
    c = pl.program_id(1)

    @pl.when(c == 0)
    def _():
        f_scr[...] = jnp.zeros_like(f_scr)
        st_scr[...] = jnp.zeros_like(st_scr)

    h = h_ref[...]
    f = _dot(h, w['w_f'][...])
    row0 = lax.broadcasted_iota(jnp.int32, f.shape, 0) == 0
    shifted = jnp.where(row0, f_scr[...], pltpu.roll(f, 1, 0))
    f_scr[...] = f[tc - 1:tc, :]
    fm = f + w['mu'][...] * (shifted - f)
    r, logw, k2, v, kk, b, g = _rwkv_pre(fm, w)

    ns = tc // CHUNK
    dm = x1_ref.shape[1]
    yb = yb_ref[...]
    tri = tri_ref[...]
    mask = mask_ref[...] > 0.5
    pw = 2 * HEAD
    lane = lambda rows: lax.broadcasted_iota(jnp.int32, (rows, pw), 1)
    low = lambda a: lane(a.shape[0]) < HEAD
    eye2 = (lax.broadcasted_iota(jnp.int32, (CHUNK, pw), 0) == lane(CHUNK) % HEAD).astype(F32)
    n_double = CHUNK.bit_length() - 2
    pairs = range(PAIRS)
    first = lambda a: jnp.where(low(a), a, jnp.zeros_like(a))
    second = lambda a: jnp.where(low(a), jnp.zeros_like(a), a)
    bd = lambda y: jnp.concatenate([first(y), second(y)], axis=0)
    ad = lambda y: jnp.concatenate([second(y), first(y)], axis=0)
    pick = lambda x, y: jnp.where(low(x), x, y)
    stack = lambda top, bot: jnp.concatenate([top.astype(BF16), bot.astype(BF16)], axis=0)
    bf = lambda a: a.astype(BF16)
    zc = jnp.zeros((CHUNK, pw), BF16)

    def gate_slice(s, phase):
        per_phase = dm // MXU_TILE_V7X // 2
        every = ns // per_phase
        if s % every == 0:
            c0 = (phase * per_phase + s // every) * MXU_TILE_V7X
            ga_scr[:, c0:c0 + MXU_TILE_V7X], gb_scr[:, c0:c0 + MXU_TILE_V7X] = _merge_gates(
                h, yb, w, c0, c0 + MXU_TILE_V7X)

    def chunk_setup(s):
        sl = slice(s * CHUNK, (s + 1) * CHUNK)
        lw = logw[sl]
        hi, mid, lo = _split3(lw)
        lpre = _dot(tri, jnp.concatenate([hi, mid, lo], axis=0))
        lsuf = lpre[CHUNK - 1:CHUNK, :] - lpre
        e_pre = jnp.exp(lpre)
        e_neg = jnp.exp(-lpre)
        e_suf = jnp.exp(lsuf)
        lhs = jnp.concatenate([kk[sl] * jnp.exp(lpre - lw), r[sl] * e_pre], axis=0).astype(BF16)
        rhs = jnp.concatenate([b[sl] * e_neg, k2[sl] * e_neg], axis=0).astype(BF16)
        sfx_t = jnp.concatenate([b[sl] * e_suf, k2[sl] * e_suf], axis=0).T.astype(BF16)
        gate_slice(s, 0)
        return lhs, rhs, sfx_t, v[sl].astype(BF16), e_pre[CHUNK - 1:CHUNK, :]

    def chunk_units(setups):
        units = [(s, pr) for s in setups for pr in pairs]
        cut = lambda a, pr: a[:, pr * pw:(pr + 1) * pw]
        lhs = {u: cut(setups[u[0]][0], u[1]) for u in units}
        rhs = {u: cut(setups[u[0]][1], u[1]) for u in units}
        sc = {u: _dot_nt(lhs[u], jnp.concatenate(
            [first(rhs[u]), second(rhs[u][CHUNK:]), second(rhs[u][:CHUNK])], axis=0)) for u in units}
        a0 = {u: jnp.where(mask, sc[u][:, :pw], 0.0) for u in units}
        a1 = {u: jnp.where(mask, sc[u][:, pw:], 0.0) for u in units}
        m_b = {u: pick(a0[u][:CHUNK], a1[u][:CHUNK]) for u in units}
        m_ks = {u: pick(a1[u][:CHUNK], a0[u][:CHUNK]) for u in units}
        n_b = {u: pick(a0[u][CHUNK:], a1[u][CHUNK:]) for u in units}
        n_ks = {u: pick(a1[u][CHUNK:], a0[u][CHUNK:]) for u in units}
        p = {u: -m_b[u] for u in units}
        winv = {u: eye2 + p[u] for u in units}
        p = {u: _dot(bf(p[u]), bd(bf(p[u]))) for u in units}
        for _ in range(n_double - 1):
            wp = {u: _dot(stack(winv[u], p[u]), bd(bf(p[u]))) for u in units}
            winv = {u: winv[u] + wp[u][:CHUNK] for u in units}
            p = {u: wp[u][CHUNK:] for u in units}
        wb = {u: bf(winv[u] + _dot(bf(winv[u]), bd(bf(p[u])))) for u in units}
        adv = {u: ad(cut(setups[u[0]][3], u[1])) for u in units}
        mkv = {u: bf(_dot(bf(m_ks[u]), adv[u])) for u in units}
        gu = {u: _dot(wb[u], jnp.concatenate([bd(lhs[u][:CHUNK]), bd(mkv[u])], axis=1)) for u in units}
        bdg = {u: bd(bf(gu[u][:, :pw])) for u in units}
        bdu = {u: bd(bf(-gu[u][:, pw:])) for u in units}
        q = {u: lhs[u][CHUNK:].astype(F32) - _dot(bf(n_b[u]), bdg[u]) for u in units}
        o0 = {u: _dot(jnp.concatenate([bf(n_b[u]), bf(n_ks[u])], axis=1),
                      jnp.concatenate([bdu[u], adv[u]], axis=0)) for u in units}
        sfx_t = {u: jnp.concatenate([setups[u[0]][2][(2 * u[1] + i) * HEAD:(2 * u[1] + i + 1) * HEAD, :]
                                     for i in range(2)], axis=1)
                 for u in units}
        gv = {u: jnp.concatenate(
            [jnp.concatenate([bdg[u][:CHUNK], bdu[u][:CHUNK]], axis=1),
             jnp.concatenate([zc, adv[u][CHUNK:]], axis=1),
             jnp.concatenate([bdg[u][CHUNK:], bdu[u][CHUNK:]], axis=1),
             jnp.concatenate([zc, adv[u][:CHUNK]], axis=1)], axis=0) for u in units}
        sg = {u: _dot(sfx_t[u], gv[u]) for u in units}
        qphi = {u: stack(q[u], eye2 * cut(setups[u[0]][4], u[1]) - sg[u][:, :pw]) for u in units}
        opsi = {u: jnp.concatenate([o0[u], sg[u][:, pw:]], axis=0) for u in units}
        return qphi, opsi

    qphi, opsi = chunk_units({s: chunk_setup(s) for s in range(ns)})
    st = [st_scr[pr] for pr in pairs]
    for s in range(ns):
        os = [_dot(qphi[s, pr], bd(bf(st[pr]))) + opsi[s, pr] for pr in pairs]
        for pr in pairs:
            o_scr[s * CHUNK:(s + 1) * CHUNK, pr * pw:(pr + 1) * pw] = os[pr][:CHUNK]
        st = [os[pr][CHUNK:] for pr in pairs]
        gate_slice(s, 1)
    for pr in pairs:
        st_scr[pr] = st[pr]

    y_a = _rwkv_post(o_scr[...], _rwkv_bonus(r, k2, v, w), g, w)
    x2_ref[...] = _merge_out(x1_ref[...], y_a, ga_scr[...], gb_scr[...], w)

    @pl.when(c == pl.num_programs(1) - 1)
    def _():
        shift_ref[...] = f[tc - 1:tc, :]
        for hd in range(HEADS):
            wkv_ref[hd] = st_scr[hd // 2][:, (hd % 2) * HEAD:(hd % 2 + 1) * HEAD].T


def _mixer_prompt(h, x1, yb, wts, tri, mask, *, tc):
    bsz, seq, d = h.shape
    assert seq % tc == 0 and tc % CHUNK == 0
    nf = wts['w_f'].shape[1]
    wlist = [wts[n] for n in _W_NAMES]
    tok = lambda width: pl.BlockSpec((None, tc, width), lambda b, c: (b, c, 0))
    return pl.pallas_call(
        functools.partial(_mixer_prompt_body, tc=tc),
        grid=(bsz, seq // tc),
        in_specs=[tok(d), tok(d), tok(RW)] + [_const_spec(a.shape) for a in wlist]
                 + [_const_spec(tri.shape), _const_spec(mask.shape)],
        out_specs=[tok(d),
                   pl.BlockSpec((None, 1, nf), lambda b, c: (b, 0, 0)),
                   pl.BlockSpec((None, HEADS, HEAD, HEAD), lambda b, c: (b, 0, 0, 0))],
        out_shape=[jax.ShapeDtypeStruct((bsz, seq, d), F32),
                   jax.ShapeDtypeStruct((bsz, 1, nf), F32),
                   jax.ShapeDtypeStruct((bsz, HEADS, HEAD, HEAD), F32)],
        scratch_shapes=[pltpu.VMEM((1, nf), F32),
                        pltpu.VMEM((PAIRS, HEAD, 2 * HEAD), F32),
                        pltpu.VMEM((tc, RW), F32),
                        pltpu.VMEM((tc, d), F32), pltpu.VMEM((tc, d), F32)],
        compiler_params=_cparams(2),
        name="mixer_prompt",
    )(h, x1, yb, *wlist, tri, mask)


def _mixer_sample_body(*refs):
    h_ref, x1_ref, yb_ref, shift_in_ref, s_ref = refs[0:5]
    nwt = len(_W_NAMES)
    w = dict(zip(_W_NAMES, refs[5:5 + nwt]))
    x2_ref, shift_ref, s_out_ref = refs[5 + nwt:8 + nwt]
    t_scr, row_scr, ot_scr = refs[8 + nwt:]
    hd = pl.program_id(0)

    @pl.when(hd == 0)
    def _():
        f = _dot(h_ref[...], w['w_f'][...])
        fm = f + w['mu'][...] * (shift_in_ref[...] - f)
        shift_ref[...] = f
        r, logw, k2, v, kk, b, g = _rwkv_pre(fm, w)
        for i, a in enumerate((kk, jnp.exp(logw), b, k2, r, v)):
            t_scr[i] = a.T
        for i, a in enumerate((r, k2, v, g)):
            row_scr[i] = a

    off = pl.multiple_of(hd * HEAD, HEAD)
    hs = pl.ds(off, HEAD)
    kk_t, dec_t, b_t, k_t, r_t = (t_scr[i, hs, :] for i in range(5))

    def v_group(vg, carry):
        v0 = pl.multiple_of(vg * 8, 8)
        s8 = s_ref[pl.ds(v0, 8)]
        v8 = t_scr[5, pl.ds(off + v0, 8), :]
        o_rows = []
        for j in range(8):
            sa = -jnp.sum(s8[j] * kk_t, axis=0, keepdims=True)
            sn = s8[j] * dec_t + sa * b_t + v8[j:j + 1, :] * k_t
            s_out_ref[v0 + j] = sn
            o_rows.append(jnp.sum(sn * r_t, axis=0, keepdims=True))
        ot_scr[pl.ds(off + v0, 8), :] = jnp.concatenate(o_rows, axis=0)
        return carry

    lax.fori_loop(0, HEAD // 8, v_group, 0)

    @pl.when(hd == pl.num_programs(0) - 1)
    def _():
        r, k2, v, g = (row_scr[i] for i in range(4))
        y_a = _rwkv_post(ot_scr[...].T, _rwkv_bonus(r, k2, v, w), g, w)
        gate_a, gated_b = _merge_gates(h_ref[...], yb_ref[...], w)
        x2_ref[...] = _merge_out(x1_ref[...], y_a, gate_a, gated_b, w)


def _mixer_sample(h, x1, yb, shift_in, wkv_t, wts):
    n, d = h.shape
    nf = wts['w_f'].shape[1]
    wlist = [wts[nm] for nm in _W_NAMES]
    st_spec = pl.BlockSpec((None, HEAD, HEAD, n), lambda hd: (hd, 0, 0, 0))
    return pl.pallas_call(
        _mixer_sample_body,
        grid=(HEADS,),
        in_specs=[_const_spec(a.shape) for a in (h, x1, yb, shift_in)] + [st_spec]
                 + [_const_spec(a.shape) for a in wlist],
        out_specs=[pl.BlockSpec((n, d), lambda hd: (0, 0)),
                   pl.BlockSpec((n, nf), lambda hd: (0, 0)), st_spec],
        out_shape=[jax.ShapeDtypeStruct((n, d), F32),
                   jax.ShapeDtypeStruct((n, nf), F32),
                   jax.ShapeDtypeStruct(wkv_t.shape, F32)],
        scratch_shapes=[pltpu.VMEM((6, RW, n), F32), pltpu.VMEM((4, n, RW), F32),
                        pltpu.VMEM((RW, n), F32)],
        compiler_params=_cparams(1),
        name="mixer_sample",
    )(h, x1, yb, shift_in, wkv_t, *wlist)


def _chunk_constants():
    t = jnp.arange(CHUNK)
    incl = (t[:, None] >= t[None, :])
    strict = (t[:, None] > t[None, :])
    tri = jnp.concatenate([incl, incl, incl], axis=1).astype(BF16)
    top = jnp.concatenate([strict, strict], axis=1)
    bot = jnp.concatenate([incl, incl], axis=1)
    mask = jnp.concatenate([top, bot], axis=0).astype(F32)
    return tri, mask


def kernel(x_prompt, x_sample, state_shift, state_wkv, state_s5_re, state_s5_im, g_ffn1, ffn1_gate, ffn1_up, ffn1_down, g_mix, w_in, mu_shift, w0, w_up, a0, a_up, g_up, k_k, k_a, r_k, lnx_w, lnx_b, A_re, A_im, log_dt, B_re, B_im, C_re, C_im, D_skip, w_glu, w_a_up, w_b_up, w_out, g_ffn2, ffn2_gate, ffn2_up, ffn2_down, g_final):
    depth = g_ffn1.shape[0]
    assert depth == 1
    bp, seq, d = x_prompt.shape
    bs = x_sample.shape[0]
    assert x_sample.shape[1] == 1
    ncols_f = mu_shift.shape[1]
    su = D_skip.shape[1]
    groups, pstate = A_re.shape[1:]
    nstate = groups * pstate
    bf = lambda a: a.astype(BF16)
    row = lambda a: a.reshape(1, -1).astype(F32)

    w_in0 = w_in[0]
    eye_h = jnp.eye(HEADS, dtype=F32)
    wts = {
        'w_f': bf(w_in0[:, :ncols_f]),
        'w_gh': bf(0.5 * w_in0[:, ncols_f + su:]),
        'mu': row(mu_shift[0]), 'w0': row(w0[0]), 'w_up': bf(w_up[0]), 'a0': row(a0[0]),
        'a_up': bf(a_up[0]), 'g_up': bf(g_up[0]), 'k_k': row(k_k[0]), 'k_a': row(k_a[0]),
        'r_k': row(r_k[0]), 'lnx_w': row(lnx_w[0]), 'lnx_b': row(lnx_b[0]),
        'e': bf(jnp.kron(eye_h, jnp.ones((HEAD, HEAD), F32))),
        'w_a_up': bf(w_a_up[0]), 'w_b_up': bf(w_b_up[0]), 'w_out': bf(w_out[0]),
    }
    w_u = bf(w_in0[:, ncols_f:ncols_f + su])
    abr, abi, bbr, bbi = _s5_params(A_re[0], A_im[0], log_dt[0],
                                    jnp.swapaxes(B_re[0], 1, 2), jnp.swapaxes(B_im[0], 1, 2))
    bb = bf(jnp.concatenate([_block_diag(bbr, S5_BLOCKS), _block_diag(bbi, S5_BLOCKS)], axis=2))
    cm = bf(jnp.concatenate([_block_diag(jnp.swapaxes(C_re[0], 1, 2), S5_BLOCKS),
                             _block_diag(-jnp.swapaxes(C_im[0], 1, 2), S5_BLOCKS)], axis=1))
    abr = abr.reshape(1, nstate)
    abi = abi.reshape(1, nstate)
    dsk = row(D_skip[0])
    wglu = bf(w_glu[0])
    ffn1 = (row(g_ffn1[0]), ffn1_gate[0], ffn1_up[0], ffn1_down[0], row(g_mix[0]))
    ffn2 = (row(g_ffn2[0]), ffn2_gate[0], ffn2_up[0], ffn2_down[0], row(g_final))
    fc = 2 * MXU_TILE_V7X
    tri, mask = _chunk_constants()

    x1p, hnp, x1s, hns = _ffn(x_prompt.reshape(bp * seq, d), x_sample.reshape(bs, d), *ffn1,
                              final=False, tm=512, fc=fc)

    hnp = hnp.reshape(bp, seq, d)
    zeros_state = jnp.zeros((bp, nstate), F32)
    yb_p, s5r_p, s5i_p = _s5(hnp, w_u, bb, cm, abr, abi, dsk, wglu, zeros_state, zeros_state,
                             nb=bp, steps=32, sub=4)
    x2p, shift_p, wkv_p = _mixer_prompt(hnp, x1p.reshape(bp, seq, d), yb_p, wts, tri, mask, tc=512)

    yb_s, s5r_s, s5i_s = _s5(hns[None], w_u, bb, cm, abr, abi, dsk, wglu,
                             state_s5_re[0].reshape(bs, nstate), state_s5_im[0].reshape(bs, nstate),
                             nb=bs, steps=1)
    wkv_t = jnp.transpose(state_wkv[0], (1, 2, 3, 0))
    x2s, shift_s, wkv_ts = _mixer_sample(hns, x1s, yb_s[0], state_shift[0], wkv_t, wts)
    wkv_s = jnp.transpose(wkv_ts, (3, 0, 1, 2))

    y_prompt, y_sample = _ffn(x2p.reshape(bp * seq, d), x2s, *ffn2, final=True, tm=512, fc=fc)
    y_prompt = y_prompt.reshape(bp, seq, d)
    y_sample = y_sample.reshape(bs, 1, d)

    st5 = lambda a, n: a.reshape(1, n, groups, pstate)
    return (y_prompt, y_sample,
            shift_p.reshape(1, bp, ncols_f), wkv_p[None], st5(s5r_p, bp), st5(s5i_p, bp),
            shift_s[None], wkv_s[None], st5(s5r_s, bs), st5(s5i_s, bs))
```
